```python
import math
import jax, jax.numpy as jnp
from jax import lax
import numpy as np

D_MODEL = 1024
BATCH = 4
SEQ = 4096
DEPTH = 4
DEC_BATCH = 32
DEC_SEQ = 4
PAST_LEN = 8192
PAGE_SIZE = 128

N_MIXERS = 2
N_GLA_LAYERS = (DEPTH + N_MIXERS - 1) // N_MIXERS
N_DIL_LAYERS = DEPTH // N_MIXERS
GLA_HEADS = 4
GLA_KD = D_MODEL // 2
GLA_VD = D_MODEL
GLA_DK = GLA_KD // GLA_HEADS
GLA_DV = GLA_VD // GLA_HEADS
GLA_GATE_RANK = 16
GLA_GATE_NORM = 16.0
GLA_CHUNK = 32
GLA_IN = 2 * GLA_KD + 2 * GLA_VD + GLA_GATE_RANK
DIL_GROUPS = ((128, 1), (512, 4), (2048, 16))
N_GROUPS = len(DIL_GROUPS)
DIL_HEADS = 16
DIL_HD = D_MODEL // DIL_HEADS
DIL_WIDTH = DIL_HEADS * DIL_HD
DIL_IN = N_GROUPS * 3 * DIL_WIDTH
DIL_BLOCK = 128
DIL_SCALE = DIL_HD ** -0.5
NUM_BUCKETS = 32
MAX_DISTANCE = 2048
D_FF = 2816
CONV_WIDTH = 3
EPS = 1e-6
NEG = -1e30

kernel_name = "hybrid_gla_dilated_swa_convffn_step"


def _rmsnorm(x, g):
    xf = x.astype(jnp.float32)
    y = xf * lax.rsqrt(jnp.mean(xf * xf, axis=-1, keepdims=True) + EPS)
    return (y * g.astype(jnp.float32)).astype(x.dtype)


def _rel_bucket(dist):
    max_exact = NUM_BUCKETS // 2
    df = jnp.maximum(dist, 1).astype(jnp.float32)
    large = max_exact + (jnp.log(df / max_exact) / math.log(MAX_DISTANCE / max_exact)
                         * (NUM_BUCKETS - max_exact)).astype(jnp.int32)
    large = jnp.minimum(large, NUM_BUCKETS - 1)
    return jnp.where(dist < max_exact, dist, large)


def _group_bias(rel_bias, g, d, steps):
    tab = rel_bias[:, g * DIL_HEADS:(g + 1) * DIL_HEADS].astype(jnp.float32)
    return jnp.moveaxis(tab[_rel_bucket(steps * d)], -1, 0)


def _gla_scan(q, k, v, g, s0):
    B, T, H = q.shape[:3]
    C = GLA_CHUNK
    n = -(-T // C)
    pad = n * C - T

    def prep(a):
        a = jnp.pad(a, ((0, 0), (0, pad), (0, 0), (0, 0)))
        return a.reshape(B, n, C, H, a.shape[-1]).transpose(1, 0, 3, 2, 4)

    causal = jnp.tril(jnp.ones((C, C), dtype=bool))
    mid = C // 2

    def step(S, inp):
        qc, kc, vc, gc = inp
        b = jnp.cumsum(gc, axis=2)
        ref = b[:, :, mid:mid + 1]
        a = jnp.einsum('bhid,bhjd->bhij', qc * jnp.exp(b - ref), kc * jnp.exp(ref - b))
        a = jnp.where(causal, a, 0.0)
        o = (jnp.einsum('bhij,bhje->bhie', a, vc)
             + jnp.einsum('bhid,bhde->bhie', qc * jnp.exp(b), S))
        b_last = b[:, :, -1:]
        S = (jnp.exp(b_last[:, :, 0])[..., None] * S
             + jnp.einsum('bhjd,bhje->bhde', kc * jnp.exp(b_last - b), vc))
        return S, o

    sT, o = lax.scan(step, s0, (prep(q), prep(k), prep(v), prep(g)))
    o = o.transpose(1, 0, 3, 2, 4).reshape(B, n * C, H, v.shape[-1])[:, :T]
    return o, sT


def _gla_mixer(h, w_in, w_g2, b_g, g_norm, w_out, s0):
    B, T, _ = h.shape
    p = h @ w_in
    q, k, v, r, gz = jnp.split(
        p, [GLA_KD, 2 * GLA_KD, 2 * GLA_KD + GLA_VD, 2 * GLA_KD + 2 * GLA_VD], axis=-1)
    heads = lambda a, e: a.reshape(B, T, GLA_HEADS, e).astype(jnp.float32)
    glog = jax.nn.log_sigmoid((gz @ w_g2 + b_g).astype(jnp.float32)) / GLA_GATE_NORM
    o, sT = _gla_scan(heads(q, GLA_DK) * (GLA_DK ** -0.5), heads(k, GLA_DK), heads(v, GLA_DV),
                      glog.reshape(B, T, GLA_HEADS, GLA_DK), s0.astype(jnp.float32))
    o = _rmsnorm(o, g_norm) * jax.nn.silu(heads(r, GLA_DV))
    return o.reshape(B, T, GLA_VD).astype(h.dtype) @ w_out, sT


def _dil_qkv(h, w_in, q_gain, k_gain):
    B, T, _ = h.shape
    p = (h @ w_in).reshape(B, T, N_GROUPS, 3, DIL_HEADS, DIL_HD)
    return _rmsnorm(p[:, :, :, 0], q_gain), _rmsnorm(p[:, :, :, 1], k_gain), p[:, :, :, 2]


def _dil_group_prompt(q, k, v, bias, d, J):
    B, T, H, E = q.shape
    L = T // d
    nb = -(-L // DIL_BLOCK)
    Lp = nb * DIL_BLOCK

    def sub(a):
        a = a.reshape(B, L, d, H, E).transpose(0, 2, 1, 3, 4)
        a = jnp.pad(a, ((0, 0), (0, 0), (0, Lp - L), (0, 0), (0, 0)))
        return a.reshape(B, d, nb, DIL_BLOCK, H, E)

    def window(a):
        prev = jnp.pad(a, ((0, 0), (0, 0), (1, 0), (0, 0), (0, 0), (0, 0)))[:, :, :-1]
        return jnp.concatenate([prev, a], axis=3)

    qb = sub(q)
    kw, vw = window(sub(k)), window(sub(v))
    n_i = jnp.arange(nb)[:, None, None]
    q_i = jnp.arange(DIL_BLOCK)[None, :, None]
    k_i = jnp.arange(2 * DIL_BLOCK)[None, None, :]
    rel = q_i + DIL_BLOCK - k_i
    valid = (rel >= 0) & (rel <= J) & ((n_i - 1) * DIL_BLOCK + k_i >= 0)
    s = jnp.einsum('brnqhe,brnkhe->brnhqk', qb, kw,
                   preferred_element_type=jnp.float32) * DIL_SCALE + bias
    s = jnp.where(valid[None, None, :, None], s, NEG)
    m = jnp.max(s, axis=-1, keepdims=True)
    p = jnp.exp(s - m)
    den = jnp.sum(p, axis=-1)
    o = jnp.einsum('brnhqk,brnkhe->brnqhe', p, vw.astype(jnp.float32))
    o = o / jnp.swapaxes(den, -1, -2)[..., None]
    lse = m[..., 0] + jnp.log(den)
    o = o.reshape(B, d, Lp, H, E)[:, :, :L].transpose(0, 2, 1, 3, 4).reshape(B, T, H, E)
    lse = jnp.swapaxes(lse, -1, -2).reshape(B, d, Lp, H)[:, :, :L]
    lse = lse.transpose(0, 2, 1, 3).reshape(B, T, H)
    return o, lse


def _dil_group_sample(q, k_new, v_new, k_buf, v_buf, bias, d, J):
    S = q.shape[1]
    Wb = k_buf.shape[1]
    kx = jnp.concatenate([k_buf.astype(k_new.dtype), k_new], axis=1)
    vx = jnp.concatenate([v_buf.astype(v_new.dtype), v_new], axis=1)
    idx = Wb + jnp.arange(S)[:, None] - d * jnp.arange(J + 1)[None, :]
    valid = idx >= 0
    idxc = jnp.maximum(idx, 0)
    kg, vg = kx[:, idxc], vx[:, idxc]
    s = jnp.einsum('bshe,bsjhe->bhsj', q, kg,
                   preferred_element_type=jnp.float32) * DIL_SCALE + bias[:, None, :]
    s = jnp.where(valid, s, NEG)
    m = jnp.max(s, axis=-1, keepdims=True)
    p = jnp.exp(s - m)
    den = jnp.sum(p, axis=-1)
    o = jnp.einsum('bhsj,bsjhe->bshe', p, vg.astype(jnp.float32))
    o = o / jnp.swapaxes(den, 1, 2)[..., None]
    lse = jnp.swapaxes(m[..., 0] + jnp.log(den), 1, 2)
    return o, lse


def _combine_groups(outs, lses):
    w = jax.nn.softmax(jnp.stack(lses, axis=0), axis=0)
    return jnp.sum(w[..., None] * jnp.stack(outs, axis=0), axis=0)


def _dil_mixer_prompt(h, w_in, q_gain, k_gain, w_out, rel_bias):
    B, T, _ = h.shape
    q, k, v = _dil_qkv(h, w_in, q_gain, k_gain)
    rel = jnp.arange(DIL_BLOCK)[:, None] + DIL_BLOCK - jnp.arange(2 * DIL_BLOCK)[None, :]
    outs, lses, k_rows, v_rows = [], [], [], []
    for g, (W, d) in enumerate(DIL_GROUPS):
        J = W // d
        bias = _group_bias(rel_bias, g, d, jnp.clip(rel, 0, J))
        o, lse = _dil_group_prompt(q[:, :, g], k[:, :, g], v[:, :, g], bias, d, J)
        outs.append(o)
        lses.append(lse)
        keep = min(W, T)
        k_rows.append(k[:, T - keep:, g])
        v_rows.append(v[:, T - keep:, g])
    o = _combine_groups(outs, lses)
    return o.reshape(B, T, DIL_WIDTH).astype(h.dtype) @ w_out, k_rows, v_rows


def _dil_mixer_sample(h, w_in, q_gain, k_gain, w_out, rel_bias, k_bufs, v_bufs):
    B, T, _ = h.shape
    q, k, v = _dil_qkv(h, w_in, q_gain, k_gain)
    outs, lses, k_rows, v_rows = [], [], [], []
    for g, (W, d) in enumerate(DIL_GROUPS):
        J = W // d
        bias = _group_bias(rel_bias, g, d, jnp.arange(J + 1))
        o, lse = _dil_group_sample(q[:, :, g], k[:, :, g], v[:, :, g],
                                   k_bufs[g], v_bufs[g], bias, d, J)
        outs.append(o)
        lses.append(lse)
        k_rows.append(k[:, :, g])
        v_rows.append(v[:, :, g])
    o = _combine_groups(outs, lses)
    return o.reshape(B, T, DIL_WIDTH).astype(h.dtype) @ w_out, k_rows, v_rows


def _conv_ffn(h, w_up, conv_w, conv_b, w_down, buf):
    T = h.shape[1]
    u = h @ w_up
    ux = jnp.concatenate([buf.astype(u.dtype), u], axis=1)
    c = conv_b + sum(ux[:, i:i + T] * conv_w[i] for i in range(CONV_WIDTH))
    gate, val = jnp.split(c, 2, axis=-1)
    return (jax.nn.silu(gate) * val) @ w_down, ux[:, -(CONV_WIDTH - 1):]


def setup_inputs(seed: int = 0) -> dict:
    key = jax.random.key(seed)
    ks = jax.random.split(key, 32)
    f32 = jnp.float32
    nrm = lambda k, shape, scale=1.0: jax.random.normal(k, shape, f32) * scale
    wb = [min(w, PAST_LEN) for (w, _) in DIL_GROUPS]
    cshape = lambda i: (N_DIL_LAYERS, DEC_BATCH, wb[i], DIL_HEADS, DIL_HD)
    return {
        "x_prompt": nrm(ks[0], (BATCH, SEQ, D_MODEL)),
        "x_sample": nrm(ks[1], (DEC_BATCH, DEC_SEQ, D_MODEL)),
        "state_gla": nrm(ks[2], (N_GLA_LAYERS, DEC_BATCH, GLA_HEADS, GLA_DK, GLA_DV)),
        "cache_k_g0": nrm(ks[3], cshape(0)),
        "cache_v_g0": nrm(ks[4], cshape(0)),
        "cache_k_g1": nrm(ks[5], cshape(1)),
        "cache_v_g1": nrm(ks[6], cshape(1)),
        "cache_k_g2": nrm(ks[7], cshape(2)),
        "cache_v_g2": nrm(ks[8], cshape(2)),
        "state_ffn_conv": nrm(ks[9], (DEPTH, DEC_BATCH, CONV_WIDTH - 1, 2 * D_FF)),
        "rel_bias": nrm(ks[10], (NUM_BUCKETS, N_GROUPS * DIL_HEADS), 0.2),
        "norm_mix": 1.0 + nrm(ks[11], (DEPTH, D_MODEL), 0.02),
        "norm_ffn": 1.0 + nrm(ks[12], (DEPTH, D_MODEL), 0.02),
        "gla_w_in": nrm(ks[13], (N_GLA_LAYERS, D_MODEL, GLA_IN), D_MODEL ** -0.5),
        "gla_w_gate2": nrm(ks[14], (N_GLA_LAYERS, GLA_GATE_RANK, GLA_KD), GLA_GATE_RANK ** -0.5),
        "gla_b_gate": nrm(ks[15], (N_GLA_LAYERS, GLA_KD), 0.1),
        "gla_norm": 1.0 + nrm(ks[16], (N_GLA_LAYERS, GLA_DV), 0.02),
        "gla_w_out": nrm(ks[17], (N_GLA_LAYERS, GLA_VD, D_MODEL), GLA_VD ** -0.5),
        "dil_w_in": nrm(ks[18], (N_DIL_LAYERS, D_MODEL, DIL_IN), D_MODEL ** -0.5),
        "dil_q_norm": 1.0 + nrm(ks[19], (N_DIL_LAYERS, DIL_HD), 0.02),
        "dil_k_norm": 1.0 + nrm(ks[20], (N_DIL_LAYERS, DIL_HD), 0.02),
        "dil_w_out": nrm(ks[21], (N_DIL_LAYERS, DIL_WIDTH, D_MODEL), DIL_WIDTH ** -0.5),
        "ffn_w_up": nrm(ks[22], (DEPTH, D_MODEL, 2 * D_FF), D_MODEL ** -0.5),
        "ffn_conv_w": nrm(ks[23], (DEPTH, CONV_WIDTH, 2 * D_FF), CONV_WIDTH ** -0.5),
        "ffn_conv_b": nrm(ks[24], (DEPTH, 2 * D_FF), 0.02),
        "ffn_w_down": nrm(ks[25], (DEPTH, D_FF, D_MODEL), D_FF ** -0.5),
    }


def reference(x_prompt, x_sample, state_gla, cache_k_g0, cache_v_g0, cache_k_g1, cache_v_g1,
              cache_k_g2, cache_v_g2, state_ffn_conv, rel_bias, norm_mix, norm_ffn,
              gla_w_in, gla_w_gate2, gla_b_gate, gla_norm, gla_w_out,
              dil_w_in, dil_q_norm, dil_k_norm, dil_w_out,
              ffn_w_up, ffn_conv_w, ffn_conv_b, ffn_w_down):
    k_bufs = (cache_k_g0, cache_k_g1, cache_k_g2)
    v_bufs = (cache_v_g0, cache_v_g1, cache_v_g2)
    B = x_prompt.shape[0]
    xp, xs = x_prompt, x_sample
    gla_p, gla_s = [], []
    kp = [[] for _ in DIL_GROUPS]
    vp = [[] for _ in DIL_GROUPS]
    kq = [[] for _ in DIL_GROUPS]
    vq = [[] for _ in DIL_GROUPS]
    conv_p, conv_s = [], []
    for i in range(DEPTH):
        li = i // N_MIXERS
        hp, hs = _rmsnorm(xp, norm_mix[i]), _rmsnorm(xs, norm_mix[i])
        if i % N_MIXERS == 0:
            s0 = jnp.zeros((B, GLA_HEADS, GLA_DK, GLA_DV), jnp.float32)
            mp, st_p = _gla_mixer(hp, gla_w_in[li], gla_w_gate2[li], gla_b_gate[li],
                                  gla_norm[li], gla_w_out[li], s0)
            ms, st_s = _gla_mixer(hs, gla_w_in[li], gla_w_gate2[li], gla_b_gate[li],
                                  gla_norm[li], gla_w_out[li], state_gla[li])
            gla_p.append(st_p)
            gla_s.append(st_s)
        else:
            mp, krp, vrp = _dil_mixer_prompt(hp, dil_w_in[li], dil_q_norm[li], dil_k_norm[li],
                                             dil_w_out[li], rel_bias)
            ms, krs, vrs = _dil_mixer_sample(hs, dil_w_in[li], dil_q_norm[li], dil_k_norm[li],
                                             dil_w_out[li], rel_bias,
                                             [kb[li] for kb in k_bufs], [vb[li] for vb in v_bufs])
            for g in range(N_GROUPS):
                kp[g].append(krp[g])
                vp[g].append(vrp[g])
                kq[g].append(krs[g])
                vq[g].append(vrs[g])
        xp, xs = xp + mp, xs + ms
        hp, hs = _rmsnorm(xp, norm_ffn[i]), _rmsnorm(xs, norm_ffn[i])
        buf0 = jnp.zeros((B, CONV_WIDTH - 1, 2 * D_FF), xp.dtype)
        fp, cp = _conv_ffn(hp, ffn_w_up[i], ffn_conv_w[i], ffn_conv_b[i], ffn_w_down[i], buf0)
        fs, cs = _conv_ffn(hs, ffn_w_up[i], ffn_conv_w[i], ffn_conv_b[i], ffn_w_down[i],
                           state_ffn_conv[i])
        conv_p.append(cp)
        conv_s.append(cs)
        xp, xs = xp + fp, xs + fs
    state_gla_prompt = jnp.stack(gla_p)
    state_gla_sample = jnp.stack(gla_s)
    cache_k_g0_prompt, cache_k_g0_sample = jnp.stack(kp[0]), jnp.stack(kq[0])
    cache_v_g0_prompt, cache_v_g0_sample = jnp.stack(vp[0]), jnp.stack(vq[0])
    cache_k_g1_prompt, cache_k_g1_sample = jnp.stack(kp[1]), jnp.stack(kq[1])
    cache_v_g1_prompt, cache_v_g1_sample = jnp.stack(vp[1]), jnp.stack(vq[1])
    cache_k_g2_prompt, cache_k_g2_sample = jnp.stack(kp[2]), jnp.stack(kq[2])
    cache_v_g2_prompt, cache_v_g2_sample = jnp.stack(vp[2]), jnp.stack(vq[2])
    state_ffn_conv_prompt = jnp.stack(conv_p)
    state_ffn_conv_sample = jnp.stack(conv_s)
    return (xp, xs, state_gla_prompt, state_gla_sample,
            cache_k_g0_prompt, cache_k_g0_sample, cache_v_g0_prompt, cache_v_g0_sample,
            cache_k_g1_prompt, cache_k_g1_sample, cache_v_g1_prompt, cache_v_g1_sample,
            cache_k_g2_prompt, cache_k_g2_sample, cache_v_g2_prompt, cache_v_g2_sample,
            state_ffn_conv_prompt, state_ffn_conv_sample)
```

```python
import functools
import math

import numpy as np
import jax
import jax.numpy as jnp
from jax import lax
from jax.experimental import pallas as pl
from jax.experimental.pallas import tpu as pltpu

F32 = jnp.float32
BF16 = jnp.bfloat16

D_MODEL = 1024
DEPTH = 4
N_MIXERS = 2
GLA_HEADS = 4
GLA_KD = 512
GLA_VD = 1024
GLA_DK = 128
GLA_DV = 256
GLA_GATE_RANK = 16
GLA_GATE_NORM = 16.0
GLA_CHUNK = 32
DIL_GROUPS = ((128, 1), (512, 4), (2048, 16))
N_GROUPS = 3
DIL_HEADS = 16
DIL_HD = 64
DIL_WIDTH = 1024
DIL_BLOCK = 128
DIL_SCALE = DIL_HD ** -0.5
NUM_BUCKETS = 32
MAX_DISTANCE = 2048
D_FF = 2816
CONV_WIDTH = 3
EPS = 1e-6
NEG = -1e30

LANES = 128
MXU_WIDTH = 256
VMEM_LIMIT = 48 * 1024 * 1024

_NT = (((1,), (1,)), ((), ()))
_TN = (((0,), (0,)), ((), ()))


def _params(*sem):
    return pltpu.CompilerParams(dimension_semantics=sem, vmem_limit_bytes=VMEM_LIMIT)


def _rms(x, g):
    return x * lax.rsqrt(jnp.mean(x * x, axis=-1, keepdims=True) + EPS) * g


def _silu(x):
    return x * (1.0 / (1.0 + jnp.exp(-x)))


def _nm_body(x_ref, g_ref, w_ref, o_ref, h_ref):
    @pl.when(pl.program_id(1) == 0)
    def _():
        h_ref[...] = _rms(x_ref[...], g_ref[...]).astype(BF16)

    o_ref[...] = jnp.dot(h_ref[...], w_ref[...], preferred_element_type=F32).astype(o_ref.dtype)


def norm_matmul(x, g, w, out_dtype, tm, tn):
    M, D = x.shape
    N = w.shape[1]
    return pl.pallas_call(
        _nm_body,
        grid=(M // tm, N // tn),
        in_specs=[pl.BlockSpec((tm, D), lambda i, j: (i, 0)),
                  pl.BlockSpec((1, D), lambda i, j: (0, 0)),
                  pl.BlockSpec((D, tn), lambda i, j: (0, j))],
        out_specs=pl.BlockSpec((tm, tn), lambda i, j: (i, j)),
        out_shape=jax.ShapeDtypeStruct((M, N), out_dtype),
        scratch_shapes=[pltpu.VMEM((tm, D), BF16)],
        compiler_params=_params("parallel", "arbitrary"),
        name="norm_matmul",
    )(x, g, w)


def _nmqk_body(x_ref, g_ref, w_ref, cg_ref, bd_ref, o_ref, h_ref, *, tn):
    j = pl.program_id(1)

    @pl.when(j == 0)
    def _():
        h_ref[...] = _rms(x_ref[...], g_ref[...]).astype(BF16)

    acc = jnp.dot(h_ref[...], w_ref[...], preferred_element_type=F32)
    is_norm = ((j * tn) // DIL_WIDTH) % 3 != 2

    @pl.when(is_norm)
    def _():
        for c in range(tn // MXU_WIDTH):
            cs = slice(c * MXU_WIDTH, (c + 1) * MXU_WIDTH)
            pc = acc[:, cs]
            ss = jnp.dot((pc * pc).astype(BF16), bd_ref[...], preferred_element_type=F32)
            o_ref[:, cs] = (pc * lax.rsqrt(ss * (1.0 / DIL_HD) + EPS) * cg_ref[:, cs]).astype(o_ref.dtype)

    @pl.when(jnp.logical_not(is_norm))
    def _():
        o_ref[...] = acc.astype(o_ref.dtype)


def norm_matmul_qk(x, g, w, colgain, bd, out_dtype, tm, tn):
    M, D = x.shape
    N = w.shape[1]
    return pl.pallas_call(
        functools.partial(_nmqk_body, tn=tn),
        grid=(M // tm, N // tn),
        in_specs=[pl.BlockSpec((tm, D), lambda i, j: (i, 0)),
                  pl.BlockSpec((1, D), lambda i, j: (0, 0)),
                  pl.BlockSpec((D, tn), lambda i, j: (0, j)),
                  pl.BlockSpec((1, tn), lambda i, j: (0, j)),
                  pl.BlockSpec((MXU_WIDTH, MXU_WIDTH), lambda i, j: (0, 0))],
        out_specs=pl.BlockSpec((tm, tn), lambda i, j: (i, j)),
        out_shape=jax.ShapeDtypeStruct((M, N), out_dtype),
        scratch_shapes=[pltpu.VMEM((tm, D), BF16)],
        compiler_params=_params("parallel", "arbitrary"),
        name="norm_matmul_qk",
    )(x, g, w, colgain, bd)


def _mmres_body(y_ref, w_ref, x_ref, o_ref):
    o_ref[...] = x_ref[...] + jnp.dot(y_ref[...], w_ref[...], preferred_element_type=F32)


def matmul_residual(y, w, x, tm, tn):
    M, K = y.shape
    N = w.shape[1]
    return pl.pallas_call(
        _mmres_body,
        grid=(M // tm, N // tn),
        in_specs=[pl.BlockSpec((tm, K), lambda i, j: (i, 0)),
                  pl.BlockSpec((K, tn), lambda i, j: (0, j)),
                  pl.BlockSpec((tm, tn), lambda i, j: (i, j))],
        out_specs=pl.BlockSpec((tm, tn), lambda i, j: (i, j)),
        out_shape=jax.ShapeDtypeStruct((M, N), F32),
        compiler_params=_params("parallel", "parallel"),
        name="matmul_residual",
    )(y, w, x)


def _gate_body(x_ref, g_ref, wgz_ref, wg2_ref, b_ref, o_ref):
    h = _rms(x_ref[...], g_ref[...]).astype(BF16)
    gz = jnp.dot(h, wgz_ref[...], preferred_element_type=F32)
    z = jnp.dot(gz.astype(BF16), wg2_ref[...], preferred_element_type=F32) + b_ref[...]
    o_ref[...] = (jnp.minimum(z, 0.0) - jnp.log(1.0 + jnp.exp(-jnp.abs(z)))) * (1.0 / GLA_GATE_NORM)


def gla_gate(x, g, wgz, wg2, b, tm):
    M, D = x.shape
    return pl.pallas_call(
        _gate_body,
        grid=(M // tm,),
        in_specs=[pl.BlockSpec((tm, D), lambda i: (i, 0)),
                  pl.BlockSpec((1, D), lambda i: (0, 0)),
                  pl.BlockSpec((D, LANES), lambda i: (0, 0)),
                  pl.BlockSpec((LANES, GLA_KD), lambda i: (0, 0)),
                  pl.BlockSpec((1, GLA_KD), lambda i: (0, 0))],
        out_specs=pl.BlockSpec((tm, GLA_KD), lambda i: (i, 0)),
        out_shape=jax.ShapeDtypeStruct((M, GLA_KD), F32),
        compiler_params=_params("parallel"),
        name="gla_gate",
    )(x, g, wgz, wg2, b)


def _gla_body(q_ref, k_ref, v_ref, r_ref, gl_ref, s0_ref, gn_ref, tri_ref, y_ref, st_ref, S_ref, *, TB, CH):
    c = pl.program_id(1)

    @pl.when(c == 0)
    def _():
        S_ref[...] = s0_ref[...]

    gl = gl_ref[...]
    g1 = gl.astype(BF16)
    r1 = gl - g1.astype(F32)
    g2 = r1.astype(BF16)
    g3 = (r1 - g2.astype(F32)).astype(BF16)
    tri = tri_ref[...]
    bfull = (jnp.dot(tri, g1, preferred_element_type=F32)
             + jnp.dot(tri, g2, preferred_element_type=F32)
             + jnp.dot(tri, g3, preferred_element_type=F32))

    row = lax.broadcasted_iota(jnp.int32, (CH, CH), 0)
    col = lax.broadcasted_iota(jnp.int32, (CH, CH), 1)
    causal = row >= col
    mid = CH // 2
    gn = gn_ref[...]

    for h in range(GLA_HEADS):
        ks = slice(h * GLA_DK, (h + 1) * GLA_DK)
        vs = slice(h * GLA_DV, (h + 1) * GLA_DV)
        S = S_ref[h]
        for sc in range(TB // CH):
            rs = slice(sc * CH, (sc + 1) * CH)
            b = bfull[rs, ks]
            if sc > 0:
                b = b - bfull[sc * CH - 1:sc * CH, ks]
            ref = b[mid:mid + 1]
            blast = b[CH - 1:CH]
            qf = q_ref[rs, ks].astype(F32) * (GLA_DK ** -0.5)
            kf = k_ref[rs, ks].astype(F32)
            vv = v_ref[rs, vs]
            qe = (qf * jnp.exp(b - ref)).astype(BF16)
            ke = (kf * jnp.exp(ref - b)).astype(BF16)
            a = lax.dot_general(qe, ke, _NT, preferred_element_type=F32)
            a = jnp.where(causal, a, 0.0)
            o = (jnp.dot(a.astype(BF16), vv, preferred_element_type=F32)
                 + lax.dot_general((qf * jnp.exp(b)).astype(BF16), S.astype(BF16), _NT,
                                   preferred_element_type=F32))
            kd = (kf * jnp.exp(blast - b)).astype(BF16)
            S = jnp.exp(blast) * S + lax.dot_general(vv, kd, _TN, preferred_element_type=F32)
            y = _rms(o, gn) * _silu(r_ref[rs, vs].astype(F32))
            y_ref[rs, vs] = y.astype(y_ref.dtype)
        S_ref[h] = S

    @pl.when(c == pl.num_programs(1) - 1)
    def _():
        st_ref[...] = S_ref[...]


def gla_scan(p, glog, s0t, gn, TB, CH):
    B, T, _ = p.shape
    tri = jnp.asarray(np.tril(np.ones((TB, TB), np.float32)), BF16)
    return pl.pallas_call(
        functools.partial(_gla_body, TB=TB, CH=CH),
        grid=(B, T // TB),
        in_specs=[pl.BlockSpec((None, TB, GLA_KD), lambda b, c: (b, c, 0)),
                  pl.BlockSpec((None, TB, GLA_KD), lambda b, c: (b, c, 1)),
                  pl.BlockSpec((None, TB, GLA_VD), lambda b, c: (b, c, 1)),
                  pl.BlockSpec((None, TB, GLA_VD), lambda b, c: (b, c, 2)),
                  pl.BlockSpec((None, TB, GLA_KD), lambda b, c: (b, c, 0)),
                  pl.BlockSpec((None, GLA_HEADS, GLA_DV, GLA_DK), lambda b, c: (b, 0, 0, 0)),
                  pl.BlockSpec((1, GLA_DV), lambda b, c: (0, 0)),
                  pl.BlockSpec((TB, TB), lambda b, c: (0, 0))],
        out_specs=[pl.BlockSpec((None, TB, GLA_VD), lambda b, c: (b, c, 0)),
                   pl.BlockSpec((None, GLA_HEADS, GLA_DV, GLA_DK), lambda b, c: (b, 0, 0, 0))],
        out_shape=[jax.ShapeDtypeStruct((B, T, GLA_VD), BF16),
                   jax.ShapeDtypeStruct((B, GLA_HEADS, GLA_DV, GLA_DK), F32)],
        scratch_shapes=[pltpu.VMEM((GLA_HEADS, GLA_DV, GLA_DK), F32)],
        compiler_params=_params("parallel", "arbitrary"),
        name="gla_scan",
    )(p, p, p, p, glog, s0t, gn, tri)


def _attn_body(q_ref, kp_ref, kc_ref, vp_ref, vc_ref, bias_ref, o_ref, lse_ref):
    var = jnp.where(pl.program_id(2) == 0, 0, 1)
    lane = lax.broadcasted_iota(jnp.int32, (DIL_BLOCK, LANES), 1)
    lo = lane < DIL_HD
    lse_acc = jnp.zeros((DIL_BLOCK, LANES), F32)
    zero = jnp.zeros((), BF16)
    for hp in range(DIL_HEADS // 2):
        cs = slice(hp * LANES, (hp + 1) * LANES)
        q2 = q_ref[:, cs]
        qab = jnp.concatenate([jnp.where(lo, q2, zero), jnp.where(lo, zero, q2)], axis=0)
        k2 = jnp.concatenate([kp_ref[:, cs], kc_ref[:, cs]], axis=0)
        v2 = jnp.concatenate([vp_ref[:, cs], vc_ref[:, cs]], axis=0)
        s = lax.dot_general(qab, k2, _NT, preferred_element_type=F32) + bias_ref[var, hp]
        m = jnp.max(s, axis=-1, keepdims=True)
        p = jnp.exp(s - m)
        l = jnp.sum(p, axis=-1, keepdims=True)
        o = jnp.dot(p.astype(BF16), v2, preferred_element_type=F32) / l
        o_ref[:, cs] = jnp.where(lo, o[:DIL_BLOCK], o[DIL_BLOCK:])
        lse = m + jnp.log(l)
        lse_acc = jnp.where(lane == 2 * hp, lse[:DIL_BLOCK], lse_acc)
        lse_acc = jnp.where(lane == 2 * hp + 1, lse[DIL_BLOCK:], lse_acc)
    lse_ref[...] = lse_acc


def dil_attention(qkv, bias):
    B, d, L, _ = qkv.shape
    nb = L // DIL_BLOCK
    blk = (None, None, DIL_BLOCK, DIL_WIDTH)
    prev = lambda n: jnp.maximum(n - 1, 0)
    return pl.pallas_call(
        _attn_body,
        grid=(B, d, nb),
        in_specs=[pl.BlockSpec(blk, lambda b, r, n: (b, r, n, 0)),
                  pl.BlockSpec(blk, lambda b, r, n: (b, r, prev(n), 1)),
                  pl.BlockSpec(blk, lambda b, r, n: (b, r, n, 1)),
                  pl.BlockSpec(blk, lambda b, r, n: (b, r, prev(n), 2)),
                  pl.BlockSpec(blk, lambda b, r, n: (b, r, n, 2)),
                  pl.BlockSpec((2, DIL_HEADS // 2, 2 * DIL_BLOCK, 2 * DIL_BLOCK), lambda b, r, n: (0, 0, 0, 0))],
        out_specs=[pl.BlockSpec(blk, lambda b, r, n: (b, r, n, 0)),
                   pl.BlockSpec((None, None, DIL_BLOCK, LANES), lambda b, r, n: (b, r, n, 0))],
        out_shape=[jax.ShapeDtypeStruct((B, d, L, DIL_WIDTH), F32),
                   jax.ShapeDtypeStruct((B, d, L, LANES), F32)],
        compiler_params=_params("parallel", "parallel", "arbitrary"),
        name="dil_attention",
    )(qkv, qkv, qkv, qkv, qkv, bias)


def _dilout_body(o0_ref, o1_ref, o2_ref, l0_ref, l1_ref, l2_ref, x_ref, w_ref, e_ref, out_ref):
    l0, l1, l2 = l0_ref[...], l1_ref[...], l2_ref[...]
    mx = jnp.maximum(jnp.maximum(l0, l1), l2)
    e0, e1, e2 = jnp.exp(l0 - mx), jnp.exp(l1 - mx), jnp.exp(l2 - mx)
    den = e0 + e1 + e2
    ex = e_ref[...]

    def expand(w):
        w1 = w.astype(BF16)
        w2 = (w - w1.astype(F32)).astype(BF16)
        return (jnp.dot(w1, ex, preferred_element_type=F32) + jnp.dot(w2, ex, preferred_element_type=F32))

    o = (expand(e0 / den) * o0_ref[...] + expand(e1 / den) * o1_ref[...] + expand(e2 / den) * o2_ref[...])
    out_ref[...] = x_ref[...] + jnp.dot(o.astype(BF16), w_ref[...], preferred_element_type=F32)


def dil_out(os, ls, x, w, expand, tm):
    M = x.shape[0]
    ospec = pl.BlockSpec((tm, DIL_WIDTH), lambda i: (i, 0))
    lspec = pl.BlockSpec((tm, LANES), lambda i: (i, 0))
    return pl.pallas_call(
        _dilout_body,
        grid=(M // tm,),
        in_specs=[ospec, ospec, ospec, lspec, lspec, lspec, ospec,
                  pl.BlockSpec((DIL_WIDTH, D_MODEL), lambda i: (0, 0)),
                  pl.BlockSpec((LANES, DIL_WIDTH), lambda i: (0, 0))],
        out_specs=pl.BlockSpec((tm, D_MODEL), lambda i: (i, 0)),
        out_shape=jax.ShapeDtypeStruct((M, D_MODEL), F32),
        compiler_params=_params("parallel"),
        name="dil_out",
    )(*os, *ls, x, w, expand)


def _sattn_body(qkv_ref, k0_ref, v0_ref, k1_ref, v1_ref, k2_ref, v2_ref, b0_ref, b1_ref, b2_ref, bn_ref, o_ref):
    kbufs = (k0_ref, k1_ref, k2_ref)
    vbufs = (v0_ref, v1_ref, v2_ref)
    bbufs = (b0_ref, b1_ref, b2_ref)
    R = DIL_HEADS * 4
    ms, ls, accs = [], [], []
    for g in range(N_GROUPS):
        q = qkv_ref[g, 0].astype(BF16)
        kn = qkv_ref[g, 1].astype(BF16)
        vn = qkv_ref[g, 2].astype(BF16)
        nrows = math.prod(kbufs[g].shape) // DIL_HD
        kb = kbufs[g][...].reshape(nrows, DIL_HD).astype(BF16)
        vb = vbufs[g][...].reshape(nrows, DIL_HD).astype(BF16)
        sb = lax.dot_general(q, kb, _NT, preferred_element_type=F32) + bbufs[g][...]
        sn = lax.dot_general(q, kn, _NT, preferred_element_type=F32) + bn_ref[g]
        m = jnp.maximum(jnp.max(sb, axis=-1, keepdims=True), jnp.max(sn, axis=-1, keepdims=True))
        pb = jnp.exp(sb - m)
        pn = jnp.exp(sn - m)
        ls.append(jnp.sum(pb, axis=-1, keepdims=True) + jnp.sum(pn, axis=-1, keepdims=True))
        accs.append(jnp.dot(pb.astype(BF16), vb, preferred_element_type=F32)
                    + jnp.dot(pn.astype(BF16), vn, preferred_element_type=F32))
        ms.append(m)
    mx = jnp.maximum(jnp.maximum(ms[0], ms[1]), ms[2])
    num = sum(jnp.exp(ms[g] - mx) * accs[g] for g in range(N_GROUPS))
    den = sum(jnp.exp(ms[g] - mx) * ls[g] for g in range(N_GROUPS))
    o_ref[...] = num / den


def dil_sample_attention(li, qkv, kcs, vcs, bbs, bn):
    DB = qkv.shape[0]
    R = DIL_HEADS * 4
    in_specs = [pl.BlockSpec((None, N_GROUPS, 3, R, DIL_HD), lambda b: (b, 0, 0, 0, 0))]
    args = [qkv]
    for g in range(N_GROUPS):
        for c in (kcs[g], vcs[g]):
            if c.ndim == 5:
                in_specs.append(pl.BlockSpec((None, None) + c.shape[2:], lambda b: (li, b, 0, 0, 0)))
            else:
                in_specs.append(pl.BlockSpec((None, None, c.shape[2], 4, DIL_HEADS, DIL_HD),
                                             lambda b: (li, b, 0, 0, 0, 0)))
            args.append(c)
    for bb in bbs:
        in_specs.append(pl.BlockSpec(bb.shape, lambda b: (0, 0)))
        args.append(bb)
    in_specs.append(pl.BlockSpec(bn.shape, lambda b: (0, 0, 0)))
    args.append(bn)
    return pl.pallas_call(
        _sattn_body,
        grid=(DB,),
        in_specs=in_specs,
        out_specs=pl.BlockSpec((None, R, DIL_HD), lambda b: (b, 0, 0)),
        out_shape=jax.ShapeDtypeStruct((DB, R, DIL_HD), F32),
        compiler_params=_params("parallel"),
        name="dil_sample_attention",
    )(*args)


FFN_CHUNK = MXU_WIDTH
HALO = 16


def _conv_gate(u_scr, ug, uv, hg, hv, cw_g, cw_v, cb_g, cb_v, rows):
    out = []
    for idx, (u, hu, cw, cb) in enumerate(((ug, hg, cw_g, cb_g), (uv, hv, cw_v, cb_v))):
        u_scr[idx, 0:HALO] = hu
        u_scr[idx, HALO:HALO + rows] = u
        u1 = u_scr[idx, HALO - 1:HALO - 1 + rows]
        u2 = u_scr[idx, HALO - 2:HALO - 2 + rows]
        out.append(cb + (cw[0:1] * u2 + cw[1:2] * u1 + cw[2:3] * u))
    return _silu(out[0]) * out[1]


def _ffnup_body(x_ref, xh_ref, g_ref, wg_ref, wv_ref, cw_ref, cb_ref, act_ref, tail_ref, u_scr, *, tm):
    first = pl.program_id(1) == 0
    h = _rms(x_ref[...], g_ref[...]).astype(BF16)
    hh = _rms(xh_ref[...], g_ref[...])
    hh = jnp.where(first, 0.0, hh).astype(BF16)
    for c in range(D_FF // FFN_CHUNK):
        cs = slice(c * FFN_CHUNK, (c + 1) * FFN_CHUNK)
        vsl = slice(D_FF + c * FFN_CHUNK, D_FF + (c + 1) * FFN_CHUNK)
        ug = jnp.dot(h, wg_ref[:, cs], preferred_element_type=F32)
        uv = jnp.dot(h, wv_ref[:, cs], preferred_element_type=F32)
        hg = jnp.dot(hh, wg_ref[:, cs], preferred_element_type=F32)
        hv = jnp.dot(hh, wv_ref[:, cs], preferred_element_type=F32)
        act = _conv_gate(u_scr, ug, uv, hg, hv, cw_ref[:, cs], cw_ref[:, vsl], cb_ref[:, cs], cb_ref[:, vsl], tm)
        act_ref[:, cs] = act.astype(act_ref.dtype)
        tail_ref[:, cs] = ug[tm - 8:tm]
        tail_ref[:, vsl] = uv[tm - 8:tm]


def ffn_up_prompt(x, g, wg, wv, cw, cb, tm):
    B, T, D = x.shape
    hblk = tm // HALO
    return pl.pallas_call(
        functools.partial(_ffnup_body, tm=tm),
        grid=(B, T // tm),
        in_specs=[pl.BlockSpec((None, tm, D), lambda b, i: (b, i, 0)),
                  pl.BlockSpec((None, HALO, D), lambda b, i: (b, jnp.maximum(i * hblk - 1, 0), 0)),
                  pl.BlockSpec((1, D), lambda b, i: (0, 0)),
                  pl.BlockSpec((D, D_FF), lambda b, i: (0, 0)),
                  pl.BlockSpec((D, D_FF), lambda b, i: (0, 1)),
                  pl.BlockSpec((CONV_WIDTH, 2 * D_FF), lambda b, i: (0, 0)),
                  pl.BlockSpec((1, 2 * D_FF), lambda b, i: (0, 0))],
        out_specs=[pl.BlockSpec((None, tm, D_FF), lambda b, i: (b, i, 0)),
                   pl.BlockSpec((None, 8, 2 * D_FF), lambda b, i: (b, 0, 0))],
        out_shape=[jax.ShapeDtypeStruct((B, T, D_FF), BF16),
                   jax.ShapeDtypeStruct((B, 8, 2 * D_FF), F32)],
        scratch_shapes=[pltpu.VMEM((2, HALO + tm, FFN_CHUNK), F32)],
        compiler_params=_params("parallel", "arbitrary"),
        name="ffn_up_prompt",
    )(x, x, g, wg, wv, cw, cb)


def _ffnup_s_body(x_ref, g_ref, wg_ref, wv_ref, cw_ref, cb_ref, c1_ref, c2_ref, act_ref, u_ref, u_scr, *, rows, seq):
    h = _rms(x_ref[...], g_ref[...]).astype(BF16)
    pos = lax.broadcasted_iota(jnp.int32, (rows, FFN_CHUNK), 0) % seq
    u_scr[:, 0:HALO] = jnp.zeros((2, HALO, FFN_CHUNK), F32)
    for c in range(D_FF // FFN_CHUNK):
        acts = []
        for idx, off in enumerate((0, D_FF)):
            cs = slice(off + c * FFN_CHUNK, off + (c + 1) * FFN_CHUNK)
            w_ref = wg_ref if idx == 0 else wv_ref
            u = jnp.dot(h, w_ref[:, c * FFN_CHUNK:(c + 1) * FFN_CHUNK], preferred_element_type=F32)
            u_scr[idx, HALO:HALO + rows] = u
            u1 = jnp.where(pos >= 1, u_scr[idx, HALO - 1:HALO - 1 + rows], c1_ref[:, cs])
            u2 = jnp.where(pos >= 2, u_scr[idx, HALO - 2:HALO - 2 + rows], c2_ref[:, cs])
            acts.append(cb_ref[:, cs] + (cw_ref[0:1, cs] * u2 + cw_ref[1:2, cs] * u1 + cw_ref[2:3, cs] * u))
            u_ref[:, cs] = u
        act_ref[:, c * FFN_CHUNK:(c + 1) * FFN_CHUNK] = (_silu(acts[0]) * acts[1]).astype(act_ref.dtype)


def ffn_up_sample(x, g, wg, wv, cw, cb, carry1, carry2, seq):
    M, D = x.shape
    full = lambda shape: pl.BlockSpec(shape, lambda i: tuple(0 for _ in shape))
    return pl.pallas_call(
        functools.partial(_ffnup_s_body, rows=M, seq=seq),
        grid=(1,),
        in_specs=[full((M, D)), full((1, D)),
                  pl.BlockSpec((D, D_FF), lambda i: (0, 0)),
                  pl.BlockSpec((D, D_FF), lambda i: (0, 1)),
                  full((CONV_WIDTH, 2 * D_FF)), full((1, 2 * D_FF)),
                  full((M, 2 * D_FF)), full((M, 2 * D_FF))],
        out_specs=[full((M, D_FF)), full((M, 2 * D_FF))],
        out_shape=[jax.ShapeDtypeStruct((M, D_FF), BF16),
                   jax.ShapeDtypeStruct((M, 2 * D_FF), F32)],
        scratch_shapes=[pltpu.VMEM((2, HALO + M, FFN_CHUNK), F32)],
        compiler_params=_params("arbitrary"),
        name="ffn_up_sample",
    )(x, g, wg, wv, cw, cb, carry1, carry2)


def _rel_bucket(dist):
    max_exact = NUM_BUCKETS // 2
    df = jnp.maximum(dist, 1).astype(F32)
    large = max_exact + (jnp.log(df / max_exact) / math.log(MAX_DISTANCE / max_exact)
                         * (NUM_BUCKETS - max_exact)).astype(jnp.int32)
    large = jnp.minimum(large, NUM_BUCKETS - 1)
    return jnp.where(dist < max_exact, dist, large)


def _prompt_bias(rel_bias, g, d):
    J = DIL_GROUPS[g][0] // d
    qi = np.arange(DIL_BLOCK)[:, None]
    ki = np.arange(2 * DIL_BLOCK)[None, :]
    rel = qi + DIL_BLOCK - ki
    tab = rel_bias[:, g * DIL_HEADS:(g + 1) * DIL_HEADS].astype(F32)
    bias = jnp.moveaxis(tab[_rel_bucket(jnp.asarray(np.clip(rel, 0, J) * d))], -1, 0)
    out = []
    for first in (True, False):
        valid = (rel >= 0) & (rel <= J) & ((ki >= DIL_BLOCK) | (not first))
        b = jnp.where(jnp.asarray(valid)[None], bias, NEG)
        out.append(b.reshape(DIL_HEADS // 2, 2 * DIL_BLOCK, 2 * DIL_BLOCK))
    return jnp.stack(out)


def _sample_bias(rel_bias, g, d, seq):
    W = DIL_GROUPS[g][0]
    J = W // d
    tab = rel_bias[:, g * DIL_HEADS:(g + 1) * DIL_HEADS].astype(F32)
    s = np.arange(seq)[:, None]
    if d == 1:
        pos = np.arange(W)[None, :]
    else:
        nres = 4
        pos = (np.arange(W // d)[:, None] * d + np.arange(nres)[None, :]).reshape(1, -1)
    dist = W + s - pos
    valid = (dist >= 0) & (dist <= W) & (dist % d == 0)
    steps = np.where(valid, dist // d, 0)
    bval = tab[_rel_bucket(jnp.asarray(steps * d))]
    eye = np.eye(DIL_HEADS, dtype=bool)
    full = jnp.where(jnp.asarray(valid[:, None, :, None] & eye[None, :, None, :]), bval[:, None], NEG)
    bbuf = full.reshape(seq * DIL_HEADS, -1)
    dn = s - np.arange(seq)[None, :]
    vn = (dn >= 0) & (dn % d == 0) & (dn // d <= J)
    bn = tab[_rel_bucket(jnp.asarray(np.where(vn, dn, 0)))]
    fulln = jnp.where(jnp.asarray(vn[:, None, :, None] & eye[None, :, None, :]), bn[:, None], NEG)
    return bbuf, fulln.reshape(seq * DIL_HEADS, seq * DIL_HEADS)


def _colgain(q_gain, k_gain, ngroups):
    seg = jnp.concatenate([jnp.tile(q_gain.astype(F32) * DIL_SCALE, DIL_HEADS),
                           jnp.tile(k_gain.astype(F32), DIL_HEADS),
                           jnp.ones((DIL_WIDTH,), F32)])
    return jnp.tile(seg, ngroups)[None, :]


def kernel(x_prompt, x_sample, state_gla, cache_k_g0, cache_v_g0, cache_k_g1, cache_v_g1, cache_k_g2, cache_v_g2,
           state_ffn_conv, rel_bias, norm_mix, norm_ffn, gla_w_in, gla_w_gate2, gla_b_gate, gla_norm, gla_w_out,
           dil_w_in, dil_q_norm, dil_k_norm, dil_w_out, ffn_w_up, ffn_conv_w, ffn_conv_b, ffn_w_down):
    B, T, D = x_prompt.shape
    DB, S, _ = x_sample.shape
    MP, MS = B * T, DB * S
    assert all(c.shape[2] == w for c, (w, _) in zip((cache_k_g0, cache_k_g1, cache_k_g2), DIL_GROUPS))

    xp = x_prompt.reshape(MP, D)
    xs = x_sample.reshape(MS, D)
    k_caches = (cache_k_g0,
                cache_k_g1.reshape(cache_k_g1.shape[:2] + (128, 4, DIL_HEADS, DIL_HD)),
                cache_k_g2.reshape(cache_k_g2.shape[:2] + (128, 16, DIL_HEADS, DIL_HD)))
    v_caches = (cache_v_g0,
                cache_v_g1.reshape(cache_v_g1.shape[:2] + (128, 4, DIL_HEADS, DIL_HD)),
                cache_v_g2.reshape(cache_v_g2.shape[:2] + (128, 16, DIL_HEADS, DIL_HD)))

    blockdiag = jnp.asarray(np.kron(np.eye(MXU_WIDTH // DIL_HD), np.ones((DIL_HD, DIL_HD))), BF16)
    expand = jnp.asarray(np.kron(np.eye(LANES, DIL_HEADS, dtype=np.float32).reshape(LANES, DIL_HEADS),
                                 np.ones((1, DIL_HD), np.float32)), BF16)
    p_bias = [_prompt_bias(rel_bias, g, d) for g, (_, d) in enumerate(DIL_GROUPS)]
    s_bias = [_sample_bias(rel_bias, g, d, S) for g, (_, d) in enumerate(DIL_GROUPS)]
    s_bias_buf = [b for b, _ in s_bias]
    s_bias_new = jnp.stack([b for _, b in s_bias])

    gla_p, gla_s = [], []
    kp = [[] for _ in DIL_GROUPS]
    vp = [[] for _ in DIL_GROUPS]
    kq = [[] for _ in DIL_GROUPS]
    vq = [[] for _ in DIL_GROUPS]
    conv_p, conv_s = [], []
    SPAD = 16

    for i in range(DEPTH):
        li = i // N_MIXERS
        gmix = norm_mix[i][None, :].astype(F32)
        if i % N_MIXERS == 0:
            w_main = gla_w_in[li][:, :2 * GLA_KD + 2 * GLA_VD].astype(BF16)
            wgz = jnp.pad(gla_w_in[li][:, 2 * GLA_KD + 2 * GLA_VD:], ((0, 0), (0, LANES - GLA_GATE_RANK))).astype(BF16)
            wg2 = jnp.pad(gla_w_gate2[li], ((0, LANES - GLA_GATE_RANK), (0, 0))).astype(BF16)
            bg = gla_b_gate[li][None, :].astype(F32)
            gn = gla_norm[li][None, :].astype(F32)
            w_out = gla_w_out[li].astype(BF16)
            pp = norm_matmul(xp, gmix, w_main, BF16, 1024, 512).reshape(B, T, -1)
            glp = gla_gate(xp, gmix, wgz, wg2, bg, 1024).reshape(B, T, GLA_KD)
            s0 = jnp.zeros((B, GLA_HEADS, GLA_DV, GLA_DK), F32)
            yp, stp = gla_scan(pp, glp, s0, gn, 128, GLA_CHUNK)
            xp = matmul_residual(yp.reshape(MP, GLA_VD), w_out, xp, 1024, 512)
            gla_p.append(jnp.swapaxes(stp, -1, -2))
            ps = norm_matmul(xs, gmix, w_main, BF16, MS, 512).reshape(DB, S, -1)
            gls = gla_gate(xs, gmix, wgz, wg2, bg, MS).reshape(DB, S, GLA_KD)
            ps = jnp.pad(ps, ((0, 0), (0, SPAD - S), (0, 0)))
            gls = jnp.pad(gls, ((0, 0), (0, SPAD - S), (0, 0)))
            ys, sts = gla_scan(ps, gls, jnp.swapaxes(state_gla[li].astype(F32), -1, -2), gn, SPAD, SPAD)
            xs = matmul_residual(ys[:, :S].reshape(MS, GLA_VD), w_out, xs, MS, 512)
            gla_s.append(jnp.swapaxes(sts, -1, -2))
        else:
            w_in = dil_w_in[li].astype(BF16)
            w_out = dil_w_out[li].astype(BF16)
            cg = _colgain(dil_q_norm[li], dil_k_norm[li], N_GROUPS)
            os_, ls_ = [], []
            for g, (W, d) in enumerate(DIL_GROUPS):
                L = T // d
                xg = xp if d == 1 else xp.reshape(B, L, d, D).transpose(0, 2, 1, 3).reshape(MP, D)
                gs = slice(g * 3 * DIL_WIDTH, (g + 1) * 3 * DIL_WIDTH)
                qkv = norm_matmul_qk(xg, gmix, w_in[:, gs], cg[:, gs], blockdiag, BF16, 1024, 512)
                qkv = qkv.reshape(B, d, L, 3 * DIL_WIDTH)
                o, lse = dil_attention(qkv, p_bias[g])
                if d > 1:
                    o = o.transpose(0, 2, 1, 3)
                    lse = lse.transpose(0, 2, 1, 3)
                os_.append(o.reshape(MP, DIL_WIDTH))
                ls_.append(lse.reshape(MP, LANES))
                keep = min(W, T)
                tail = qkv[:, :, L - keep // d:, DIL_WIDTH:].transpose(0, 2, 1, 3).reshape(B, keep, 2, DIL_HEADS, DIL_HD)
                kp[g].append(tail[:, :, 0].astype(F32))
                vp[g].append(tail[:, :, 1].astype(F32))
            xp = dil_out(os_, ls_, xp, w_out, expand, 512)
            qkvs = norm_matmul_qk(xs, gmix, w_in, cg, blockdiag, F32, MS, 512)
            qkvs = qkvs.reshape(DB, S, N_GROUPS, 3, DIL_HEADS, DIL_HD)
            for g in range(N_GROUPS):
                kq[g].append(qkvs[:, :, g, 1])
                vq[g].append(qkvs[:, :, g, 2])
            qkvs_t = qkvs.transpose(0, 2, 3, 1, 4, 5).reshape(DB, N_GROUPS, 3, S * DIL_HEADS, DIL_HD)
            osamp = dil_sample_attention(li, qkvs_t, k_caches, v_caches, s_bias_buf, s_bias_new)
            osamp = osamp.reshape(MS, DIL_WIDTH).astype(BF16)
            xs = matmul_residual(osamp, w_out, xs, MS, 512)

        gffn = norm_ffn[i][None, :].astype(F32)
        w_up = ffn_w_up[i].astype(BF16)
        w_down = ffn_w_down[i].astype(BF16)
        cw = ffn_conv_w[i].astype(F32)
        cb = ffn_conv_b[i][None, :].astype(F32)
        act, tail = ffn_up_prompt(xp.reshape(B, T, D), gffn, w_up, w_up, cw, cb, 512)
        conv_p.append(tail[:, 8 - (CONV_WIDTH - 1):])
        xp = matmul_residual(act.reshape(MP, D_FF), w_down, xp, 1024, 512)
        buf = state_ffn_conv[i].astype(F32)
        zeros = jnp.zeros((DB, S - 1, 2 * D_FF), F32)
        carry1 = jnp.concatenate([buf[:, 1:2], zeros], axis=1).reshape(MS, 2 * D_FF)
        carry2 = jnp.concatenate([buf, zeros[:, 1:]], axis=1).reshape(MS, 2 * D_FF)
        acts, us = ffn_up_sample(xs, gffn, w_up, w_up, cw, cb, carry1, carry2, S)
        conv_s.append(us.reshape(DB, S, 2 * D_FF)[:, S - (CONV_WIDTH - 1):])
        xs = matmul_residual(acts, w_down, xs, MS, 512)

    outs = [xp.reshape(B, T, D), xs.reshape(DB, S, D), jnp.stack(gla_p), jnp.stack(gla_s)]
    for g in range(N_GROUPS):
        outs += [jnp.stack(kp[g]), jnp.stack(kq[g]), jnp.stack(vp[g]), jnp.stack(vq[g])]
    outs += [jnp.stack(conv_p), jnp.stack(conv_s)]
    return tuple(outs)
```

```python
import functools
import math

import numpy as np
import jax
import jax.numpy as jnp
from jax import lax
from jax.experimental import pallas as pl
from jax.experimental.pallas import tpu as pltpu

F32 = jnp.float32
BF16 = jnp.bfloat16

D_MODEL = 1024
DEPTH = 4
N_MIXERS = 2
GLA_HEADS = 4
GLA_KD = 512
GLA_VD = 1024
GLA_DK = 128
GLA_DV = 256
GLA_GATE_RANK = 16
GLA_GATE_NORM = 16.0
GLA_CHUNK = 32
DIL_GROUPS = ((128, 1), (512, 4), (2048, 16))
N_GROUPS = 3
DIL_HEADS = 16
DIL_HD = 64
DIL_WIDTH = 1024
DIL_BLOCK = 128
DIL_SCALE = DIL_HD ** -0.5
NUM_BUCKETS = 32
MAX_DISTANCE = 2048
D_FF = 2816
CONV_WIDTH = 3
EPS = 1e-6
NEG = -1e30

LANES = 128
MXU_WIDTH = 256
VMEM_LIMIT = 48 * 1024 * 1024

_NT = (((1,), (1,)), ((), ()))
_TN = (((0,), (0,)), ((), ()))


def _params(*sem):
    return pltpu.CompilerParams(dimension_semantics=sem, vmem_limit_bytes=VMEM_LIMIT)


def _rms(x, g):
    return x * lax.rsqrt(jnp.mean(x * x, axis=-1, keepdims=True) + EPS) * g


def _silu(x):
    return x * (1.0 / (1.0 + jnp.exp(-x)))


def _nm_body(x_ref, g_ref, w_ref, o_ref, h_ref):
    @pl.when(pl.program_id(1) == 0)
    def _():
        h_ref[...] = _rms(x_ref[...], g_ref[...]).astype(BF16)

    o_ref[...] = jnp.dot(h_ref[...], w_ref[...], preferred_element_type=F32).astype(o_ref.dtype)


def norm_matmul(x, g, w, out_dtype, tm, tn):
    M, D = x.shape
    N = w.shape[1]
    return pl.pallas_call(
        _nm_body,
        grid=(M // tm, N // tn),
        in_specs=[pl.BlockSpec((tm, D), lambda i, j: (i, 0)),
                  pl.BlockSpec((1, D), lambda i, j: (0, 0)),
                  pl.BlockSpec((D, tn), lambda i, j: (0, j))],
        out_specs=pl.BlockSpec((tm, tn), lambda i, j: (i, j)),
        out_shape=jax.ShapeDtypeStruct((M, N), out_dtype),
        scratch_shapes=[pltpu.VMEM((tm, D), BF16)],
        compiler_params=_params("parallel", "arbitrary"),
        name="norm_matmul",
    )(x, g, w)


def _nmqk_body(x_ref, g_ref, w_ref, cg_ref, bd_ref, o_ref, h_ref, *, tn):
    j = pl.program_id(1)

    @pl.when(j == 0)
    def _():
        h_ref[...] = _rms(x_ref[...], g_ref[...]).astype(BF16)

    acc = jnp.dot(h_ref[...], w_ref[...], preferred_element_type=F32)
    is_norm = ((j * tn) // DIL_WIDTH) % 3 != 2

    @pl.when(is_norm)
    def _():
        for c in range(tn // MXU_WIDTH):
            cs = slice(c * MXU_WIDTH, (c + 1) * MXU_WIDTH)
            pc = acc[:, cs]
            ss = jnp.dot((pc * pc).astype(BF16), bd_ref[...], preferred_element_type=F32)
            o_ref[:, cs] = (pc * lax.rsqrt(ss * (1.0 / DIL_HD) + EPS) * cg_ref[:, cs]).astype(o_ref.dtype)

    @pl.when(jnp.logical_not(is_norm))
    def _():
        o_ref[...] = acc.astype(o_ref.dtype)


def norm_matmul_qk(x, g, w, colgain, bd, out_dtype, tm, tn, group=None):
    M, D = x.shape
    N = w.shape[1] if group is None else 3 * DIL_WIDTH
    j0 = 0 if group is None else group * N // tn
    return pl.pallas_call(
        functools.partial(_nmqk_body, tn=tn),
        grid=(M // tm, N // tn),
        in_specs=[pl.BlockSpec((tm, D), lambda i, j: (i, 0)),
                  pl.BlockSpec((1, D), lambda i, j: (0, 0)),
                  pl.BlockSpec((D, tn), lambda i, j: (0, j + j0)),
                  pl.BlockSpec((1, tn), lambda i, j: (0, j + j0)),
                  pl.BlockSpec((MXU_WIDTH, MXU_WIDTH), lambda i, j: (0, 0))],
        out_specs=pl.BlockSpec((tm, tn), lambda i, j: (i, j)),
        out_shape=jax.ShapeDtypeStruct((M, N), out_dtype),
        scratch_shapes=[pltpu.VMEM((tm, D), BF16)],
        compiler_params=_params("parallel", "arbitrary"),
        name="norm_matmul_qk",
    )(x, g, w, colgain, bd)


def _mmres_body(y_ref, w_ref, x_ref, o_ref):
    o_ref[...] = x_ref[...] + jnp.dot(y_ref[...], w_ref[...], preferred_element_type=F32)


def matmul_residual(y, w, x, tm, tn):
    M, K = y.shape
    N = w.shape[1]
    return pl.pallas_call(
        _mmres_body,
        grid=(M // tm, N // tn),
        in_specs=[pl.BlockSpec((tm, K), lambda i, j: (i, 0)),
                  pl.BlockSpec((K, tn), lambda i, j: (0, j)),
                  pl.BlockSpec((tm, tn), lambda i, j: (i, j))],
        out_specs=pl.BlockSpec((tm, tn), lambda i, j: (i, j)),
        out_shape=jax.ShapeDtypeStruct((M, N), F32),
        compiler_params=_params("parallel", "parallel"),
        name="matmul_residual",
    )(y, w, x)


def _gate_body(x_ref, g_ref, wgz_ref, wg2_ref, b_ref, o_ref):
    h = _rms(x_ref[...], g_ref[...]).astype(BF16)
    gz = jnp.dot(h, wgz_ref[...], preferred_element_type=F32)
    z = jnp.dot(gz.astype(BF16), wg2_ref[...], preferred_element_type=F32) + b_ref[...]
    o_ref[...] = (jnp.minimum(z, 0.0) - jnp.log(1.0 + jnp.exp(-jnp.abs(z)))) * (1.0 / GLA_GATE_NORM)


def gla_gate(x, g, wgz, wg2, b, tm):
    M, D = x.shape
    return pl.pallas_call(
        _gate_body,
        grid=(M // tm,),
        in_specs=[pl.BlockSpec((tm, D), lambda i: (i, 0)),
                  pl.BlockSpec((1, D), lambda i: (0, 0)),
                  pl.BlockSpec((D, LANES), lambda i: (0, 0)),
                  pl.BlockSpec((LANES, GLA_KD), lambda i: (0, 0)),
                  pl.BlockSpec((1, GLA_KD), lambda i: (0, 0))],
        out_specs=pl.BlockSpec((tm, GLA_KD), lambda i: (i, 0)),
        out_shape=jax.ShapeDtypeStruct((M, GLA_KD), F32),
        compiler_params=_params("parallel"),
        name="gla_gate",
    )(x, g, wgz, wg2, b)


def _gla_body(q_ref, k_ref, v_ref, r_ref, gl_ref, s0_ref, gn_ref, tri_ref, y_ref, st_ref, S_ref, *, TB, CH):
    c = pl.program_id(1)

    @pl.when(c == 0)
    def _():
        S_ref[...] = s0_ref[...]

    gl = gl_ref[...]
    g1 = gl.astype(BF16)
    r1 = gl - g1.astype(F32)
    g2 = r1.astype(BF16)
    g3 = (r1 - g2.astype(F32)).astype(BF16)
    tri = tri_ref[...]
    bfull = (jnp.dot(tri, g1, preferred_element_type=F32)
             + jnp.dot(tri, g2, preferred_element_type=F32)
             + jnp.dot(tri, g3, preferred_element_type=F32))

    row = lax.broadcasted_iota(jnp.int32, (CH, CH), 0)
    col = lax.broadcasted_iota(jnp.int32, (CH, CH), 1)
    causal = row >= col
    mid = CH // 2
    gn = gn_ref[...]

    for h in range(GLA_HEADS):
        ks = slice(h * GLA_DK, (h + 1) * GLA_DK)
        vs = slice(h * GLA_DV, (h + 1) * GLA_DV)
        S = S_ref[h]
        for sc in range(TB // CH):
            rs = slice(sc * CH, (sc + 1) * CH)
            b = bfull[rs, ks]
            if sc > 0:
                b = b - bfull[sc * CH - 1:sc * CH, ks]
            ref = b[mid:mid + 1]
            blast = b[CH - 1:CH]
            qf = q_ref[rs, ks].astype(F32) * (GLA_DK ** -0.5)
            kf = k_ref[rs, ks].astype(F32)
            vv = v_ref[rs, vs]
            qe = (qf * jnp.exp(b - ref)).astype(BF16)
            ke = (kf * jnp.exp(ref - b)).astype(BF16)
            a = lax.dot_general(qe, ke, _NT, preferred_element_type=F32)
            a = jnp.where(causal, a, 0.0)
            o = (jnp.dot(a.astype(BF16), vv, preferred_element_type=F32)
                 + lax.dot_general((qf * jnp.exp(b)).astype(BF16), S.astype(BF16), _NT,
                                   preferred_element_type=F32))
            kd = (kf * jnp.exp(blast - b)).astype(BF16)
            S = jnp.exp(blast) * S + lax.dot_general(vv, kd, _TN, preferred_element_type=F32)
            y = _rms(o, gn) * _silu(r_ref[rs, vs].astype(F32))
            y_ref[rs, vs] = y.astype(y_ref.dtype)
        S_ref[h] = S

    @pl.when(c == pl.num_programs(1) - 1)
    def _():
        st_ref[...] = S_ref[...]


def gla_scan(p, glog, s0t, gn, TB, CH):
    B, T, _ = p.shape
    tri = jnp.asarray(np.tril(np.ones((TB, TB), np.float32)), BF16)
    return pl.pallas_call(
        functools.partial(_gla_body, TB=TB, CH=CH),
        grid=(B, T // TB),
        in_specs=[pl.BlockSpec((None, TB, GLA_KD), lambda b, c: (b, c, 0)),
                  pl.BlockSpec((None, TB, GLA_KD), lambda b, c: (b, c, 1)),
                  pl.BlockSpec((None, TB, GLA_VD), lambda b, c: (b, c, 1)),
                  pl.BlockSpec((None, TB, GLA_VD), lambda b, c: (b, c, 2)),
                  pl.BlockSpec((None, TB, GLA_KD), lambda b, c: (b, c, 0)),
                  pl.BlockSpec((None, GLA_HEADS, GLA_DV, GLA_DK), lambda b, c: (b, 0, 0, 0)),
                  pl.BlockSpec((1, GLA_DV), lambda b, c: (0, 0)),
                  pl.BlockSpec((TB, TB), lambda b, c: (0, 0))],
        out_specs=[pl.BlockSpec((None, TB, GLA_VD), lambda b, c: (b, c, 0)),
                   pl.BlockSpec((None, GLA_HEADS, GLA_DV, GLA_DK), lambda b, c: (b, 0, 0, 0))],
        out_shape=[jax.ShapeDtypeStruct((B, T, GLA_VD), BF16),
                   jax.ShapeDtypeStruct((B, GLA_HEADS, GLA_DV, GLA_DK), F32)],
        scratch_shapes=[pltpu.VMEM((GLA_HEADS, GLA_DV, GLA_DK), F32)],
        compiler_params=_params("parallel", "arbitrary"),
        name="gla_scan",
    )(p, p, p, p, glog, s0t, gn, tri)


def _attn_body(q_ref, kp_ref, kc_ref, vp_ref, vc_ref, bias_ref, o_ref, lse_ref):
    var = jnp.where(pl.program_id(2) == 0, 0, 1)
    lane = lax.broadcasted_iota(jnp.int32, (DIL_BLOCK, LANES), 1)
    lo = lane < DIL_HD
    lse_acc = jnp.zeros((DIL_BLOCK, LANES), F32)
    zero = jnp.zeros((), BF16)
    for hp in range(DIL_HEADS // 2):
        cs = slice(hp * LANES, (hp + 1) * LANES)
        q2 = q_ref[:, cs]
        qab = jnp.concatenate([jnp.where(lo, q2, zero), jnp.where(lo, zero, q2)], axis=0)
        k2 = jnp.concatenate([kp_ref[:, cs], kc_ref[:, cs]], axis=0)
        v2 = jnp.concatenate([vp_ref[:, cs], vc_ref[:, cs]], axis=0)
        s = lax.dot_general(qab, k2, _NT, preferred_element_type=F32) + bias_ref[var, hp]
        m = jnp.max(s, axis=-1, keepdims=True)
        p = jnp.exp(s - m)
        l = jnp.sum(p, axis=-1, keepdims=True)
        o = jnp.dot(p.astype(BF16), v2, preferred_element_type=F32) / l
        o_ref[:, cs] = jnp.where(lo, o[:DIL_BLOCK], o[DIL_BLOCK:])
        lse = m + jnp.log(l)
        lse_acc = jnp.where(lane == 2 * hp, lse[:DIL_BLOCK], lse_acc)
        lse_acc = jnp.where(lane == 2 * hp + 1, lse[DIL_BLOCK:], lse_acc)
    lse_ref[...] = lse_acc


def dil_attention(qkv, bias):
    B, d, L, _ = qkv.shape
    nb = L // DIL_BLOCK
    blk = (None, None, DIL_BLOCK, DIL_WIDTH)
    prev = lambda n: jnp.maximum(n - 1, 0)
    return pl.pallas_call(
        _attn_body,
        grid=(B, d, nb),
        in_specs=[pl.BlockSpec(blk, lambda b, r, n: (b, r, n, 0)),
                  pl.BlockSpec(blk, lambda b, r, n: (b, r, prev(n), 1)),
                  pl.BlockSpec(blk, lambda b, r, n: (b, r, n, 1)),
                  pl.BlockSpec(blk, lambda b, r, n: (b, r, prev(n), 2)),
                  pl.BlockSpec(blk, lambda b, r, n: (b, r, n, 2)),
                  pl.BlockSpec((2, DIL_HEADS // 2, 2 * DIL_BLOCK, 2 * DIL_BLOCK), lambda b, r, n: (0, 0, 0, 0))],
        out_specs=[pl.BlockSpec(blk, lambda b, r, n: (b, r, n, 0)),
                   pl.BlockSpec((None, None, DIL_BLOCK, LANES), lambda b, r, n: (b, r, n, 0))],
        out_shape=[jax.ShapeDtypeStruct((B, d, L, DIL_WIDTH), F32),
                   jax.ShapeDtypeStruct((B, d, L, LANES), F32)],
        compiler_params=_params("parallel", "parallel", "arbitrary"),
        name="dil_attention",
    )(qkv, qkv, qkv, qkv, qkv, bias)


def _dilout_body(o0_ref, o1_ref, o2_ref, l0_ref, l1_ref, l2_ref, x_ref, w_ref, e_ref, out_ref):
    l0, l1, l2 = l0_ref[...], l1_ref[...], l2_ref[...]
    mx = jnp.maximum(jnp.maximum(l0, l1), l2)
    e0, e1, e2 = jnp.exp(l0 - mx), jnp.exp(l1 - mx), jnp.exp(l2 - mx)
    den = e0 + e1 + e2
    ex = e_ref[...]

    def expand(w):
        w1 = w.astype(BF16)
        w2 = (w - w1.astype(F32)).astype(BF16)
        return (jnp.dot(w1, ex, preferred_element_type=F32) + jnp.dot(w2, ex, preferred_element_type=F32))

    o = (expand(e0 / den) * o0_ref[...] + expand(e1 / den) * o1_ref[...] + expand(e2 / den) * o2_ref[...])
    out_ref[...] = x_ref[...] + jnp.dot(o.astype(BF16), w_ref[...], preferred_element_type=F32)


def dil_out(os, ls, x, w, expand, tm):
    M = x.shape[0]
    ospec = pl.BlockSpec((tm, DIL_WIDTH), lambda i: (i, 0))
    lspec = pl.BlockSpec((tm, LANES), lambda i: (i, 0))
    return pl.pallas_call(
        _dilout_body,
        grid=(M // tm,),
        in_specs=[ospec, ospec, ospec, lspec, lspec, lspec, ospec,
                  pl.BlockSpec((DIL_WIDTH, D_MODEL), lambda i: (0, 0)),
                  pl.BlockSpec((LANES, DIL_WIDTH), lambda i: (0, 0))],
        out_specs=pl.BlockSpec((tm, D_MODEL), lambda i: (i, 0)),
        out_shape=jax.ShapeDtypeStruct((M, D_MODEL), F32),
        compiler_params=_params("parallel"),
        name="dil_out",
    )(*os, *ls, x, w, expand)


S_CHUNK = 512
S_NEWPAD = LANES


def _sattn_body(qkv_ref, k0_ref, v0_ref, k1_ref, v1_ref, k2_ref, v2_ref, b0_ref, b1_ref, b2_ref, bn_ref,
                o_ref, m_ref, l_ref, acc_ref, *, seq):
    c = pl.program_id(1)
    R = seq * DIL_HEADS
    rowh = lax.broadcasted_iota(jnp.int32, (R, DIL_WIDTH), 0) % DIL_HEADS
    colh = lax.broadcasted_iota(jnp.int32, (R, DIL_WIDTH), 1) // DIL_HD
    hmask = rowh == colh

    def seg(g, part):
        return qkv_ref[:, (3 * g + part) * DIL_WIDTH:(3 * g + part + 1) * DIL_WIDTH]

    def qbd(g):
        q = seg(g, 0)
        qrep = jnp.concatenate([jnp.broadcast_to(q[s:s + 1], (DIL_HEADS, DIL_WIDTH)) for s in range(seq)], axis=0)
        return jnp.where(hmask, qrep, 0.0).astype(BF16)

    def update(s, pv_fn):
        m_old = m_ref[...]
        m_new = jnp.maximum(m_old, jnp.max(s, axis=-1, keepdims=True))
        alpha = jnp.exp(m_old - m_new)
        p = jnp.exp(s - m_new)
        l_ref[...] = alpha * l_ref[...] + jnp.sum(p, axis=-1, keepdims=True)
        acc_ref[...] = alpha * acc_ref[...] + pv_fn(p.astype(BF16))
        m_ref[...] = m_new

    def cache_segment(g, kt_ref, vt_ref, b_ref):
        s = jnp.dot(qbd(g), kt_ref[...].astype(BF16), preferred_element_type=F32) + b_ref[...]
        vt = vt_ref[...].astype(BF16)
        update(s, lambda p: lax.dot_general(p, vt, _NT, preferred_element_type=F32))

    def new_segment(g):
        pad = jnp.zeros((S_NEWPAD - seq, DIL_WIDTH), F32)
        kn = jnp.concatenate([seg(g, 1), pad], axis=0).astype(BF16)
        vn = jnp.concatenate([seg(g, 2), pad], axis=0).astype(BF16)
        s = lax.dot_general(qbd(g), kn, _NT, preferred_element_type=F32) + bn_ref[g]
        update(s, lambda p: jnp.dot(p, vn, preferred_element_type=F32))

    @pl.when(c == 0)
    def _():
        m_ref[...] = jnp.full(m_ref.shape, 2 * NEG, F32)
        l_ref[...] = jnp.zeros(l_ref.shape, F32)
        acc_ref[...] = jnp.zeros(acc_ref.shape, F32)
        for g in range(N_GROUPS):
            new_segment(g)
        cache_segment(0, k0_ref, v0_ref, b0_ref)
        cache_segment(1, k1_ref, v1_ref, b1_ref)

    cache_segment(2, k2_ref, v2_ref, b2_ref)

    @pl.when(c == pl.num_programs(1) - 1)
    def _():
        on = jnp.where(hmask, acc_ref[...] / l_ref[...], 0.0).astype(BF16)
        srow = lax.broadcasted_iota(jnp.int32, (8, R), 0)
        scol = lax.broadcasted_iota(jnp.int32, (8, R), 1) // DIL_HEADS
        sel = jnp.where(srow == scol, 1.0, 0.0).astype(BF16)
        o_ref[...] = jnp.dot(sel, on, preferred_element_type=F32)[:seq]


def dil_sample_attention(li, qkv, kts, vts, bbs, bn):
    DB, S, _ = qkv.shape
    R = DIL_HEADS * S
    W2 = kts[2].shape[-1]
    in_specs = [pl.BlockSpec((None, S, qkv.shape[-1]), lambda b, c: (b, 0, 0))]
    args = [qkv]
    for g in range(2):
        for a in (kts[g], vts[g]):
            in_specs.append(pl.BlockSpec((None, None, DIL_WIDTH, a.shape[-1]), lambda b, c: (li, b, 0, 0)))
            args.append(a)
    for a in (kts[2], vts[2]):
        in_specs.append(pl.BlockSpec((None, None, DIL_WIDTH, S_CHUNK), lambda b, c: (li, b, 0, c)))
        args.append(a)
    for g in range(2):
        in_specs.append(pl.BlockSpec(bbs[g].shape, lambda b, c: (0, 0)))
        args.append(bbs[g])
    in_specs.append(pl.BlockSpec((R, S_CHUNK), lambda b, c: (0, c)))
    args.append(bbs[2])
    in_specs.append(pl.BlockSpec(bn.shape, lambda b, c: (0, 0, 0)))
    args.append(bn)
    return pl.pallas_call(
        functools.partial(_sattn_body, seq=S),
        grid=(DB, W2 // S_CHUNK),
        in_specs=in_specs,
        out_specs=pl.BlockSpec((None, S, DIL_WIDTH), lambda b, c: (b, 0, 0)),
        out_shape=jax.ShapeDtypeStruct((DB, S, DIL_WIDTH), F32),
        scratch_shapes=[pltpu.VMEM((R, 1), F32), pltpu.VMEM((R, 1), F32), pltpu.VMEM((R, DIL_WIDTH), F32)],
        compiler_params=_params("parallel", "arbitrary"),
        name="dil_sample_attention",
    )(*args)


FFN_CHUNK = MXU_WIDTH
HALO = 16


def _conv_gate(u_scr, ug, uv, hg, hv, cw_g, cw_v, cb_g, cb_v, rows):
    out = []
    for idx, (u, hu, cw, cb) in enumerate(((ug, hg, cw_g, cb_g), (uv, hv, cw_v, cb_v))):
        u_scr[idx, 0:HALO] = hu
        u_scr[idx, HALO:HALO + rows] = u
        u1 = u_scr[idx, HALO - 1:HALO - 1 + rows]
        u2 = u_scr[idx, HALO - 2:HALO - 2 + rows]
        out.append(cb + (cw[0:1] * u2 + cw[1:2] * u1 + cw[2:3] * u))
    return _silu(out[0]) * out[1]


def _ffnup_body(x_ref, xh_ref, g_ref, wg_ref, wv_ref, cw_ref, cb_ref, act_ref, tail_ref, u_scr, *, tm):
    first = pl.program_id(1) == 0
    h = _rms(x_ref[...], g_ref[...]).astype(BF16)
    hh = _rms(xh_ref[...], g_ref[...])
    hh = jnp.where(first, 0.0, hh).astype(BF16)
    for c in range(D_FF // FFN_CHUNK):
        cs = slice(c * FFN_CHUNK, (c + 1) * FFN_CHUNK)
        vsl = slice(D_FF + c * FFN_CHUNK, D_FF + (c + 1) * FFN_CHUNK)
        ug = jnp.dot(h, wg_ref[:, cs], preferred_element_type=F32)
        uv = jnp.dot(h, wv_ref[:, cs], preferred_element_type=F32)
        hg = jnp.dot(hh, wg_ref[:, cs], preferred_element_type=F32)
        hv = jnp.dot(hh, wv_ref[:, cs], preferred_element_type=F32)
        act = _conv_gate(u_scr, ug, uv, hg, hv, cw_ref[:, cs], cw_ref[:, vsl], cb_ref[:, cs], cb_ref[:, vsl], tm)
        act_ref[:, cs] = act.astype(act_ref.dtype)
        tail_ref[:, cs] = ug[tm - 8:tm]
        tail_ref[:, vsl] = uv[tm - 8:tm]


def ffn_up_prompt(x, g, wg, wv, cw, cb, tm):
    B, T, D = x.shape
    hblk = tm // HALO
    return pl.pallas_call(
        functools.partial(_ffnup_body, tm=tm),
        grid=(B, T // tm),
        in_specs=[pl.BlockSpec((None, tm, D), lambda b, i: (b, i, 0)),
                  pl.BlockSpec((None, HALO, D), lambda b, i: (b, jnp.maximum(i * hblk - 1, 0), 0)),
                  pl.BlockSpec((1, D), lambda b, i: (0, 0)),
                  pl.BlockSpec((D, D_FF), lambda b, i: (0, 0)),
                  pl.BlockSpec((D, D_FF), lambda b, i: (0, 1)),
                  pl.BlockSpec((CONV_WIDTH, 2 * D_FF), lambda b, i: (0, 0)),
                  pl.BlockSpec((1, 2 * D_FF), lambda b, i: (0, 0))],
        out_specs=[pl.BlockSpec((None, tm, D_FF), lambda b, i: (b, i, 0)),
                   pl.BlockSpec((None, 8, 2 * D_FF), lambda b, i: (b, 0, 0))],
        out_shape=[jax.ShapeDtypeStruct((B, T, D_FF), BF16),
                   jax.ShapeDtypeStruct((B, 8, 2 * D_FF), F32)],
        scratch_shapes=[pltpu.VMEM((2, HALO + tm, FFN_CHUNK), F32)],
        compiler_params=_params("parallel", "arbitrary"),
        name="ffn_up_prompt",
    )(x, x, g, wg, wv, cw, cb)


def _ffnup_s_body(x_ref, g_ref, wg_ref, wv_ref, cw_ref, cb_ref, c1_ref, c2_ref, act_ref, u_ref, u_scr, *, rows, seq):
    h = _rms(x_ref[...], g_ref[...]).astype(BF16)
    pos = lax.broadcasted_iota(jnp.int32, (rows, FFN_CHUNK), 0) % seq
    u_scr[:, 0:HALO] = jnp.zeros((2, HALO, FFN_CHUNK), F32)
    for c in range(D_FF // FFN_CHUNK):
        acts = []
        for idx, off in enumerate((0, D_FF)):
            cs = slice(off + c * FFN_CHUNK, off + (c + 1) * FFN_CHUNK)
            w_ref = wg_ref if idx == 0 else wv_ref
            u = jnp.dot(h, w_ref[:, c * FFN_CHUNK:(c + 1) * FFN_CHUNK], preferred_element_type=F32)
            u_scr[idx, HALO:HALO + rows] = u
            u1 = jnp.where(pos >= 1, u_scr[idx, HALO - 1:HALO - 1 + rows], c1_ref[:, cs])
            u2 = jnp.where(pos >= 2, u_scr[idx, HALO - 2:HALO - 2 + rows], c2_ref[:, cs])
            acts.append(cb_ref[:, cs] + (cw_ref[0:1, cs] * u2 + cw_ref[1:2, cs] * u1 + cw_ref[2:3, cs] * u))
            u_ref[:, cs] = u
        act_ref[:, c * FFN_CHUNK:(c + 1) * FFN_CHUNK] = (_silu(acts[0]) * acts[1]).astype(act_ref.dtype)


def ffn_up_sample(x, g, wg, wv, cw, cb, carry1, carry2, seq):
    M, D = x.shape
    full = lambda shape: pl.BlockSpec(shape, lambda i: tuple(0 for _ in shape))
    return pl.pallas_call(
        functools.partial(_ffnup_s_body, rows=M, seq=seq),
        grid=(1,),
        in_specs=[full((M, D)), full((1, D)),
                  pl.BlockSpec((D, D_FF), lambda i: (0, 0)),
                  pl.BlockSpec((D, D_FF), lambda i: (0, 1)),
                  full((CONV_WIDTH, 2 * D_FF)), full((1, 2 * D_FF)),
                  full((M, 2 * D_FF)), full((M, 2 * D_FF))],
        out_specs=[full((M, D_FF)), full((M, 2 * D_FF))],
        out_shape=[jax.ShapeDtypeStruct((M, D_FF), BF16),
                   jax.ShapeDtypeStruct((M, 2 * D_FF), F32)],
        scratch_shapes=[pltpu.VMEM((2, HALO + M, FFN_CHUNK), F32)],
        compiler_params=_params("arbitrary"),
        name="ffn_up_sample",
    )(x, g, wg, wv, cw, cb, carry1, carry2)


def _rel_bucket(dist):
    max_exact = NUM_BUCKETS // 2
    df = jnp.maximum(dist, 1).astype(F32)
    large = max_exact + (jnp.log(df / max_exact) / math.log(MAX_DISTANCE / max_exact)
                         * (NUM_BUCKETS - max_exact)).astype(jnp.int32)
    large = jnp.minimum(large, NUM_BUCKETS - 1)
    return jnp.where(dist < max_exact, dist, large)


def _step_table(rel_bias, g, d):
    J = DIL_GROUPS[g][0] // d
    tab = rel_bias[:, g * DIL_HEADS:(g + 1) * DIL_HEADS].astype(F32)
    return tab[_rel_bucket(jnp.arange(J + 1) * d)]


def _lookup(table, idx):
    onehot = jax.nn.one_hot(jnp.asarray(idx, jnp.int32), table.shape[0], dtype=F32)
    return jnp.einsum('...j,jh->...h', onehot, table, precision=lax.Precision.HIGHEST)


def _prompt_bias(rel_bias, g, d):
    J = DIL_GROUPS[g][0] // d
    qi = np.arange(DIL_BLOCK)[:, None]
    ki = np.arange(2 * DIL_BLOCK)[None, :]
    rel = qi + DIL_BLOCK - ki
    bias = jnp.moveaxis(_lookup(_step_table(rel_bias, g, d), np.clip(rel, 0, J)), -1, 0)
    out = []
    for first in (True, False):
        valid = (rel >= 0) & (rel <= J) & ((ki >= DIL_BLOCK) | (not first))
        b = jnp.where(jnp.asarray(valid)[None], bias, NEG)
        out.append(b.reshape(DIL_HEADS // 2, 2 * DIL_BLOCK, 2 * DIL_BLOCK))
    return jnp.stack(out)


def _sample_bias(rel_bias, g, d, seq):
    W = DIL_GROUPS[g][0]
    J = W // d
    table = _step_table(rel_bias, g, d)
    s = np.arange(seq)[:, None]
    dist = W + s - np.arange(W)[None, :]
    valid = (dist >= 0) & (dist <= W) & (dist % d == 0)
    bval = jnp.moveaxis(_lookup(table, np.where(valid, dist // d, 0)), -1, 1)
    bbuf = jnp.where(jnp.asarray(valid)[:, None, :], bval, NEG).reshape(seq * DIL_HEADS, W)
    dn = s - np.arange(S_NEWPAD)[None, :]
    vn = (dn >= 0) & (dn % d == 0) & (dn // d <= J) & (np.arange(S_NEWPAD)[None, :] < seq)
    bnew = jnp.moveaxis(_lookup(table, np.where(vn, dn // d, 0)), -1, 1)
    bnew = jnp.where(jnp.asarray(vn)[:, None, :], bnew, NEG).reshape(seq * DIL_HEADS, S_NEWPAD)
    return bbuf, bnew


def _colgain(q_gain, k_gain, ngroups):
    seg = jnp.concatenate([jnp.tile(q_gain.astype(F32) * DIL_SCALE, DIL_HEADS),
                           jnp.tile(k_gain.astype(F32), DIL_HEADS),
                           jnp.ones((DIL_WIDTH,), F32)])
    return jnp.tile(seg, ngroups)[None, :]


def kernel(x_prompt, x_sample, state_gla, cache_k_g0, cache_v_g0, cache_k_g1, cache_v_g1, cache_k_g2, cache_v_g2,
           state_ffn_conv, rel_bias, norm_mix, norm_ffn, gla_w_in, gla_w_gate2, gla_b_gate, gla_norm, gla_w_out,
           dil_w_in, dil_q_norm, dil_k_norm, dil_w_out, ffn_w_up, ffn_conv_w, ffn_conv_b, ffn_w_down):
    B, T, D = x_prompt.shape
    DB, S, _ = x_sample.shape
    MP, MS = B * T, DB * S
    assert all(c.shape[2] == w for c, (w, _) in zip((cache_k_g0, cache_k_g1, cache_k_g2), DIL_GROUPS))

    xp = x_prompt.reshape(MP, D)
    xs = x_sample.reshape(MS, D)
    fmajor = lambda c: c.transpose(0, 1, 3, 4, 2).reshape(c.shape[0], c.shape[1], DIL_WIDTH, c.shape[2])
    k_caches = tuple(fmajor(c) for c in (cache_k_g0, cache_k_g1, cache_k_g2))
    v_caches = tuple(fmajor(c) for c in (cache_v_g0, cache_v_g1, cache_v_g2))

    blockdiag = jnp.asarray(np.kron(np.eye(MXU_WIDTH // DIL_HD), np.ones((DIL_HD, DIL_HD))), BF16)
    expand = jnp.asarray(np.kron(np.eye(LANES, DIL_HEADS, dtype=np.float32).reshape(LANES, DIL_HEADS),
                                 np.ones((1, DIL_HD), np.float32)), BF16)
    p_bias = [_prompt_bias(rel_bias, g, d) for g, (_, d) in enumerate(DIL_GROUPS)]
    s_bias = [_sample_bias(rel_bias, g, d, S) for g, (_, d) in enumerate(DIL_GROUPS)]
    s_bias_buf = [b for b, _ in s_bias]
    s_bias_new = jnp.stack([b for _, b in s_bias])

    gla_p, gla_s = [], []
    kp = [[] for _ in DIL_GROUPS]
    vp = [[] for _ in DIL_GROUPS]
    kq = [[] for _ in DIL_GROUPS]
    vq = [[] for _ in DIL_GROUPS]
    conv_p, conv_s = [], []
    SPAD = 16

    for i in range(DEPTH):
        li = i // N_MIXERS
        gmix = norm_mix[i][None, :].astype(F32)
        if i % N_MIXERS == 0:
            w_main = gla_w_in[li][:, :2 * GLA_KD + 2 * GLA_VD].astype(BF16)
            wgz = jnp.pad(gla_w_in[li][:, 2 * GLA_KD + 2 * GLA_VD:], ((0, 0), (0, LANES - GLA_GATE_RANK))).astype(BF16)
            wg2 = jnp.pad(gla_w_gate2[li], ((0, LANES - GLA_GATE_RANK), (0, 0))).astype(BF16)
            bg = gla_b_gate[li][None, :].astype(F32)
            gn = gla_norm[li][None, :].astype(F32)
            w_out = gla_w_out[li].astype(BF16)
            pp = norm_matmul(xp, gmix, w_main, BF16, 1024, 512).reshape(B, T, -1)
            glp = gla_gate(xp, gmix, wgz, wg2, bg, 1024).reshape(B, T, GLA_KD)
            s0 = jnp.zeros((B, GLA_HEADS, GLA_DV, GLA_DK), F32)
            yp, stp = gla_scan(pp, glp, s0, gn, 128, GLA_CHUNK)
            xp = matmul_residual(yp.reshape(MP, GLA_VD), w_out, xp, 1024, 512)
            gla_p.append(jnp.swapaxes(stp, -1, -2))
            ps = norm_matmul(xs, gmix, w_main, BF16, MS, 512).reshape(DB, S, -1)
            gls = gla_gate(xs, gmix, wgz, wg2, bg, MS).reshape(DB, S, GLA_KD)
            ps = jnp.pad(ps, ((0, 0), (0, SPAD - S), (0, 0)))
            gls = jnp.pad(gls, ((0, 0), (0, SPAD - S), (0, 0)))
            ys, sts = gla_scan(ps, gls, jnp.swapaxes(state_gla[li].astype(F32), -1, -2), gn, SPAD, SPAD)
            xs = matmul_residual(ys[:, :S].reshape(MS, GLA_VD), w_out, xs, MS, 512)
            gla_s.append(jnp.swapaxes(sts, -1, -2))
        else:
            w_in = dil_w_in[li].astype(BF16)
            w_out = dil_w_out[li].astype(BF16)
            cg = _colgain(dil_q_norm[li], dil_k_norm[li], N_GROUPS)
            os_, ls_ = [], []
            for g, (W, d) in enumerate(DIL_GROUPS):
                L = T // d
                xg = xp if d == 1 else xp.reshape(B, L, d, D).transpose(0, 2, 1, 3).reshape(MP, D)
                qkv = norm_matmul_qk(xg, gmix, w_in, cg, blockdiag, BF16, 1024, 512, group=g)
                qkv = qkv.reshape(B, d, L, 3 * DIL_WIDTH)
                o, lse = dil_attention(qkv, p_bias[g])
                if d > 1:
                    o = o.transpose(0, 2, 1, 3)
                    lse = lse.transpose(0, 2, 1, 3)
                os_.append(o.reshape(MP, DIL_WIDTH))
                ls_.append(lse.reshape(MP, LANES))
                keep = min(W, T)
                tail = qkv[:, :, L - keep // d:, DIL_WIDTH:].transpose(0, 2, 1, 3).reshape(B, keep, 2, DIL_HEADS, DIL_HD)
                kp[g].append(tail[:, :, 0].astype(F32))
                vp[g].append(tail[:, :, 1].astype(F32))
            xp = dil_out(os_, ls_, xp, w_out, expand, 512)
            qkvs = norm_matmul_qk(xs, gmix, w_in, cg, blockdiag, F32, MS, 512)
            osamp = dil_sample_attention(li, qkvs.reshape(DB, S, -1), k_caches, v_caches, s_bias_buf, s_bias_new)
            qkvs = qkvs.reshape(DB, S, N_GROUPS, 3, DIL_HEADS, DIL_HD)
            for g in range(N_GROUPS):
                kq[g].append(qkvs[:, :, g, 1])
                vq[g].append(qkvs[:, :, g, 2])
            osamp = osamp.reshape(MS, DIL_WIDTH).astype(BF16)
            xs = matmul_residual(osamp, w_out, xs, MS, 512)

        gffn = norm_ffn[i][None, :].astype(F32)
        w_up = ffn_w_up[i].astype(BF16)
        w_down = ffn_w_down[i].astype(BF16)
        cw = ffn_conv_w[i].astype(F32)
        cb = ffn_conv_b[i][None, :].astype(F32)
        act, tail = ffn_up_prompt(xp.reshape(B, T, D), gffn, w_up, w_up, cw, cb, 512)
        conv_p.append(tail[:, 8 - (CONV_WIDTH - 1):])
        xp = matmul_residual(act.reshape(MP, D_FF), w_down, xp, 1024, 512)
        buf = state_ffn_conv[i].astype(F32)
        zeros = jnp.zeros((DB, S - 1, 2 * D_FF), F32)
        carry1 = jnp.concatenate([buf[:, 1:2], zeros], axis=1).reshape(MS, 2 * D_FF)
        carry2 = jnp.concatenate([buf, zeros[:, 1:]], axis=1).reshape(MS, 2 * D_FF)
        acts, us = ffn_up_sample(xs, gffn, w_up, w_up, cw, cb, carry1, carry2, S)
        conv_s.append(us.reshape(DB, S, 2 * D_FF)[:, S - (CONV_WIDTH - 1):])
        xs = matmul_residual(acts, w_down, xs, MS, 512)

    outs = [xp.reshape(B, T, D), xs.reshape(DB, S, D), jnp.stack(gla_p), jnp.stack(gla_s)]
    for g in range(N_GROUPS):
        outs += [jnp.stack(kp[g]), jnp.stack(kq[g]), jnp.stack(vp[g]), jnp.stack(vq[g])]
    outs += [jnp.stack(conv_p), jnp.stack(conv_s)]
    return tuple(outs)
```

```python
import functools
import math

import numpy as np
import jax
import jax.numpy as jnp
from jax import lax
from jax.experimental import pallas as pl
from jax.experimental.pallas import tpu as pltpu

F32 = jnp.float32
BF16 = jnp.bfloat16

D_MODEL = 1024
DEPTH = 4
N_MIXERS = 2
GLA_HEADS = 4
GLA_KD = 512
GLA_VD = 1024
GLA_DK = 128
GLA_DV = 256
GLA_GATE_RANK = 16
GLA_GATE_NORM = 16.0
GLA_CHUNK = 32
DIL_GROUPS = ((128, 1), (512, 4), (2048, 16))
N_GROUPS = 3
DIL_HEADS = 16
DIL_HD = 64
DIL_WIDTH = 1024
DIL_BLOCK = 128
DIL_SCALE = DIL_HD ** -0.5
NUM_BUCKETS = 32
MAX_DISTANCE = 2048
D_FF = 2816
CONV_WIDTH = 3
EPS = 1e-6
NEG = -1e30

LANES = 128
MXU_WIDTH = 256
VMEM_LIMIT = 48 * 1024 * 1024

_NT = (((1,), (1,)), ((), ()))
_TN = (((0,), (0,)), ((), ()))


def _params(*sem):
    return pltpu.CompilerParams(dimension_semantics=sem, vmem_limit_bytes=VMEM_LIMIT)


def _rms(x, g):
    return x * lax.rsqrt(jnp.mean(x * x, axis=-1, keepdims=True) + EPS) * g


def _silu(x):
    return x * (1.0 / (1.0 + jnp.exp(-x)))


def _cast_body(w_ref, o_ref):
    o_ref[...] = w_ref[...].astype(o_ref.dtype)


def to_bf16(w, rows):
    L, R, C = w.shape
    return pl.pallas_call(
        _cast_body,
        grid=(L, R // rows),
        in_specs=[pl.BlockSpec((None, rows, C), lambda l, i: (l, i, 0))],
        out_specs=pl.BlockSpec((None, rows, C), lambda l, i: (l, i, 0)),
        out_shape=jax.ShapeDtypeStruct((L, R, C), BF16),
        compiler_params=_params("parallel", "parallel"),
        name="to_bf16",
    )(w)


def _nm_body(x_ref, g_ref, w_ref, o_ref, h_ref):
    @pl.when(pl.program_id(1) == 0)
    def _():
        h_ref[...] = _rms(x_ref[...], g_ref[...]).astype(BF16)

    o_ref[...] = jnp.dot(h_ref[...], w_ref[...], preferred_element_type=F32).astype(o_ref.dtype)


def norm_matmul(x, g, w, layer, ncols, out_dtype, tm, tn):
    M, D = x.shape
    return pl.pallas_call(
        _nm_body,
        grid=(M // tm, ncols // tn),
        in_specs=[pl.BlockSpec((tm, D), lambda i, j: (i, 0)),
                  pl.BlockSpec((1, D), lambda i, j: (0, 0)),
                  pl.BlockSpec((None, D, tn), lambda i, j: (layer, 0, j))],
        out_specs=pl.BlockSpec((tm, tn), lambda i, j: (i, j)),
        out_shape=jax.ShapeDtypeStruct((M, ncols), out_dtype),
        scratch_shapes=[pltpu.VMEM((tm, D), BF16)],
        compiler_params=_params("parallel", "arbitrary"),
        name="norm_matmul",
    )(x, g, w)


PERM_ROWS = 16
QK_ROWBLOCK = 256


def _nmqk_body(x_ref, g_ref, w_ref, cg_ref, bd_ref, perm_ref, o_ref, h_ref, *, tm, tn, d):
    j = pl.program_id(2)
    rows = tm // d

    @pl.when(j == 0)
    def _():
        h = _rms(x_ref[...], g_ref[...]).astype(BF16)
        if d == 1:
            h_ref[...] = h
        else:
            sub = PERM_ROWS * d
            for s in range(tm // sub):
                hs = jnp.dot(perm_ref[...], h[s * sub:(s + 1) * sub], preferred_element_type=F32).astype(BF16)
                for r in range(d):
                    dst = r * rows + PERM_ROWS * s
                    h_ref[dst:dst + PERM_ROWS, :] = hs[r * PERM_ROWS:(r + 1) * PERM_ROWS]

    is_norm = ((j * tn) // DIL_WIDTH) % 3 != 2

    rb = min(QK_ROWBLOCK, tm)

    def emit(a, cs, y):
        yb = y.astype(o_ref.dtype)
        for r in range(d):
            lo, hi = max(a, r * rows), min(a + rb, (r + 1) * rows)
            if lo < hi:
                o_ref[r, lo - r * rows:hi - r * rows, cs] = yb[lo - a:hi - a]

    def chunks(norm):
        blocks = [(slice(c * MXU_WIDTH, (c + 1) * MXU_WIDTH), a)
                  for c in range(tn // MXU_WIDTH) for a in range(0, tm, rb)]
        proj = lambda cs, a: jnp.dot(h_ref[a:a + rb, :], w_ref[:, cs], preferred_element_type=F32)
        ahead = proj(*blocks[0])
        for n, (cs, a) in enumerate(blocks):
            pc = ahead
            if n + 1 < len(blocks):
                ahead = proj(*blocks[n + 1])
            if norm:
                ss = jnp.dot((pc * pc).astype(BF16), bd_ref[...], preferred_element_type=F32)
                pc = pc * lax.rsqrt(ss * (1.0 / DIL_HD) + EPS) * cg_ref[:, cs]
            emit(a, cs, pc)

    @pl.when(is_norm)
    def _():
        chunks(True)

    @pl.when(jnp.logical_not(is_norm))
    def _():
        chunks(False)


def norm_matmul_qk(x, g, w, layer, colgain, bd, out_dtype, tm, tn, d=1, group=None):
    B, T, D = x.shape
    N = w.shape[2] if group is None else 3 * DIL_WIDTH
    j0 = 0 if group is None else group * N // tn
    sub = PERM_ROWS * d
    pm = np.zeros((sub, sub), np.float32)
    for r in range(d):
        for i in range(PERM_ROWS):
            pm[r * PERM_ROWS + i, i * d + r] = 1.0
    return pl.pallas_call(
        functools.partial(_nmqk_body, tm=tm, tn=tn, d=d),
        grid=(B, T // tm, N // tn),
        in_specs=[pl.BlockSpec((None, tm, D), lambda b, i, j: (b, i, 0)),
                  pl.BlockSpec((1, D), lambda b, i, j: (0, 0)),
                  pl.BlockSpec((None, D, tn), lambda b, i, j: (layer, 0, j + j0)),
                  pl.BlockSpec((1, tn), lambda b, i, j: (0, j + j0)),
                  pl.BlockSpec((MXU_WIDTH, MXU_WIDTH), lambda b, i, j: (0, 0)),
                  pl.BlockSpec((sub, sub), lambda b, i, j: (0, 0))],
        out_specs=pl.BlockSpec((None, d, tm // d, tn), lambda b, i, j: (b, 0, i, j)),
        out_shape=jax.ShapeDtypeStruct((B, d, T // d, N), out_dtype),
        scratch_shapes=[pltpu.VMEM((tm, D), BF16)],
        compiler_params=_params("parallel", "parallel", "arbitrary"),
        name="norm_matmul_qk",
    )(x, g, w, colgain, bd, jnp.asarray(pm, BF16))


def _mmres_body(y_ref, w_ref, x_ref, o_ref):
    o_ref[...] = x_ref[...] + jnp.dot(y_ref[...], w_ref[...], preferred_element_type=F32)


def matmul_residual(y, w, layer, x, tm, tn):
    M, K = y.shape
    N = w.shape[2]
    return pl.pallas_call(
        _mmres_body,
        grid=(M // tm, N // tn),
        in_specs=[pl.BlockSpec((tm, K), lambda i, j: (i, 0)),
                  pl.BlockSpec((None, K, tn), lambda i, j: (layer, 0, j)),
                  pl.BlockSpec((tm, tn), lambda i, j: (i, j))],
        out_specs=pl.BlockSpec((tm, tn), lambda i, j: (i, j)),
        out_shape=jax.ShapeDtypeStruct((M, N), F32),
        compiler_params=_params("parallel", "parallel"),
        name="matmul_residual",
    )(y, w, x)


def _gate_body(x_ref, g_ref, wgz_ref, wg2_ref, b_ref, o_ref):
    h = _rms(x_ref[...], g_ref[...]).astype(BF16)
    gz = jnp.dot(h, wgz_ref[...], preferred_element_type=F32)
    z = jnp.dot(gz.astype(BF16), wg2_ref[...], preferred_element_type=F32) + b_ref[...]
    o_ref[...] = (jnp.minimum(z, 0.0) - jnp.log(1.0 + jnp.exp(-jnp.abs(z)))) * (1.0 / GLA_GATE_NORM)


def gla_gate(x, g, wgz, wg2, b, tm):
    M, D = x.shape
    return pl.pallas_call(
        _gate_body,
        grid=(M // tm,),
        in_specs=[pl.BlockSpec((tm, D), lambda i: (i, 0)),
                  pl.BlockSpec((1, D), lambda i: (0, 0)),
                  pl.BlockSpec((D, LANES), lambda i: (0, 0)),
                  pl.BlockSpec((LANES, GLA_KD), lambda i: (0, 0)),
                  pl.BlockSpec((1, GLA_KD), lambda i: (0, 0))],
        out_specs=pl.BlockSpec((tm, GLA_KD), lambda i: (i, 0)),
        out_shape=jax.ShapeDtypeStruct((M, GLA_KD), F32),
        compiler_params=_params("parallel"),
        name="gla_gate",
    )(x, g, wgz, wg2, b)


def _gla_body(q_ref, k_ref, v_ref, r_ref, gl_ref, s0_ref, gn_ref, tri_ref, y_ref, st_ref, S_ref, *, TB, CH):
    c = pl.program_id(1)

    @pl.when(c == 0)
    def _():
        S_ref[...] = s0_ref[...]

    gl = gl_ref[...]
    g1 = gl.astype(BF16)
    r1 = gl - g1.astype(F32)
    g2 = r1.astype(BF16)
    g3 = (r1 - g2.astype(F32)).astype(BF16)
    tri = tri_ref[...]
    bfull = (jnp.dot(tri, g1, preferred_element_type=F32)
             + jnp.dot(tri, g2, preferred_element_type=F32)
             + jnp.dot(tri, g3, preferred_element_type=F32))

    row = lax.broadcasted_iota(jnp.int32, (CH, CH), 0)
    col = lax.broadcasted_iota(jnp.int32, (CH, CH), 1)
    causal = row >= col
    mid = CH // 2
    gn = gn_ref[...]

    for h in range(GLA_HEADS):
        ks = slice(h * GLA_DK, (h + 1) * GLA_DK)
        vs = slice(h * GLA_DV, (h + 1) * GLA_DV)
        S = S_ref[h]
        for sc in range(TB // CH):
            rs = slice(sc * CH, (sc + 1) * CH)
            b = bfull[rs, ks]
            if sc > 0:
                b = b - bfull[sc * CH - 1:sc * CH, ks]
            ref = b[mid:mid + 1]
            blast = b[CH - 1:CH]
            qf = q_ref[rs, ks].astype(F32) * (GLA_DK ** -0.5)
            kf = k_ref[rs, ks].astype(F32)
            vv = v_ref[rs, vs]
            qe = (qf * jnp.exp(b - ref)).astype(BF16)
            ke = (kf * jnp.exp(ref - b)).astype(BF16)
            a = lax.dot_general(qe, ke, _NT, preferred_element_type=F32)
            a = jnp.where(causal, a, 0.0)
            o = (jnp.dot(a.astype(BF16), vv, preferred_element_type=F32)
                 + lax.dot_general((qf * jnp.exp(b)).astype(BF16), S.astype(BF16), _NT,
                                   preferred_element_type=F32))
            kd = (kf * jnp.exp(blast - b)).astype(BF16)
            S = jnp.exp(blast) * S + lax.dot_general(vv, kd, _TN, preferred_element_type=F32)
            y = _rms(o, gn) * _silu(r_ref[rs, vs].astype(F32))
            y_ref[rs, vs] = y.astype(y_ref.dtype)
        S_ref[h] = S

    @pl.when(c == pl.num_programs(1) - 1)
    def _():
        st_ref[...] = S_ref[...]


def gla_scan(p, glog, s0t, gn, TB, CH):
    B, T, _ = p.shape
    tri = jnp.asarray(np.tril(np.ones((TB, TB), np.float32)), BF16)
    return pl.pallas_call(
        functools.partial(_gla_body, TB=TB, CH=CH),
        grid=(B, T // TB),
        in_specs=[pl.BlockSpec((None, TB, GLA_KD), lambda b, c: (b, c, 0)),
                  pl.BlockSpec((None, TB, GLA_KD), lambda b, c: (b, c, 1)),
                  pl.BlockSpec((None, TB, GLA_VD), lambda b, c: (b, c, 1)),
                  pl.BlockSpec((None, TB, GLA_VD), lambda b, c: (b, c, 2)),
                  pl.BlockSpec((None, TB, GLA_KD), lambda b, c: (b, c, 0)),
                  pl.BlockSpec((None, GLA_HEADS, GLA_DV, GLA_DK), lambda b, c: (b, 0, 0, 0)),
                  pl.BlockSpec((1, GLA_DV), lambda b, c: (0, 0)),
                  pl.BlockSpec((TB, TB), lambda b, c: (0, 0))],
        out_specs=[pl.BlockSpec((None, TB, GLA_VD), lambda b, c: (b, c, 0)),
                   pl.BlockSpec((None, GLA_HEADS, GLA_DV, GLA_DK), lambda b, c: (b, 0, 0, 0))],
        out_shape=[jax.ShapeDtypeStruct((B, T, GLA_VD), BF16),
                   jax.ShapeDtypeStruct((B, GLA_HEADS, GLA_DV, GLA_DK), F32)],
        scratch_shapes=[pltpu.VMEM((GLA_HEADS, GLA_DV, GLA_DK), F32)],
        compiler_params=_params("parallel", "arbitrary"),
        name="gla_scan",
    )(p, p, p, p, glog, s0t, gn, tri)


def _attn_body(q_ref, kp_ref, kc_ref, vp_ref, vc_ref, bias_ref, o_ref, lse_ref):
    var = jnp.where(pl.program_id(2) == 0, 0, 1)
    lane = lax.broadcasted_iota(jnp.int32, (DIL_BLOCK, LANES), 1)
    lo = lane < DIL_HD
    lse_acc = jnp.zeros((DIL_BLOCK, LANES), F32)
    zero = jnp.zeros((), BF16)
    for hp in range(DIL_HEADS // 2):
        cs = slice(hp * LANES, (hp + 1) * LANES)
        q2 = q_ref[:, cs]
        qab = jnp.concatenate([jnp.where(lo, q2, zero), jnp.where(lo, zero, q2)], axis=0)
        k2 = jnp.concatenate([kp_ref[:, cs], kc_ref[:, cs]], axis=0)
        v2 = jnp.concatenate([vp_ref[:, cs], vc_ref[:, cs]], axis=0)
        s = lax.dot_general(qab, k2, _NT, preferred_element_type=F32) + bias_ref[var, hp]
        m = jnp.max(s, axis=-1, keepdims=True)
        p = jnp.exp(s - m)
        l = jnp.sum(p, axis=-1, keepdims=True)
        o = jnp.dot(p.astype(BF16), v2, preferred_element_type=F32) / l
        o_ref[:, cs] = jnp.where(lo, o[:DIL_BLOCK], o[DIL_BLOCK:])
        lse = m + jnp.log(l)
        lse_acc = jnp.where(lane == 2 * hp, lse[:DIL_BLOCK], lse_acc)
        lse_acc = jnp.where(lane == 2 * hp + 1, lse[DIL_BLOCK:], lse_acc)
    lse_ref[...] = lse_acc


def dil_attention(qkv, bias):
    B, d, L, _ = qkv.shape
    nb = L // DIL_BLOCK
    blk = (None, None, DIL_BLOCK, DIL_WIDTH)
    prev = lambda n: jnp.maximum(n - 1, 0)
    return pl.pallas_call(
        _attn_body,
        grid=(B, d, nb),
        in_specs=[pl.BlockSpec(blk, lambda b, r, n: (b, r, n, 0)),
                  pl.BlockSpec(blk, lambda b, r, n: (b, r, prev(n), 1)),
                  pl.BlockSpec(blk, lambda b, r, n: (b, r, n, 1)),
                  pl.BlockSpec(blk, lambda b, r, n: (b, r, prev(n), 2)),
                  pl.BlockSpec(blk, lambda b, r, n: (b, r, n, 2)),
                  pl.BlockSpec((2, DIL_HEADS // 2, 2 * DIL_BLOCK, 2 * DIL_BLOCK), lambda b, r, n: (0, 0, 0, 0))],
        out_specs=[pl.BlockSpec(blk, lambda b, r, n: (b, r, n, 0)),
                   pl.BlockSpec((None, None, DIL_BLOCK, LANES), lambda b, r, n: (b, r, n, 0))],
        out_shape=[jax.ShapeDtypeStruct((B, d, L, DIL_WIDTH), F32),
                   jax.ShapeDtypeStruct((B, d, L, LANES), F32)],
        compiler_params=_params("parallel", "parallel", "arbitrary"),
        name="dil_attention",
    )(qkv, qkv, qkv, qkv, qkv, bias)


def _dilout_body(o0_ref, o1_ref, o2_ref, l0_ref, l1_ref, l2_ref, x_ref, w_ref, e_ref, out_ref):
    l0, l1, l2 = l0_ref[...], l1_ref[...], l2_ref[...]
    mx = jnp.maximum(jnp.maximum(l0, l1), l2)
    e0, e1, e2 = jnp.exp(l0 - mx), jnp.exp(l1 - mx), jnp.exp(l2 - mx)
    den = e0 + e1 + e2
    ex = e_ref[...]

    def expand(w):
        w1 = w.astype(BF16)
        w2 = (w - w1.astype(F32)).astype(BF16)
        return (jnp.dot(w1, ex, preferred_element_type=F32) + jnp.dot(w2, ex, preferred_element_type=F32))

    o = (expand(e0 / den) * o0_ref[...] + expand(e1 / den) * o1_ref[...] + expand(e2 / den) * o2_ref[...])
    out_ref[...] = x_ref[...] + jnp.dot(o.astype(BF16), w_ref[...], preferred_element_type=F32)


def dil_out(os, ls, x, w, layer, expand, tm):
    M = x.shape[0]
    ospec = pl.BlockSpec((tm, DIL_WIDTH), lambda i: (i, 0))
    lspec = pl.BlockSpec((tm, LANES), lambda i: (i, 0))
    return pl.pallas_call(
        _dilout_body,
        grid=(M // tm,),
        in_specs=[ospec, ospec, ospec, lspec, lspec, lspec, ospec,
                  pl.BlockSpec((None, DIL_WIDTH, D_MODEL), lambda i: (layer, 0, 0)),
                  pl.BlockSpec((LANES, DIL_WIDTH), lambda i: (0, 0))],
        out_specs=pl.BlockSpec((tm, D_MODEL), lambda i: (i, 0)),
        out_shape=jax.ShapeDtypeStruct((M, D_MODEL), F32),
        compiler_params=_params("parallel"),
        name="dil_out",
    )(*os, *ls, x, w, expand)


S_CHUNK = 512
S_NEWPAD = LANES


def _sattn_body(qkv_ref, k0_ref, v0_ref, k1_ref, v1_ref, k2_ref, v2_ref, b0_ref, b1_ref, b2_ref, bn_ref,
                o_ref, m_ref, l_ref, acc_ref, *, seq):
    c = pl.program_id(1)
    R = seq * DIL_HEADS
    rowh = lax.broadcasted_iota(jnp.int32, (R, DIL_WIDTH), 0) % DIL_HEADS
    colh = lax.broadcasted_iota(jnp.int32, (R, DIL_WIDTH), 1) // DIL_HD
    hmask = rowh == colh

    def seg(g, part):
        return qkv_ref[:, (3 * g + part) * DIL_WIDTH:(3 * g + part + 1) * DIL_WIDTH]

    def qbd(g):
        q = seg(g, 0)
        qrep = jnp.concatenate([jnp.broadcast_to(q[s:s + 1], (DIL_HEADS, DIL_WIDTH)) for s in range(seq)], axis=0)
        return jnp.where(hmask, qrep, 0.0).astype(BF16)

    def update(s, pv_fn):
        m_old = m_ref[...]
        m_new = jnp.maximum(m_old, jnp.max(s, axis=-1, keepdims=True))
        alpha = jnp.exp(m_old - m_new)
        p = jnp.exp(s - m_new)
        l_ref[...] = alpha * l_ref[...] + jnp.sum(p, axis=-1, keepdims=True)
        acc_ref[...] = alpha * acc_ref[...] + pv_fn(p.astype(BF16))
        m_ref[...] = m_new

    def cache_segment(g, kt_ref, vt_ref, b_ref):
        s = jnp.dot(qbd(g), kt_ref[...].astype(BF16), preferred_element_type=F32) + b_ref[...]
        vt = vt_ref[...].astype(BF16)
        update(s, lambda p: lax.dot_general(p, vt, _NT, preferred_element_type=F32))

    def new_segment(g):
        pad = jnp.zeros((S_NEWPAD - seq, DIL_WIDTH), F32)
        kn = jnp.concatenate([seg(g, 1), pad], axis=0).astype(BF16)
        vn = jnp.concatenate([seg(g, 2), pad], axis=0).astype(BF16)
        s = lax.dot_general(qbd(g), kn, _NT, preferred_element_type=F32) + bn_ref[g]
        update(s, lambda p: jnp.dot(p, vn, preferred_element_type=F32))

    @pl.when(c == 0)
    def _():
        m_ref[...] = jnp.full(m_ref.shape, 2 * NEG, F32)
        l_ref[...] = jnp.zeros(l_ref.shape, F32)
        acc_ref[...] = jnp.zeros(acc_ref.shape, F32)
        for g in range(N_GROUPS):
            new_segment(g)
        cache_segment(0, k0_ref, v0_ref, b0_ref)
        cache_segment(1, k1_ref, v1_ref, b1_ref)

    cache_segment(2, k2_ref, v2_ref, b2_ref)

    @pl.when(c == pl.num_programs(1) - 1)
    def _():
        on = jnp.where(hmask, acc_ref[...] / l_ref[...], 0.0).astype(BF16)
        srow = lax.broadcasted_iota(jnp.int32, (8, R), 0)
        scol = lax.broadcasted_iota(jnp.int32, (8, R), 1) // DIL_HEADS
        sel = jnp.where(srow == scol, 1.0, 0.0).astype(BF16)
        o_ref[...] = jnp.dot(sel, on, preferred_element_type=F32)[:seq]


def dil_sample_attention(li, qkv, kts, vts, bbs, bn):
    DB, S, _ = qkv.shape
    R = DIL_HEADS * S
    W2 = kts[2].shape[-1]
    in_specs = [pl.BlockSpec((None, S, qkv.shape[-1]), lambda b, c: (b, 0, 0))]
    args = [qkv]
    for g in range(2):
        for a in (kts[g], vts[g]):
            in_specs.append(pl.BlockSpec((None, None, DIL_WIDTH, a.shape[-1]), lambda b, c: (li, b, 0, 0)))
            args.append(a)
    for a in (kts[2], vts[2]):
        in_specs.append(pl.BlockSpec((None, None, DIL_WIDTH, S_CHUNK), lambda b, c: (li, b, 0, c)))
        args.append(a)
    for g in range(2):
        in_specs.append(pl.BlockSpec(bbs[g].shape, lambda b, c: (0, 0)))
        args.append(bbs[g])
    in_specs.append(pl.BlockSpec((R, S_CHUNK), lambda b, c: (0, c)))
    args.append(bbs[2])
    in_specs.append(pl.BlockSpec(bn.shape, lambda b, c: (0, 0, 0)))
    args.append(bn)
    return pl.pallas_call(
        functools.partial(_sattn_body, seq=S),
        grid=(DB, W2 // S_CHUNK),
        in_specs=in_specs,
        out_specs=pl.BlockSpec((None, S, DIL_WIDTH), lambda b, c: (b, 0, 0)),
        out_shape=jax.ShapeDtypeStruct((DB, S, DIL_WIDTH), F32),
        scratch_shapes=[pltpu.VMEM((R, 1), F32), pltpu.VMEM((R, 1), F32), pltpu.VMEM((R, DIL_WIDTH), F32)],
        compiler_params=_params("parallel", "arbitrary"),
        name="dil_sample_attention",
    )(*args)


FFN_CHUNK = MXU_WIDTH
HALO = 16


FFN_ROWBLOCK = 128


def _causal_conv(ext, cw, cb):
    u1 = pltpu.roll(ext, 1, axis=0)[HALO:]
    u2 = pltpu.roll(ext, 2, axis=0)[HALO:]
    return cb + (cw[0:1] * u2 + cw[1:2] * u1 + cw[2:3] * ext[HALO:])


def _ffnup_body(x_ref, xh_ref, g_ref, wg_ref, wv_ref, cw_ref, cb_ref, act_ref, tail_ref, h_ref, *, tm):
    first = pl.program_id(1) == 0
    hh = _rms(xh_ref[...], g_ref[...])
    h_ref[0:HALO] = jnp.where(first, 0.0, hh).astype(BF16)
    h_ref[HALO:HALO + tm] = _rms(x_ref[...], g_ref[...]).astype(BF16)
    for c in range(D_FF // FFN_CHUNK):
        cols = (slice(c * FFN_CHUNK, (c + 1) * FFN_CHUNK),
                slice(D_FF + c * FFN_CHUNK, D_FF + (c + 1) * FFN_CHUNK))
        tails = [None, None]
        for a in range(0, tm, FFN_ROWBLOCK):
            conv = []
            for idx, w_ref in enumerate((wg_ref, wv_ref)):
                if a == 0:
                    ext = jnp.dot(h_ref[0:HALO + FFN_ROWBLOCK], w_ref[:, cols[0]], preferred_element_type=F32)
                else:
                    u = jnp.dot(h_ref[HALO + a:HALO + a + FFN_ROWBLOCK], w_ref[:, cols[0]],
                                preferred_element_type=F32)
                    ext = jnp.concatenate([tails[idx], u], axis=0)
                tails[idx] = ext[FFN_ROWBLOCK:]
                conv.append(_causal_conv(ext, cw_ref[:, cols[idx]], cb_ref[:, cols[idx]]))
            act_ref[a:a + FFN_ROWBLOCK, cols[0]] = (_silu(conv[0]) * conv[1]).astype(act_ref.dtype)
        for idx in range(2):
            tail_ref[:, cols[idx]] = tails[idx][HALO - 8:]


def ffn_up_prompt(x, g, w, layer, cw, cb, tm):
    B, T, D = x.shape
    hblk = tm // HALO
    return pl.pallas_call(
        functools.partial(_ffnup_body, tm=tm),
        grid=(B, T // tm),
        in_specs=[pl.BlockSpec((None, tm, D), lambda b, i: (b, i, 0)),
                  pl.BlockSpec((None, HALO, D), lambda b, i: (b, jnp.maximum(i * hblk - 1, 0), 0)),
                  pl.BlockSpec((1, D), lambda b, i: (0, 0)),
                  pl.BlockSpec((None, D, D_FF), lambda b, i: (layer, 0, 0)),
                  pl.BlockSpec((None, D, D_FF), lambda b, i: (layer, 0, 1)),
                  pl.BlockSpec((CONV_WIDTH, 2 * D_FF), lambda b, i: (0, 0)),
                  pl.BlockSpec((1, 2 * D_FF), lambda b, i: (0, 0))],
        out_specs=[pl.BlockSpec((None, tm, D_FF), lambda b, i: (b, i, 0)),
                   pl.BlockSpec((None, 8, 2 * D_FF), lambda b, i: (b, 0, 0))],
        out_shape=[jax.ShapeDtypeStruct((B, T, D_FF), BF16),
                   jax.ShapeDtypeStruct((B, 8, 2 * D_FF), F32)],
        scratch_shapes=[pltpu.VMEM((HALO + tm, D), BF16)],
        compiler_params=_params("parallel", "arbitrary"),
        name="ffn_up_prompt",
    )(x, x, g, w, w, cw, cb)


def _ffnup_s_body(x_ref, g_ref, wg_ref, wv_ref, cw_ref, cb_ref, c1_ref, c2_ref, act_ref, u_ref, u_scr, *, rows, seq):
    h = _rms(x_ref[...], g_ref[...]).astype(BF16)
    pos = lax.broadcasted_iota(jnp.int32, (rows, FFN_CHUNK), 0) % seq
    u_scr[:, 0:HALO] = jnp.zeros((2, HALO, FFN_CHUNK), F32)
    for c in range(D_FF // FFN_CHUNK):
        acts = []
        for idx, off in enumerate((0, D_FF)):
            cs = slice(off + c * FFN_CHUNK, off + (c + 1) * FFN_CHUNK)
            w_ref = wg_ref if idx == 0 else wv_ref
            u = jnp.dot(h, w_ref[:, c * FFN_CHUNK:(c + 1) * FFN_CHUNK], preferred_element_type=F32)
            u_scr[idx, HALO:HALO + rows] = u
            u1 = jnp.where(pos >= 1, u_scr[idx, HALO - 1:HALO - 1 + rows], c1_ref[:, cs])
            u2 = jnp.where(pos >= 2, u_scr[idx, HALO - 2:HALO - 2 + rows], c2_ref[:, cs])
            acts.append(cb_ref[:, cs] + (cw_ref[0:1, cs] * u2 + cw_ref[1:2, cs] * u1 + cw_ref[2:3, cs] * u))
            u_ref[:, cs] = u
        act_ref[:, c * FFN_CHUNK:(c + 1) * FFN_CHUNK] = (_silu(acts[0]) * acts[1]).astype(act_ref.dtype)


def ffn_up_sample(x, g, w, layer, cw, cb, carry1, carry2, seq):
    M, D = x.shape
    full = lambda shape: pl.BlockSpec(shape, lambda i: tuple(0 for _ in shape))
    return pl.pallas_call(
        functools.partial(_ffnup_s_body, rows=M, seq=seq),
        grid=(1,),
        in_specs=[full((M, D)), full((1, D)),
                  pl.BlockSpec((None, D, D_FF), lambda i: (layer, 0, 0)),
                  pl.BlockSpec((None, D, D_FF), lambda i: (layer, 0, 1)),
                  full((CONV_WIDTH, 2 * D_FF)), full((1, 2 * D_FF)),
                  full((M, 2 * D_FF)), full((M, 2 * D_FF))],
        out_specs=[full((M, D_FF)), full((M, 2 * D_FF))],
        out_shape=[jax.ShapeDtypeStruct((M, D_FF), BF16),
                   jax.ShapeDtypeStruct((M, 2 * D_FF), F32)],
        scratch_shapes=[pltpu.VMEM((2, HALO + M, FFN_CHUNK), F32)],
        compiler_params=_params("arbitrary"),
        name="ffn_up_sample",
    )(x, g, w, w, cw, cb, carry1, carry2)


def _rel_bucket(dist):
    max_exact = NUM_BUCKETS // 2
    df = jnp.maximum(dist, 1).astype(F32)
    large = max_exact + (jnp.log(df / max_exact) / math.log(MAX_DISTANCE / max_exact)
                         * (NUM_BUCKETS - max_exact)).astype(jnp.int32)
    large = jnp.minimum(large, NUM_BUCKETS - 1)
    return jnp.where(dist < max_exact, dist, large)


def _step_table(rel_bias, g, d):
    J = DIL_GROUPS[g][0] // d
    tab = rel_bias[:, g * DIL_HEADS:(g + 1) * DIL_HEADS].astype(F32)
    return tab[_rel_bucket(jnp.arange(J + 1) * d)]


def _lookup(table, idx):
    onehot = jax.nn.one_hot(jnp.asarray(idx, jnp.int32), table.shape[0], dtype=F32)
    return jnp.einsum('...j,jh->...h', onehot, table, precision=lax.Precision.HIGHEST)


def _prompt_bias(rel_bias, g, d):
    J = DIL_GROUPS[g][0] // d
    qi = np.arange(DIL_BLOCK)[:, None]
    ki = np.arange(2 * DIL_BLOCK)[None, :]
    rel = qi + DIL_BLOCK - ki
    bias = jnp.moveaxis(_lookup(_step_table(rel_bias, g, d), np.clip(rel, 0, J)), -1, 0)
    out = []
    for first in (True, False):
        valid = (rel >= 0) & (rel <= J) & ((ki >= DIL_BLOCK) | (not first))
        b = jnp.where(jnp.asarray(valid)[None], bias, NEG)
        out.append(b.reshape(DIL_HEADS // 2, 2 * DIL_BLOCK, 2 * DIL_BLOCK))
    return jnp.stack(out)


def _sample_bias(rel_bias, g, d, seq):
    W = DIL_GROUPS[g][0]
    J = W // d
    table = _step_table(rel_bias, g, d)
    s = np.arange(seq)[:, None]
    dist = W + s - np.arange(W)[None, :]
    valid = (dist >= 0) & (dist <= W) & (dist % d == 0)
    bval = jnp.moveaxis(_lookup(table, np.where(valid, dist // d, 0)), -1, 1)
    bbuf = jnp.where(jnp.asarray(valid)[:, None, :], bval, NEG).reshape(seq * DIL_HEADS, W)
    dn = s - np.arange(S_NEWPAD)[None, :]
    vn = (dn >= 0) & (dn % d == 0) & (dn // d <= J) & (np.arange(S_NEWPAD)[None, :] < seq)
    bnew = jnp.moveaxis(_lookup(table, np.where(vn, dn // d, 0)), -1, 1)
    bnew = jnp.where(jnp.asarray(vn)[:, None, :], bnew, NEG).reshape(seq * DIL_HEADS, S_NEWPAD)
    return bbuf, bnew


def _colgain(q_gain, k_gain, ngroups):
    seg = jnp.concatenate([jnp.tile(q_gain.astype(F32) * DIL_SCALE, DIL_HEADS),
                           jnp.tile(k_gain.astype(F32), DIL_HEADS),
                           jnp.ones((DIL_WIDTH,), F32)])
    return jnp.tile(seg, ngroups)[None, :]


def kernel(x_prompt, x_sample, state_gla, cache_k_g0, cache_v_g0, cache_k_g1, cache_v_g1, cache_k_g2, cache_v_g2,
           state_ffn_conv, rel_bias, norm_mix, norm_ffn, gla_w_in, gla_w_gate2, gla_b_gate, gla_norm, gla_w_out,
           dil_w_in, dil_q_norm, dil_k_norm, dil_w_out, ffn_w_up, ffn_conv_w, ffn_conv_b, ffn_w_down):
    B, T, D = x_prompt.shape
    DB, S, _ = x_sample.shape
    MP, MS = B * T, DB * S
    assert all(c.shape[2] == w for c, (w, _) in zip((cache_k_g0, cache_k_g1, cache_k_g2), DIL_GROUPS))

    xp = x_prompt.reshape(MP, D)
    xs = x_sample.reshape(MS, D)
    fmajor = lambda c: c.transpose(0, 1, 3, 4, 2).reshape(c.shape[0], c.shape[1], DIL_WIDTH, c.shape[2])
    k_caches = tuple(fmajor(c) for c in (cache_k_g0, cache_k_g1, cache_k_g2))
    v_caches = tuple(fmajor(c) for c in (cache_v_g0, cache_v_g1, cache_v_g2))

    blockdiag = jnp.asarray(np.kron(np.eye(MXU_WIDTH // DIL_HD), np.ones((DIL_HD, DIL_HD))), BF16)
    expand = jnp.asarray(np.kron(np.eye(LANES, DIL_HEADS, dtype=np.float32).reshape(LANES, DIL_HEADS),
                                 np.ones((1, DIL_HD), np.float32)), BF16)
    p_bias = [_prompt_bias(rel_bias, g, d) for g, (_, d) in enumerate(DIL_GROUPS)]
    s_bias = [_sample_bias(rel_bias, g, d, S) for g, (_, d) in enumerate(DIL_GROUPS)]
    s_bias_buf = [b for b, _ in s_bias]
    s_bias_new = jnp.stack([b for _, b in s_bias])

    gla_p, gla_s = [], []
    kp = [[] for _ in DIL_GROUPS]
    vp = [[] for _ in DIL_GROUPS]
    kq = [[] for _ in DIL_GROUPS]
    vq = [[] for _ in DIL_GROUPS]
    conv_p, conv_s = [], []
    SPAD = 16

    gla_w_in_b = to_bf16(gla_w_in, 256)
    gla_w_out_b = to_bf16(gla_w_out, 256)
    dil_w_in_b = to_bf16(dil_w_in, 128)
    dil_w_out_b = to_bf16(dil_w_out, 256)
    ffn_w_up_b = to_bf16(ffn_w_up, 256)
    ffn_w_down_b = to_bf16(ffn_w_down, 256)
    n_main = 2 * GLA_KD + 2 * GLA_VD

    for i in range(DEPTH):
        li = i // N_MIXERS
        gmix = norm_mix[i][None, :].astype(F32)
        if i % N_MIXERS == 0:
            wgz = jnp.pad(gla_w_in[li][:, n_main:], ((0, 0), (0, LANES - GLA_GATE_RANK))).astype(BF16)
            wg2 = jnp.pad(gla_w_gate2[li], ((0, LANES - GLA_GATE_RANK), (0, 0))).astype(BF16)
            bg = gla_b_gate[li][None, :].astype(F32)
            gn = gla_norm[li][None, :].astype(F32)
            pp = norm_matmul(xp, gmix, gla_w_in_b, li, n_main, BF16, 1024, 1024).reshape(B, T, -1)
            glp = gla_gate(xp, gmix, wgz, wg2, bg, 1024).reshape(B, T, GLA_KD)
            s0 = jnp.zeros((B, GLA_HEADS, GLA_DV, GLA_DK), F32)
            yp, stp = gla_scan(pp, glp, s0, gn, 128, GLA_CHUNK)
            xp = matmul_residual(yp.reshape(MP, GLA_VD), gla_w_out_b, li, xp, 1024, D)
            gla_p.append(jnp.swapaxes(stp, -1, -2))
            ps = norm_matmul(xs, gmix, gla_w_in_b, li, n_main, BF16, MS, 1024).reshape(DB, S, -1)
            gls = gla_gate(xs, gmix, wgz, wg2, bg, MS).reshape(DB, S, GLA_KD)
            ps = jnp.pad(ps, ((0, 0), (0, SPAD - S), (0, 0)))
            gls = jnp.pad(gls, ((0, 0), (0, SPAD - S), (0, 0)))
            ys, sts = gla_scan(ps, gls, jnp.swapaxes(state_gla[li].astype(F32), -1, -2), gn, SPAD, SPAD)
            xs = matmul_residual(ys[:, :S].reshape(MS, GLA_VD), gla_w_out_b, li, xs, MS, D)
            gla_s.append(jnp.swapaxes(sts, -1, -2))
        else:
            cg = _colgain(dil_q_norm[li], dil_k_norm[li], N_GROUPS)
            os_, ls_ = [], []
            for g, (W, d) in enumerate(DIL_GROUPS):
                L = T // d
                qkv = norm_matmul_qk(xp.reshape(B, T, D), gmix, dil_w_in_b, li, cg, blockdiag, BF16,
                                     1024, 1024, d=d, group=g)
                o, lse = dil_attention(qkv, p_bias[g])
                if d > 1:
                    o = o.transpose(0, 2, 1, 3)
                    lse = lse.transpose(0, 2, 1, 3)
                os_.append(o.reshape(MP, DIL_WIDTH))
                ls_.append(lse.reshape(MP, LANES))
                keep = min(W, T)
                tail = qkv[:, :, L - keep // d:, DIL_WIDTH:].transpose(0, 2, 1, 3).reshape(B, keep, 2, DIL_HEADS, DIL_HD)
                kp[g].append(tail[:, :, 0].astype(F32))
                vp[g].append(tail[:, :, 1].astype(F32))
            xp = dil_out(os_, ls_, xp, dil_w_out_b, li, expand, 512)
            qkvs = norm_matmul_qk(xs.reshape(1, MS, D), gmix, dil_w_in_b, li, cg, blockdiag, F32, MS, 1024)
            osamp = dil_sample_attention(li, qkvs.reshape(DB, S, -1), k_caches, v_caches, s_bias_buf, s_bias_new)
            qkvs = qkvs.reshape(DB, S, N_GROUPS, 3, DIL_HEADS, DIL_HD)
            for g in range(N_GROUPS):
                kq[g].append(qkvs[:, :, g, 1])
                vq[g].append(qkvs[:, :, g, 2])
            osamp = osamp.reshape(MS, DIL_WIDTH).astype(BF16)
            xs = matmul_residual(osamp, dil_w_out_b, li, xs, MS, D)

        gffn = norm_ffn[i][None, :].astype(F32)
        cw = ffn_conv_w[i].astype(F32)
        cb = ffn_conv_b[i][None, :].astype(F32)
        act, tail = ffn_up_prompt(xp.reshape(B, T, D), gffn, ffn_w_up_b, i, cw, cb, 512)
        conv_p.append(tail[:, 8 - (CONV_WIDTH - 1):])
        xp = matmul_residual(act.reshape(MP, D_FF), ffn_w_down_b, i, xp, 1024, D)
        buf = state_ffn_conv[i].astype(F32)
        zeros = jnp.zeros((DB, S - 1, 2 * D_FF), F32)
        carry1 = jnp.concatenate([buf[:, 1:2], zeros], axis=1).reshape(MS, 2 * D_FF)
        carry2 = jnp.concatenate([buf, zeros[:, 1:]], axis=1).reshape(MS, 2 * D_FF)
        acts, us = ffn_up_sample(xs, gffn, ffn_w_up_b, i, cw, cb, carry1, carry2, S)
        conv_s.append(us.reshape(DB, S, 2 * D_FF)[:, S - (CONV_WIDTH - 1):])
        xs = matmul_residual(acts, ffn_w_down_b, i, xs, MS, D)

    outs = [xp.reshape(B, T, D), xs.reshape(DB, S, D), jnp.stack(gla_p), jnp.stack(gla_s)]
    for g in range(N_GROUPS):
        outs += [jnp.stack(kp[g]), jnp.stack(kq[g]), jnp.stack(vp[g]), jnp.stack(vq[g])]
    outs += [jnp.stack(conv_p), jnp.stack(conv_s)]
    return tuple(outs)
```

```python
import functools
import math

import numpy as np
import jax
import jax.numpy as jnp
from jax import lax
from jax.experimental import pallas as pl
from jax.experimental.pallas import tpu as pltpu

F32 = jnp.float32
BF16 = jnp.bfloat16

D_MODEL = 1024
DEPTH = 4
N_MIXERS = 2
GLA_HEADS = 4
GLA_KD = 512
GLA_VD = 1024
GLA_DK = 128
GLA_DV = 256
GLA_GATE_RANK = 16
GLA_GATE_NORM = 16.0
GLA_CHUNK = 32
DIL_GROUPS = ((128, 1), (512, 4), (2048, 16))
N_GROUPS = 3
DIL_HEADS = 16
DIL_HD = 64
DIL_WIDTH = 1024
DIL_BLOCK = 128
DIL_SCALE = DIL_HD ** -0.5
NUM_BUCKETS = 32
MAX_DISTANCE = 2048
D_FF = 2816
CONV_WIDTH = 3
EPS = 1e-6
NEG = -1e30

LANES = 128
MXU_WIDTH = 256
VMEM_LIMIT = 48 * 1024 * 1024

_NT = (((1,), (1,)), ((), ()))
_TN = (((0,), (0,)), ((), ()))


def _params(*sem):
    return pltpu.CompilerParams(dimension_semantics=sem, vmem_limit_bytes=VMEM_LIMIT)


def _rms(x, g):
    return x * lax.rsqrt(jnp.mean(x * x, axis=-1, keepdims=True) + EPS) * g


def _silu(x):
    return x * (1.0 / (1.0 + jnp.exp(-x)))


def _cast_body(w_ref, o_ref):
    o_ref[...] = w_ref[...].astype(o_ref.dtype)


def to_bf16(w, rows):
    L, R, C = w.shape
    return pl.pallas_call(
        _cast_body,
        grid=(L, R // rows),
        in_specs=[pl.BlockSpec((None, rows, C), lambda l, i: (l, i, 0))],
        out_specs=pl.BlockSpec((None, rows, C), lambda l, i: (l, i, 0)),
        out_shape=jax.ShapeDtypeStruct((L, R, C), BF16),
        compiler_params=_params("parallel", "parallel"),
        name="to_bf16",
    )(w)


def _nm_body(x_ref, g_ref, w_ref, o_ref, h_ref):
    @pl.when(pl.program_id(1) == 0)
    def _():
        h_ref[...] = _rms(x_ref[...], g_ref[...]).astype(BF16)

    o_ref[...] = jnp.dot(h_ref[...], w_ref[...], preferred_element_type=F32).astype(o_ref.dtype)


def norm_matmul(x, g, w, layer, ncols, out_dtype, tm, tn):
    M, D = x.shape
    return pl.pallas_call(
        _nm_body,
        grid=(M // tm, ncols // tn),
        in_specs=[pl.BlockSpec((tm, D), lambda i, j: (i, 0)),
                  pl.BlockSpec((1, D), lambda i, j: (0, 0)),
                  pl.BlockSpec((None, D, tn), lambda i, j: (layer, 0, j))],
        out_specs=pl.BlockSpec((tm, tn), lambda i, j: (i, j)),
        out_shape=jax.ShapeDtypeStruct((M, ncols), out_dtype),
        scratch_shapes=[pltpu.VMEM((tm, D), BF16)],
        compiler_params=_params("parallel", "arbitrary"),
        name="norm_matmul",
    )(x, g, w)


PERM_ROWS = 16
QK_ROWBLOCK = 256


def _nmqk_body(x_ref, g_ref, w_ref, cg_ref, bd_ref, perm_ref, o_ref, h_ref, *, tm, tn, d):
    j = pl.program_id(2)
    rows = tm // d

    @pl.when(j == 0)
    def _():
        h = _rms(x_ref[...], g_ref[...]).astype(BF16)
        if d == 1:
            h_ref[...] = h
        else:
            sub = PERM_ROWS * d
            for s in range(tm // sub):
                hs = jnp.dot(perm_ref[...], h[s * sub:(s + 1) * sub], preferred_element_type=F32).astype(BF16)
                for r in range(d):
                    dst = r * rows + PERM_ROWS * s
                    h_ref[dst:dst + PERM_ROWS, :] = hs[r * PERM_ROWS:(r + 1) * PERM_ROWS]

    is_norm = ((j * tn) // DIL_WIDTH) % 3 != 2

    rb = min(QK_ROWBLOCK, tm)

    def emit(a, cs, y):
        yb = y.astype(o_ref.dtype)
        for r in range(d):
            lo, hi = max(a, r * rows), min(a + rb, (r + 1) * rows)
            if lo < hi:
                o_ref[r, lo - r * rows:hi - r * rows, cs] = yb[lo - a:hi - a]

    def chunks(norm):
        blocks = [(slice(c * MXU_WIDTH, (c + 1) * MXU_WIDTH), a)
                  for c in range(tn // MXU_WIDTH) for a in range(0, tm, rb)]
        proj = lambda cs, a: jnp.dot(h_ref[a:a + rb, :], w_ref[:, cs], preferred_element_type=F32)
        ahead = proj(*blocks[0])
        for n, (cs, a) in enumerate(blocks):
            pc = ahead
            if n + 1 < len(blocks):
                ahead = proj(*blocks[n + 1])
            if norm:
                ss = jnp.dot((pc * pc).astype(BF16), bd_ref[...], preferred_element_type=F32)
                pc = pc * lax.rsqrt(ss * (1.0 / DIL_HD) + EPS) * cg_ref[:, cs]
            emit(a, cs, pc)

    @pl.when(is_norm)
    def _():
        chunks(True)

    @pl.when(jnp.logical_not(is_norm))
    def _():
        chunks(False)


def norm_matmul_qk(x, g, w, layer, colgain, bd, out_dtype, tm, tn, d=1, group=None):
    B, T, D = x.shape
    N = w.shape[2] if group is None else 3 * DIL_WIDTH
    j0 = 0 if group is None else group * N // tn
    sub = PERM_ROWS * d
    pm = np.zeros((sub, sub), np.float32)
    for r in range(d):
        for i in range(PERM_ROWS):
            pm[r * PERM_ROWS + i, i * d + r] = 1.0
    return pl.pallas_call(
        functools.partial(_nmqk_body, tm=tm, tn=tn, d=d),
        grid=(B, T // tm, N // tn),
        in_specs=[pl.BlockSpec((None, tm, D), lambda b, i, j: (b, i, 0)),
                  pl.BlockSpec((1, D), lambda b, i, j: (0, 0)),
                  pl.BlockSpec((None, D, tn), lambda b, i, j: (layer, 0, j + j0)),
                  pl.BlockSpec((1, tn), lambda b, i, j: (0, j + j0)),
                  pl.BlockSpec((MXU_WIDTH, MXU_WIDTH), lambda b, i, j: (0, 0)),
                  pl.BlockSpec((sub, sub), lambda b, i, j: (0, 0))],
        out_specs=pl.BlockSpec((None, d, tm // d, tn), lambda b, i, j: (b, 0, i, j)),
        out_shape=jax.ShapeDtypeStruct((B, d, T // d, N), out_dtype),
        scratch_shapes=[pltpu.VMEM((tm, D), BF16)],
        compiler_params=_params("parallel", "parallel", "arbitrary"),
        name="norm_matmul_qk",
    )(x, g, w, colgain, bd, jnp.asarray(pm, BF16))


def _mmres_body(y_ref, w_ref, x_ref, o_ref):
    o_ref[...] = x_ref[...] + jnp.dot(y_ref[...], w_ref[...], preferred_element_type=F32)


def matmul_residual(y, w, layer, x, tm, tn):
    M, K = y.shape
    N = w.shape[2]
    return pl.pallas_call(
        _mmres_body,
        grid=(M // tm, N // tn),
        in_specs=[pl.BlockSpec((tm, K), lambda i, j: (i, 0)),
                  pl.BlockSpec((None, K, tn), lambda i, j: (layer, 0, j)),
                  pl.BlockSpec((tm, tn), lambda i, j: (i, j))],
        out_specs=pl.BlockSpec((tm, tn), lambda i, j: (i, j)),
        out_shape=jax.ShapeDtypeStruct((M, N), F32),
        compiler_params=_params("parallel", "parallel"),
        name="matmul_residual",
    )(y, w, x)


def _gate_body(x_ref, g_ref, wgz_ref, wg2_ref, b_ref, o_ref):
    h = _rms(x_ref[...], g_ref[...]).astype(BF16)
    gz = jnp.dot(h, wgz_ref[...], preferred_element_type=F32)
    z = jnp.dot(gz.astype(BF16), wg2_ref[...], preferred_element_type=F32) + b_ref[...]
    o_ref[...] = (jnp.minimum(z, 0.0) - jnp.log(1.0 + jnp.exp(-jnp.abs(z)))) * (1.0 / GLA_GATE_NORM)


def gla_gate(x, g, wgz, wg2, b, tm):
    M, D = x.shape
    return pl.pallas_call(
        _gate_body,
        grid=(M // tm,),
        in_specs=[pl.BlockSpec((tm, D), lambda i: (i, 0)),
                  pl.BlockSpec((1, D), lambda i: (0, 0)),
                  pl.BlockSpec((D, LANES), lambda i: (0, 0)),
                  pl.BlockSpec((LANES, GLA_KD), lambda i: (0, 0)),
                  pl.BlockSpec((1, GLA_KD), lambda i: (0, 0))],
        out_specs=pl.BlockSpec((tm, GLA_KD), lambda i: (i, 0)),
        out_shape=jax.ShapeDtypeStruct((M, GLA_KD), F32),
        compiler_params=_params("parallel"),
        name="gla_gate",
    )(x, g, wgz, wg2, b)


def _gla_body(q_ref, k_ref, v_ref, r_ref, gl_ref, s0_ref, gn_ref, tri_ref, y_ref, st_ref, S_ref, *, TB, CH):
    c = pl.program_id(1)

    @pl.when(c == 0)
    def _():
        S_ref[...] = s0_ref[...]

    gl = gl_ref[...]
    g1 = gl.astype(BF16)
    r1 = gl - g1.astype(F32)
    g2 = r1.astype(BF16)
    g3 = (r1 - g2.astype(F32)).astype(BF16)
    tri = tri_ref[...]
    bfull = (jnp.dot(tri, g1, preferred_element_type=F32)
             + jnp.dot(tri, g2, preferred_element_type=F32)
             + jnp.dot(tri, g3, preferred_element_type=F32))

    row = lax.broadcasted_iota(jnp.int32, (CH, CH), 0)
    col = lax.broadcasted_iota(jnp.int32, (CH, CH), 1)
    causal = row >= col
    mid = CH // 2
    gn = gn_ref[...]

    states = [S_ref[h] for h in range(GLA_HEADS)]
    for sc in range(TB // CH):
        rs = slice(sc * CH, (sc + 1) * CH)
        for h in range(GLA_HEADS):
            ks = slice(h * GLA_DK, (h + 1) * GLA_DK)
            vs = slice(h * GLA_DV, (h + 1) * GLA_DV)
            S = states[h]
            b = bfull[rs, ks]
            if sc > 0:
                b = b - bfull[sc * CH - 1:sc * CH, ks]
            ref = b[mid:mid + 1]
            blast = b[CH - 1:CH]
            qf = q_ref[rs, ks].astype(F32) * (GLA_DK ** -0.5)
            kf = k_ref[rs, ks].astype(F32)
            vv = v_ref[rs, vs]
            qe = (qf * jnp.exp(b - ref)).astype(BF16)
            ke = (kf * jnp.exp(ref - b)).astype(BF16)
            a = lax.dot_general(qe, ke, _NT, preferred_element_type=F32)
            a = jnp.where(causal, a, 0.0)
            o = (jnp.dot(a.astype(BF16), vv, preferred_element_type=F32)
                 + lax.dot_general((qf * jnp.exp(b)).astype(BF16), S.astype(BF16), _NT,
                                   preferred_element_type=F32))
            kd = (kf * jnp.exp(blast - b)).astype(BF16)
            states[h] = jnp.exp(blast) * S + lax.dot_general(vv, kd, _TN, preferred_element_type=F32)
            y = _rms(o, gn) * _silu(r_ref[rs, vs].astype(F32))
            y_ref[rs, vs] = y.astype(y_ref.dtype)
    for h in range(GLA_HEADS):
        S_ref[h] = states[h]

    @pl.when(c == pl.num_programs(1) - 1)
    def _():
        st_ref[...] = S_ref[...]


def gla_scan(p, glog, s0t, gn, TB, CH):
    B, T, _ = p.shape
    tri = jnp.asarray(np.tril(np.ones((TB, TB), np.float32)), BF16)
    return pl.pallas_call(
        functools.partial(_gla_body, TB=TB, CH=CH),
        grid=(B, T // TB),
        in_specs=[pl.BlockSpec((None, TB, GLA_KD), lambda b, c: (b, c, 0)),
                  pl.BlockSpec((None, TB, GLA_KD), lambda b, c: (b, c, 1)),
                  pl.BlockSpec((None, TB, GLA_VD), lambda b, c: (b, c, 1)),
                  pl.BlockSpec((None, TB, GLA_VD), lambda b, c: (b, c, 2)),
                  pl.BlockSpec((None, TB, GLA_KD), lambda b, c: (b, c, 0)),
                  pl.BlockSpec((None, GLA_HEADS, GLA_DV, GLA_DK), lambda b, c: (b, 0, 0, 0)),
                  pl.BlockSpec((1, GLA_DV), lambda b, c: (0, 0)),
                  pl.BlockSpec((TB, TB), lambda b, c: (0, 0))],
        out_specs=[pl.BlockSpec((None, TB, GLA_VD), lambda b, c: (b, c, 0)),
                   pl.BlockSpec((None, GLA_HEADS, GLA_DV, GLA_DK), lambda b, c: (b, 0, 0, 0))],
        out_shape=[jax.ShapeDtypeStruct((B, T, GLA_VD), BF16),
                   jax.ShapeDtypeStruct((B, GLA_HEADS, GLA_DV, GLA_DK), F32)],
        scratch_shapes=[pltpu.VMEM((GLA_HEADS, GLA_DV, GLA_DK), F32)],
        compiler_params=_params("parallel", "arbitrary"),
        name="gla_scan",
    )(p, p, p, p, glog, s0t, gn, tri)


def _attn_body(q_ref, kp_ref, kc_ref, vp_ref, vc_ref, bias_ref, o_ref, lse_ref):
    var = jnp.where(pl.program_id(2) == 0, 0, 1)
    lane = lax.broadcasted_iota(jnp.int32, (DIL_BLOCK, LANES), 1)
    lo = lane < DIL_HD
    lse_acc = jnp.zeros((DIL_BLOCK, LANES), F32)
    zero = jnp.zeros((), BF16)
    def scores(hp):
        cs = slice(hp * LANES, (hp + 1) * LANES)
        q2 = q_ref[:, cs]
        qab = jnp.concatenate([jnp.where(lo, q2, zero), jnp.where(lo, zero, q2)], axis=0)
        k2 = jnp.concatenate([kp_ref[:, cs], kc_ref[:, cs]], axis=0)
        return lax.dot_general(qab, k2, _NT, preferred_element_type=F32)

    ahead = scores(0)
    for hp in range(DIL_HEADS // 2):
        cs = slice(hp * LANES, (hp + 1) * LANES)
        s = ahead + bias_ref[var, hp]
        if hp + 1 < DIL_HEADS // 2:
            ahead = scores(hp + 1)
        v2 = jnp.concatenate([vp_ref[:, cs], vc_ref[:, cs]], axis=0)
        m = jnp.max(s, axis=-1, keepdims=True)
        p = jnp.exp(s - m)
        l = jnp.sum(p, axis=-1, keepdims=True)
        o = jnp.dot(p.astype(BF16), v2, preferred_element_type=F32) * (1.0 / l)
        o_ref[:, cs] = jnp.where(lo, o[:DIL_BLOCK], o[DIL_BLOCK:])
        lse = m + jnp.log(l)
        lse_acc = jnp.where(lane == 2 * hp, lse[:DIL_BLOCK], lse_acc)
        lse_acc = jnp.where(lane == 2 * hp + 1, lse[DIL_BLOCK:], lse_acc)
    lse_ref[...] = lse_acc


def dil_attention(qkv, bias):
    B, d, L, _ = qkv.shape
    nb = L // DIL_BLOCK
    blk = (None, None, DIL_BLOCK, DIL_WIDTH)
    prev = lambda n: jnp.maximum(n - 1, 0)
    return pl.pallas_call(
        _attn_body,
        grid=(B, d, nb),
        in_specs=[pl.BlockSpec(blk, lambda b, r, n: (b, r, n, 0)),
                  pl.BlockSpec(blk, lambda b, r, n: (b, r, prev(n), 1)),
                  pl.BlockSpec(blk, lambda b, r, n: (b, r, n, 1)),
                  pl.BlockSpec(blk, lambda b, r, n: (b, r, prev(n), 2)),
                  pl.BlockSpec(blk, lambda b, r, n: (b, r, n, 2)),
                  pl.BlockSpec((2, DIL_HEADS // 2, 2 * DIL_BLOCK, 2 * DIL_BLOCK), lambda b, r, n: (0, 0, 0, 0))],
        out_specs=[pl.BlockSpec(blk, lambda b, r, n: (b, r, n, 0)),
                   pl.BlockSpec((None, None, DIL_BLOCK, LANES), lambda b, r, n: (b, r, n, 0))],
        out_shape=[jax.ShapeDtypeStruct((B, d, L, DIL_WIDTH), F32),
                   jax.ShapeDtypeStruct((B, d, L, LANES), F32)],
        compiler_params=_params("parallel", "parallel", "arbitrary"),
        name="dil_attention",
    )(qkv, qkv, qkv, qkv, qkv, bias)


def _dilout_body(o0_ref, o1_ref, o2_ref, l0_ref, l1_ref, l2_ref, x_ref, w_ref, e_ref, out_ref):
    l0, l1, l2 = l0_ref[...], l1_ref[...], l2_ref[...]
    mx = jnp.maximum(jnp.maximum(l0, l1), l2)
    e0, e1, e2 = jnp.exp(l0 - mx), jnp.exp(l1 - mx), jnp.exp(l2 - mx)
    den = e0 + e1 + e2
    ex = e_ref[...]

    def expand(w):
        w1 = w.astype(BF16)
        w2 = (w - w1.astype(F32)).astype(BF16)
        return (jnp.dot(w1, ex, preferred_element_type=F32) + jnp.dot(w2, ex, preferred_element_type=F32))

    o = (expand(e0 / den) * o0_ref[...] + expand(e1 / den) * o1_ref[...] + expand(e2 / den) * o2_ref[...])
    out_ref[...] = x_ref[...] + jnp.dot(o.astype(BF16), w_ref[...], preferred_element_type=F32)


def dil_out(os, ls, x, w, layer, expand, tm):
    M = x.shape[0]
    ospec = pl.BlockSpec((tm, DIL_WIDTH), lambda i: (i, 0))
    lspec = pl.BlockSpec((tm, LANES), lambda i: (i, 0))
    return pl.pallas_call(
        _dilout_body,
        grid=(M // tm,),
        in_specs=[ospec, ospec, ospec, lspec, lspec, lspec, ospec,
                  pl.BlockSpec((None, DIL_WIDTH, D_MODEL), lambda i: (layer, 0, 0)),
                  pl.BlockSpec((LANES, DIL_WIDTH), lambda i: (0, 0))],
        out_specs=pl.BlockSpec((tm, D_MODEL), lambda i: (i, 0)),
        out_shape=jax.ShapeDtypeStruct((M, D_MODEL), F32),
        compiler_params=_params("parallel"),
        name="dil_out",
    )(*os, *ls, x, w, expand)


def _cache_body(x_ref, g_ref, wk_ref, wv_ref, cg_ref, bd_ref, ok_ref, ov_ref, h_ref, *, tmc):
    @pl.when(pl.program_id(2) == 0)
    def _():
        h_ref[...] = _rms(x_ref[...], g_ref[...]).astype(BF16)

    h = h_ref[...]
    kt = lax.dot_general(wk_ref[...], h, _NT, preferred_element_type=F32)
    ss = jnp.dot(bd_ref[...], (kt * kt).astype(BF16), preferred_element_type=F32)
    gain = jnp.concatenate([cg_ref[...]] * (tmc // LANES), axis=1)
    ok_ref[...] = kt * lax.rsqrt(ss * (1.0 / DIL_HD) + EPS) * gain
    ov_ref[...] = lax.dot_general(wv_ref[...], h, _NT, preferred_element_type=F32)


def dil_cache_rows(x, g, wt, layer, group, kgain, bd, keep, tmc):
    B, T, D = x.shape
    nblk = DIL_WIDTH // MXU_WIDTH
    krow = (group * 3 + 1) * nblk
    vrow = (group * 3 + 2) * nblk
    t0 = (T - keep) // tmc
    out = jax.ShapeDtypeStruct((B, DIL_WIDTH, keep), F32)
    ospec = pl.BlockSpec((None, MXU_WIDTH, tmc), lambda b, i, j: (b, j, i))
    return pl.pallas_call(
        functools.partial(_cache_body, tmc=tmc),
        grid=(B, keep // tmc, nblk),
        in_specs=[pl.BlockSpec((None, tmc, D), lambda b, i, j: (b, t0 + i, 0)),
                  pl.BlockSpec((1, D), lambda b, i, j: (0, 0)),
                  pl.BlockSpec((None, MXU_WIDTH, D), lambda b, i, j: (layer, krow + j, 0)),
                  pl.BlockSpec((None, MXU_WIDTH, D), lambda b, i, j: (layer, vrow + j, 0)),
                  pl.BlockSpec((MXU_WIDTH, LANES), lambda b, i, j: (j, 0)),
                  pl.BlockSpec((MXU_WIDTH, MXU_WIDTH), lambda b, i, j: (0, 0))],
        out_specs=[ospec, ospec],
        out_shape=[out, out],
        scratch_shapes=[pltpu.VMEM((tmc, D), BF16)],
        compiler_params=_params("parallel", "parallel", "arbitrary"),
        name="dil_cache_rows",
    )(x, g, wt, wt, kgain, bd)


S_CHUNK = 512
S_NEWPAD = LANES


def _sattn_body(qkv_ref, k0_ref, v0_ref, k1_ref, v1_ref, k2_ref, v2_ref, b0_ref, b1_ref, b2_ref, bn_ref,
                o_ref, m_ref, l_ref, acc_ref, *, seq):
    c = pl.program_id(1)
    R = seq * DIL_HEADS
    rowh = lax.broadcasted_iota(jnp.int32, (R, DIL_WIDTH), 0) % DIL_HEADS
    colh = lax.broadcasted_iota(jnp.int32, (R, DIL_WIDTH), 1) // DIL_HD
    hmask = rowh == colh

    def seg(g, part):
        return qkv_ref[:, (3 * g + part) * DIL_WIDTH:(3 * g + part + 1) * DIL_WIDTH]

    def qbd(g):
        q = seg(g, 0)
        qrep = jnp.concatenate([jnp.broadcast_to(q[s:s + 1], (DIL_HEADS, DIL_WIDTH)) for s in range(seq)], axis=0)
        return jnp.where(hmask, qrep, 0.0).astype(BF16)

    def update(parts):
        m_old = m_ref[...]
        m_new = m_old
        for s, _ in parts:
            m_new = jnp.maximum(m_new, jnp.max(s, axis=-1, keepdims=True))
        alpha = jnp.exp(m_old - m_new)
        l = alpha * l_ref[...]
        acc = alpha * acc_ref[...]
        for s, pv_fn in parts:
            p = jnp.exp(s - m_new)
            l = l + jnp.sum(p, axis=-1, keepdims=True)
            acc = acc + pv_fn(p.astype(BF16))
        m_ref[...] = m_new
        l_ref[...] = l
        acc_ref[...] = acc

    def cache_segment(g, kt_ref, vt_ref, b_ref):
        s = jnp.dot(qbd(g), kt_ref[...].astype(BF16), preferred_element_type=F32) + b_ref[...]
        return s, lambda p: lax.dot_general(p, vt_ref[...].astype(BF16), _NT, preferred_element_type=F32)

    def new_segment(g):
        pad = jnp.zeros((S_NEWPAD - seq, DIL_WIDTH), F32)
        kn = jnp.concatenate([seg(g, 1), pad], axis=0).astype(BF16)
        vn = jnp.concatenate([seg(g, 2), pad], axis=0).astype(BF16)
        s = lax.dot_general(qbd(g), kn, _NT, preferred_element_type=F32) + bn_ref[g]
        return s, lambda p: jnp.dot(p, vn, preferred_element_type=F32)

    @pl.when(c == 0)
    def _():
        m_ref[...] = jnp.full(m_ref.shape, 2 * NEG, F32)
        l_ref[...] = jnp.zeros(l_ref.shape, F32)
        acc_ref[...] = jnp.zeros(acc_ref.shape, F32)
        update([new_segment(g) for g in range(N_GROUPS)]
               + [cache_segment(0, k0_ref, v0_ref, b0_ref), cache_segment(1, k1_ref, v1_ref, b1_ref),
                  cache_segment(2, k2_ref, v2_ref, b2_ref)])

    @pl.when(c > 0)
    def _():
        update([cache_segment(2, k2_ref, v2_ref, b2_ref)])

    @pl.when(c == pl.num_programs(1) - 1)
    def _():
        on = jnp.where(hmask, acc_ref[...] / l_ref[...], 0.0).astype(BF16)
        srow = lax.broadcasted_iota(jnp.int32, (8, R), 0)
        scol = lax.broadcasted_iota(jnp.int32, (8, R), 1) // DIL_HEADS
        sel = jnp.where(srow == scol, 1.0, 0.0).astype(BF16)
        o_ref[...] = jnp.dot(sel, on, preferred_element_type=F32)[:seq]


def dil_sample_attention(li, qkv, kts, vts, bbs, bn):
    DB, S, _ = qkv.shape
    R = DIL_HEADS * S
    W2 = kts[2].shape[-1]
    in_specs = [pl.BlockSpec((None, S, qkv.shape[-1]), lambda b, c: (b, 0, 0))]
    args = [qkv]
    for g in range(2):
        for a in (kts[g], vts[g]):
            in_specs.append(pl.BlockSpec((None, None, DIL_WIDTH, a.shape[-1]), lambda b, c: (li, b, 0, 0)))
            args.append(a)
    for a in (kts[2], vts[2]):
        in_specs.append(pl.BlockSpec((None, None, DIL_WIDTH, S_CHUNK), lambda b, c: (li, b, 0, c)))
        args.append(a)
    for g in range(2):
        in_specs.append(pl.BlockSpec(bbs[g].shape, lambda b, c: (0, 0)))
        args.append(bbs[g])
    in_specs.append(pl.BlockSpec((R, S_CHUNK), lambda b, c: (0, c)))
    args.append(bbs[2])
    in_specs.append(pl.BlockSpec(bn.shape, lambda b, c: (0, 0, 0)))
    args.append(bn)
    return pl.pallas_call(
        functools.partial(_sattn_body, seq=S),
        grid=(DB, W2 // S_CHUNK),
        in_specs=in_specs,
        out_specs=pl.BlockSpec((None, S, DIL_WIDTH), lambda b, c: (b, 0, 0)),
        out_shape=jax.ShapeDtypeStruct((DB, S, DIL_WIDTH), F32),
        scratch_shapes=[pltpu.VMEM((R, 1), F32), pltpu.VMEM((R, 1), F32), pltpu.VMEM((R, DIL_WIDTH), F32)],
        compiler_params=_params("parallel", "arbitrary"),
        name="dil_sample_attention",
    )(*args)


FFN_CHUNK = MXU_WIDTH
HALO = 16


FFN_ROWBLOCK = 128


def _causal_conv(ext, cw, cb):
    u1 = pltpu.roll(ext, 1, axis=0)[HALO:]
    u2 = pltpu.roll(ext, 2, axis=0)[HALO:]
    return cb + (cw[0:1] * u2 + cw[1:2] * u1 + cw[2:3] * ext[HALO:])


def _ffnup_body(x_ref, xh_ref, g_ref, wg_ref, wv_ref, cw_ref, cb_ref, act_ref, tail_ref, h_ref, *, tm):
    first = pl.program_id(1) == 0
    hh = _rms(xh_ref[...], g_ref[...])
    h_ref[0:HALO] = jnp.where(first, 0.0, hh).astype(BF16)
    h_ref[HALO:HALO + tm] = _rms(x_ref[...], g_ref[...]).astype(BF16)
    for c in range(D_FF // FFN_CHUNK):
        cols = (slice(c * FFN_CHUNK, (c + 1) * FFN_CHUNK),
                slice(D_FF + c * FFN_CHUNK, D_FF + (c + 1) * FFN_CHUNK))
        tails = [None, None]
        for a in range(0, tm, FFN_ROWBLOCK):
            conv = []
            for idx, w_ref in enumerate((wg_ref, wv_ref)):
                if a == 0:
                    ext = jnp.dot(h_ref[0:HALO + FFN_ROWBLOCK], w_ref[:, cols[0]], preferred_element_type=F32)
                else:
                    u = jnp.dot(h_ref[HALO + a:HALO + a + FFN_ROWBLOCK], w_ref[:, cols[0]],
                                preferred_element_type=F32)
                    ext = jnp.concatenate([tails[idx], u], axis=0)
                tails[idx] = ext[FFN_ROWBLOCK:]
                conv.append(_causal_conv(ext, cw_ref[:, cols[idx]], cb_ref[:, cols[idx]]))
            act_ref[a:a + FFN_ROWBLOCK, cols[0]] = (_silu(conv[0]) * conv[1]).astype(act_ref.dtype)
        for idx in range(2):
            tail_ref[:, cols[idx]] = tails[idx][HALO - 8:]


def ffn_up_prompt(x, g, w, layer, cw, cb, tm):
    B, T, D = x.shape
    hblk = tm // HALO
    return pl.pallas_call(
        functools.partial(_ffnup_body, tm=tm),
        grid=(B, T // tm),
        in_specs=[pl.BlockSpec((None, tm, D), lambda b, i: (b, i, 0)),
                  pl.BlockSpec((None, HALO, D), lambda b, i: (b, jnp.maximum(i * hblk - 1, 0), 0)),
                  pl.BlockSpec((1, D), lambda b, i: (0, 0)),
                  pl.BlockSpec((None, D, D_FF), lambda b, i: (layer, 0, 0)),
                  pl.BlockSpec((None, D, D_FF), lambda b, i: (layer, 0, 1)),
                  pl.BlockSpec((CONV_WIDTH, 2 * D_FF), lambda b, i: (0, 0)),
                  pl.BlockSpec((1, 2 * D_FF), lambda b, i: (0, 0))],
        out_specs=[pl.BlockSpec((None, tm, D_FF), lambda b, i: (b, i, 0)),
                   pl.BlockSpec((None, 8, 2 * D_FF), lambda b, i: (b, 0, 0))],
        out_shape=[jax.ShapeDtypeStruct((B, T, D_FF), BF16),
                   jax.ShapeDtypeStruct((B, 8, 2 * D_FF), F32)],
        scratch_shapes=[pltpu.VMEM((HALO + tm, D), BF16)],
        compiler_params=_params("parallel", "arbitrary"),
        name="ffn_up_prompt",
    )(x, x, g, w, w, cw, cb)


def _ffnup_s_body(x_ref, g_ref, wg_ref, wv_ref, cw_ref, cb_ref, c1_ref, c2_ref, act_ref, u_ref, u_scr, *, rows, seq):
    h = _rms(x_ref[...], g_ref[...]).astype(BF16)
    pos = lax.broadcasted_iota(jnp.int32, (rows, FFN_CHUNK), 0) % seq
    u_scr[:, 0:HALO] = jnp.zeros((2, HALO, FFN_CHUNK), F32)
    for c in range(D_FF // FFN_CHUNK):
        acts = []
        for idx, off in enumerate((0, D_FF)):
            cs = slice(off + c * FFN_CHUNK, off + (c + 1) * FFN_CHUNK)
            w_ref = wg_ref if idx == 0 else wv_ref
            u = jnp.dot(h, w_ref[:, c * FFN_CHUNK:(c + 1) * FFN_CHUNK], preferred_element_type=F32)
            u_scr[idx, HALO:HALO + rows] = u
            u1 = jnp.where(pos >= 1, u_scr[idx, HALO - 1:HALO - 1 + rows], c1_ref[:, cs])
            u2 = jnp.where(pos >= 2, u_scr[idx, HALO - 2:HALO - 2 + rows], c2_ref[:, cs])
            acts.append(cb_ref[:, cs] + (cw_ref[0:1, cs] * u2 + cw_ref[1:2, cs] * u1 + cw_ref[2:3, cs] * u))
            u_ref[:, cs] = u
        act_ref[:, c * FFN_CHUNK:(c + 1) * FFN_CHUNK] = (_silu(acts[0]) * acts[1]).astype(act_ref.dtype)


def ffn_up_sample(x, g, w, layer, cw, cb, carry1, carry2, seq):
    M, D = x.shape
    full = lambda shape: pl.BlockSpec(shape, lambda i: tuple(0 for _ in shape))
    return pl.pallas_call(
        functools.partial(_ffnup_s_body, rows=M, seq=seq),
        grid=(1,),
        in_specs=[full((M, D)), full((1, D)),
                  pl.BlockSpec((None, D, D_FF), lambda i: (layer, 0, 0)),
                  pl.BlockSpec((None, D, D_FF), lambda i: (layer, 0, 1)),
                  full((CONV_WIDTH, 2 * D_FF)), full((1, 2 * D_FF)),
                  full((M, 2 * D_FF)), full((M, 2 * D_FF))],
        out_specs=[full((M, D_FF)), full((M, 2 * D_FF))],
        out_shape=[jax.ShapeDtypeStruct((M, D_FF), BF16),
                   jax.ShapeDtypeStruct((M, 2 * D_FF), F32)],
        scratch_shapes=[pltpu.VMEM((2, HALO + M, FFN_CHUNK), F32)],
        compiler_params=_params("arbitrary"),
        name="ffn_up_sample",
    )(x, g, w, w, cw, cb, carry1, carry2)


def _rel_bucket(dist):
    max_exact = NUM_BUCKETS // 2
    df = jnp.maximum(dist, 1).astype(F32)
    large = max_exact + (jnp.log(df / max_exact) / math.log(MAX_DISTANCE / max_exact)
                         * (NUM_BUCKETS - max_exact)).astype(jnp.int32)
    large = jnp.minimum(large, NUM_BUCKETS - 1)
    return jnp.where(dist < max_exact, dist, large)


def _step_table(rel_bias, g, d):
    J = DIL_GROUPS[g][0] // d
    tab = rel_bias[:, g * DIL_HEADS:(g + 1) * DIL_HEADS].astype(F32)
    return tab[_rel_bucket(jnp.arange(J + 1) * d)]


def _lookup(table, idx):
    onehot = jax.nn.one_hot(jnp.asarray(idx, jnp.int32), table.shape[0], dtype=F32)
    return jnp.einsum('...j,jh->...h', onehot, table, precision=lax.Precision.HIGHEST)


def _prompt_bias(rel_bias, g, d):
    J = DIL_GROUPS[g][0] // d
    qi = np.arange(DIL_BLOCK)[:, None]
    ki = np.arange(2 * DIL_BLOCK)[None, :]
    rel = qi + DIL_BLOCK - ki
    bias = jnp.moveaxis(_lookup(_step_table(rel_bias, g, d), np.clip(rel, 0, J)), -1, 0)
    out = []
    for first in (True, False):
        valid = (rel >= 0) & (rel <= J) & ((ki >= DIL_BLOCK) | (not first))
        b = jnp.where(jnp.asarray(valid)[None], bias, NEG)
        out.append(b.reshape(DIL_HEADS // 2, 2 * DIL_BLOCK, 2 * DIL_BLOCK))
    return jnp.stack(out)


def _sample_bias(rel_bias, g, d, seq):
    W = DIL_GROUPS[g][0]
    J = W // d
    table = _step_table(rel_bias, g, d)
    s = np.arange(seq)[:, None]
    dist = W + s - np.arange(W)[None, :]
    valid = (dist >= 0) & (dist <= W) & (dist % d == 0)
    bval = jnp.moveaxis(_lookup(table, np.where(valid, dist // d, 0)), -1, 1)
    bbuf = jnp.where(jnp.asarray(valid)[:, None, :], bval, NEG).reshape(seq * DIL_HEADS, W)
    dn = s - np.arange(S_NEWPAD)[None, :]
    vn = (dn >= 0) & (dn % d == 0) & (dn // d <= J) & (np.arange(S_NEWPAD)[None, :] < seq)
    bnew = jnp.moveaxis(_lookup(table, np.where(vn, dn // d, 0)), -1, 1)
    bnew = jnp.where(jnp.asarray(vn)[:, None, :], bnew, NEG).reshape(seq * DIL_HEADS, S_NEWPAD)
    return bbuf, bnew


def _colgain(q_gain, k_gain, ngroups):
    seg = jnp.concatenate([jnp.tile(q_gain.astype(F32) * DIL_SCALE, DIL_HEADS),
                           jnp.tile(k_gain.astype(F32), DIL_HEADS),
                           jnp.ones((DIL_WIDTH,), F32)])
    return jnp.tile(seg, ngroups)[None, :]


def kernel(x_prompt, x_sample, state_gla, cache_k_g0, cache_v_g0, cache_k_g1, cache_v_g1, cache_k_g2, cache_v_g2,
           state_ffn_conv, rel_bias, norm_mix, norm_ffn, gla_w_in, gla_w_gate2, gla_b_gate, gla_norm, gla_w_out,
           dil_w_in, dil_q_norm, dil_k_norm, dil_w_out, ffn_w_up, ffn_conv_w, ffn_conv_b, ffn_w_down):
    B, T, D = x_prompt.shape
    DB, S, _ = x_sample.shape
    MP, MS = B * T, DB * S
    assert all(c.shape[2] == w for c, (w, _) in zip((cache_k_g0, cache_k_g1, cache_k_g2), DIL_GROUPS))

    xp = x_prompt.reshape(MP, D)
    xs = x_sample.reshape(MS, D)
    fmajor = lambda c: c.transpose(0, 1, 3, 4, 2).reshape(c.shape[0], c.shape[1], DIL_WIDTH, c.shape[2])
    k_caches = tuple(fmajor(c) for c in (cache_k_g0, cache_k_g1, cache_k_g2))
    v_caches = tuple(fmajor(c) for c in (cache_v_g0, cache_v_g1, cache_v_g2))

    blockdiag = jnp.asarray(np.kron(np.eye(MXU_WIDTH // DIL_HD), np.ones((DIL_HD, DIL_HD))), BF16)
    expand = jnp.asarray(np.kron(np.eye(LANES, DIL_HEADS, dtype=np.float32).reshape(LANES, DIL_HEADS),
                                 np.ones((1, DIL_HD), np.float32)), BF16)
    p_bias = [_prompt_bias(rel_bias, g, d) for g, (_, d) in enumerate(DIL_GROUPS)]
    s_bias = [_sample_bias(rel_bias, g, d, S) for g, (_, d) in enumerate(DIL_GROUPS)]
    s_bias_buf = [b for b, _ in s_bias]
    s_bias_new = jnp.stack([b for _, b in s_bias])

    gla_p, gla_s = [], []
    kp = [[] for _ in DIL_GROUPS]
    vp = [[] for _ in DIL_GROUPS]
    kq = [[] for _ in DIL_GROUPS]
    vq = [[] for _ in DIL_GROUPS]
    conv_p, conv_s = [], []
    SPAD = 16

    gla_w_in_b = to_bf16(gla_w_in, 256)
    gla_w_out_b = to_bf16(gla_w_out, 256)
    dil_w_in_b = to_bf16(dil_w_in, 128)
    dil_w_out_b = to_bf16(dil_w_out, 256)
    ffn_w_up_b = to_bf16(ffn_w_up, 256)
    ffn_w_down_b = to_bf16(ffn_w_down, 256)
    dil_w_in_t = jnp.swapaxes(dil_w_in_b, 1, 2)
    n_main = 2 * GLA_KD + 2 * GLA_VD

    for i in range(DEPTH):
        li = i // N_MIXERS
        gmix = norm_mix[i][None, :].astype(F32)
        if i % N_MIXERS == 0:
            wgz = jnp.pad(gla_w_in[li][:, n_main:], ((0, 0), (0, LANES - GLA_GATE_RANK))).astype(BF16)
            wg2 = jnp.pad(gla_w_gate2[li], ((0, LANES - GLA_GATE_RANK), (0, 0))).astype(BF16)
            bg = gla_b_gate[li][None, :].astype(F32)
            gn = gla_norm[li][None, :].astype(F32)
            pp = norm_matmul(xp, gmix, gla_w_in_b, li, n_main, BF16, 1024, 1024).reshape(B, T, -1)
            glp = gla_gate(xp, gmix, wgz, wg2, bg, 1024).reshape(B, T, GLA_KD)
            s0 = jnp.zeros((B, GLA_HEADS, GLA_DV, GLA_DK), F32)
            yp, stp = gla_scan(pp, glp, s0, gn, 128, GLA_CHUNK)
            xp = matmul_residual(yp.reshape(MP, GLA_VD), gla_w_out_b, li, xp, 1024, D)
            gla_p.append(jnp.swapaxes(stp, -1, -2))
            ps = norm_matmul(xs, gmix, gla_w_in_b, li, n_main, BF16, MS, 1024).reshape(DB, S, -1)
            gls = gla_gate(xs, gmix, wgz, wg2, bg, MS).reshape(DB, S, GLA_KD)
            ps = jnp.pad(ps, ((0, 0), (0, SPAD - S), (0, 0)))
            gls = jnp.pad(gls, ((0, 0), (0, SPAD - S), (0, 0)))
            ys, sts = gla_scan(ps, gls, jnp.swapaxes(state_gla[li].astype(F32), -1, -2), gn, SPAD, SPAD)
            xs = matmul_residual(ys[:, :S].reshape(MS, GLA_VD), gla_w_out_b, li, xs, MS, D)
            gla_s.append(jnp.swapaxes(sts, -1, -2))
        else:
            cg = _colgain(dil_q_norm[li], dil_k_norm[li], N_GROUPS)
            kgain = jnp.broadcast_to(jnp.tile(dil_k_norm[li].astype(F32), DIL_HEADS)[:, None], (DIL_WIDTH, LANES))
            os_, ls_ = [], []
            for g, (W, d) in enumerate(DIL_GROUPS):
                L = T // d
                qkv = norm_matmul_qk(xp.reshape(B, T, D), gmix, dil_w_in_b, li, cg, blockdiag, BF16,
                                     1024, 1024, d=d, group=g)
                o, lse = dil_attention(qkv, p_bias[g])
                if d > 1:
                    o = o.transpose(0, 2, 1, 3)
                    lse = lse.transpose(0, 2, 1, 3)
                os_.append(o.reshape(MP, DIL_WIDTH))
                ls_.append(lse.reshape(MP, LANES))
                keep = min(W, T)
                kt, vt = dil_cache_rows(xp.reshape(B, T, D), gmix, dil_w_in_t, li, g, kgain, blockdiag,
                                        keep, min(keep, 512))
                kp[g].append(kt)
                vp[g].append(vt)
            xp = dil_out(os_, ls_, xp, dil_w_out_b, li, expand, 512)
            qkvs = norm_matmul_qk(xs.reshape(1, MS, D), gmix, dil_w_in_b, li, cg, blockdiag, F32, MS, 1024)
            osamp = dil_sample_attention(li, qkvs.reshape(DB, S, -1), k_caches, v_caches, s_bias_buf, s_bias_new)
            qkvs = qkvs.reshape(DB, S, N_GROUPS, 3, DIL_HEADS, DIL_HD)
            for g in range(N_GROUPS):
                kq[g].append(qkvs[:, :, g, 1])
                vq[g].append(qkvs[:, :, g, 2])
            osamp = osamp.reshape(MS, DIL_WIDTH).astype(BF16)
            xs = matmul_residual(osamp, dil_w_out_b, li, xs, MS, D)

        gffn = norm_ffn[i][None, :].astype(F32)
        cw = ffn_conv_w[i].astype(F32)
        cb = ffn_conv_b[i][None, :].astype(F32)
        act, tail = ffn_up_prompt(xp.reshape(B, T, D), gffn, ffn_w_up_b, i, cw, cb, 512)
        conv_p.append(tail[:, 8 - (CONV_WIDTH - 1):])
        xp = matmul_residual(act.reshape(MP, D_FF), ffn_w_down_b, i, xp, 1024, D)
        buf = state_ffn_conv[i].astype(F32)
        zeros = jnp.zeros((DB, S - 1, 2 * D_FF), F32)
        carry1 = jnp.concatenate([buf[:, 1:2], zeros], axis=1).reshape(MS, 2 * D_FF)
        carry2 = jnp.concatenate([buf, zeros[:, 1:]], axis=1).reshape(MS, 2 * D_FF)
        acts, us = ffn_up_sample(xs, gffn, ffn_w_up_b, i, cw, cb, carry1, carry2, S)
        conv_s.append(us.reshape(DB, S, 2 * D_FF)[:, S - (CONV_WIDTH - 1):])
        xs = matmul_residual(acts, ffn_w_down_b, i, xs, MS, D)

    outs = [xp.reshape(B, T, D), xs.reshape(DB, S, D), jnp.stack(gla_p), jnp.stack(gla_s)]
    pmajor = lambda c: c.reshape(c.shape[0], B, DIL_HEADS, DIL_HD, c.shape[-1]).transpose(0, 1, 4, 2, 3)
    for g in range(N_GROUPS):
        outs += [pmajor(jnp.stack(kp[g])), jnp.stack(kq[g]), pmajor(jnp.stack(vp[g])), jnp.stack(vq[g])]
    outs += [jnp.stack(conv_p), jnp.stack(conv_s)]
    return tuple(outs)
```

```python
import functools
import math

import numpy as np
import jax
import jax.numpy as jnp
from jax import lax
from jax.experimental import pallas as pl
from jax.experimental.pallas import tpu as pltpu

F32 = jnp.float32
BF16 = jnp.bfloat16

D_MODEL = 1024
DEPTH = 4
N_MIXERS = 2
GLA_HEADS = 4
GLA_KD = 512
GLA_VD = 1024
GLA_DK = 128
GLA_DV = 256
GLA_GATE_RANK = 16
GLA_GATE_NORM = 16.0
GLA_CHUNK = 32
DIL_GROUPS = ((128, 1), (512, 4), (2048, 16))
N_GROUPS = 3
DIL_HEADS = 16
DIL_HD = 64
DIL_WIDTH = 1024
DIL_BLOCK = 128
DIL_SCALE = DIL_HD ** -0.5
NUM_BUCKETS = 32
MAX_DISTANCE = 2048
D_FF = 2816
CONV_WIDTH = 3
EPS = 1e-6
NEG = -1e30

LANES = 128
MXU_WIDTH = 256
VMEM_LIMIT = 48 * 1024 * 1024

_NT = (((1,), (1,)), ((), ()))
_TN = (((0,), (0,)), ((), ()))


def _params(*sem):
    return pltpu.CompilerParams(dimension_semantics=sem, vmem_limit_bytes=VMEM_LIMIT)


def _rms(x, g):
    return x * lax.rsqrt(jnp.mean(x * x, axis=-1, keepdims=True) + EPS) * g


def _silu(x):
    return x * (1.0 / (1.0 + jnp.exp(-x)))


def _cast_body(w_ref, o_ref):
    o_ref[...] = w_ref[...].astype(o_ref.dtype)


def to_bf16(w, rows):
    L, R, C = w.shape
    return pl.pallas_call(
        _cast_body,
        grid=(L, R // rows),
        in_specs=[pl.BlockSpec((None, rows, C), lambda l, i: (l, i, 0))],
        out_specs=pl.BlockSpec((None, rows, C), lambda l, i: (l, i, 0)),
        out_shape=jax.ShapeDtypeStruct((L, R, C), BF16),
        compiler_params=_params("parallel", "parallel"),
        name="to_bf16",
    )(w)


def _nm_body(x_ref, g_ref, w_ref, o_ref, h_ref):
    @pl.when(pl.program_id(1) == 0)
    def _():
        h_ref[...] = _rms(x_ref[...], g_ref[...]).astype(BF16)

    o_ref[...] = jnp.dot(h_ref[...], w_ref[...], preferred_element_type=F32).astype(o_ref.dtype)


def norm_matmul(x, g, w, layer, ncols, out_dtype, tm, tn):
    M, D = x.shape
    return pl.pallas_call(
        _nm_body,
        grid=(M // tm, ncols // tn),
        in_specs=[pl.BlockSpec((tm, D), lambda i, j: (i, 0)),
                  pl.BlockSpec((1, D), lambda i, j: (0, 0)),
                  pl.BlockSpec((None, D, tn), lambda i, j: (layer, 0, j))],
        out_specs=pl.BlockSpec((tm, tn), lambda i, j: (i, j)),
        out_shape=jax.ShapeDtypeStruct((M, ncols), out_dtype),
        scratch_shapes=[pltpu.VMEM((tm, D), BF16)],
        compiler_params=_params("parallel", "arbitrary"),
        name="norm_matmul",
    )(x, g, w)


PERM_ROWS = 16
QK_ROWBLOCK = 256


def _nmqk_body(x_ref, g_ref, w_ref, cg_ref, bd_ref, perm_ref, o_ref, h_ref, *, tm, tn, d):
    j = pl.program_id(2)
    rows = tm // d

    @pl.when(j == 0)
    def _():
        h = _rms(x_ref[...], g_ref[...]).astype(BF16)
        if d == 1:
            h_ref[...] = h
        else:
            sub = PERM_ROWS * d
            for s in range(tm // sub):
                hs = jnp.dot(perm_ref[...], h[s * sub:(s + 1) * sub], preferred_element_type=F32).astype(BF16)
                for r in range(d):
                    dst = r * rows + PERM_ROWS * s
                    h_ref[dst:dst + PERM_ROWS, :] = hs[r * PERM_ROWS:(r + 1) * PERM_ROWS]

    is_norm = ((j * tn) // DIL_WIDTH) % 3 != 2

    rb = min(QK_ROWBLOCK, tm)

    def emit(a, cs, y):
        yb = y.astype(o_ref.dtype)
        for r in range(d):
            lo, hi = max(a, r * rows), min(a + rb, (r + 1) * rows)
            if lo < hi:
                o_ref[r, lo - r * rows:hi - r * rows, cs] = yb[lo - a:hi - a]

    def chunks(norm):
        blocks = [(slice(c * MXU_WIDTH, (c + 1) * MXU_WIDTH), a)
                  for c in range(tn // MXU_WIDTH) for a in range(0, tm, rb)]
        proj = lambda cs, a: jnp.dot(h_ref[a:a + rb, :], w_ref[:, cs], preferred_element_type=F32)
        ahead = proj(*blocks[0])
        for n, (cs, a) in enumerate(blocks):
            pc = ahead
            if n + 1 < len(blocks):
                ahead = proj(*blocks[n + 1])
            if norm:
                ss = jnp.dot((pc * pc).astype(BF16), bd_ref[...], preferred_element_type=F32)
                pc = pc * lax.rsqrt(ss * (1.0 / DIL_HD) + EPS) * cg_ref[:, cs]
            emit(a, cs, pc)

    @pl.when(is_norm)
    def _():
        chunks(True)

    @pl.when(jnp.logical_not(is_norm))
    def _():
        chunks(False)


def norm_matmul_qk(x, g, w, layer, colgain, bd, out_dtype, tm, tn, d=1, group=None):
    B, T, D = x.shape
    N = w.shape[2] if group is None else 3 * DIL_WIDTH
    j0 = 0 if group is None else group * N // tn
    sub = PERM_ROWS * d
    return pl.pallas_call(
        functools.partial(_nmqk_body, tm=tm, tn=tn, d=d),
        grid=(B, T // tm, N // tn),
        in_specs=[pl.BlockSpec((None, tm, D), lambda b, i, j: (b, i, 0)),
                  pl.BlockSpec((1, D), lambda b, i, j: (0, 0)),
                  pl.BlockSpec((None, D, tn), lambda b, i, j: (layer, 0, j + j0)),
                  pl.BlockSpec((1, tn), lambda b, i, j: (0, j + j0)),
                  pl.BlockSpec((MXU_WIDTH, MXU_WIDTH), lambda b, i, j: (0, 0)),
                  pl.BlockSpec((sub, sub), lambda b, i, j: (0, 0))],
        out_specs=pl.BlockSpec((None, d, tm // d, tn), lambda b, i, j: (b, 0, i, j)),
        out_shape=jax.ShapeDtypeStruct((B, d, T // d, N), out_dtype),
        scratch_shapes=[pltpu.VMEM((tm, D), BF16)],
        compiler_params=_params("parallel", "parallel", "arbitrary"),
        name="norm_matmul_qk",
    )(x, g, w, colgain, bd, _perm_matrix(d))


def _perm_matrix(d):
    sub = PERM_ROWS * d
    pm = np.zeros((sub, sub), np.float32)
    for r in range(d):
        for i in range(PERM_ROWS):
            pm[r * PERM_ROWS + i, i * d + r] = 1.0
    return jnp.asarray(pm, BF16)


def _prep_body(x_ref, g_ref, perm_ref, o_ref, *, tm, d):
    h = _rms(x_ref[...], g_ref[...]).astype(BF16)
    if d == 1:
        o_ref[0] = h
    else:
        sub = PERM_ROWS * d
        for s in range(tm // sub):
            hs = jnp.dot(perm_ref[...], h[s * sub:(s + 1) * sub], preferred_element_type=F32).astype(BF16)
            for r in range(d):
                o_ref[r, PERM_ROWS * s:PERM_ROWS * (s + 1), :] = hs[r * PERM_ROWS:(r + 1) * PERM_ROWS]


def rms_prep(x, g, d, tm):
    B, T, D = x.shape
    sub = PERM_ROWS * d
    return pl.pallas_call(
        functools.partial(_prep_body, tm=tm, d=d),
        grid=(B, T // tm),
        in_specs=[pl.BlockSpec((None, tm, D), lambda b, i: (b, i, 0)),
                  pl.BlockSpec((1, D), lambda b, i: (0, 0)),
                  pl.BlockSpec((sub, sub), lambda b, i: (0, 0))],
        out_specs=pl.BlockSpec((None, d, tm // d, D), lambda b, i: (b, 0, i, 0)),
        out_shape=jax.ShapeDtypeStruct((B, d, T // d, D), BF16),
        compiler_params=_params("parallel", "parallel"),
        name="rms_prep",
    )(x, g, _perm_matrix(d))


def _projrows_body(h_ref, w_ref, cg_ref, bd_ref, o_ref, *, tm, norm):
    rb = min(QK_ROWBLOCK, tm)
    blocks = [(slice(c * MXU_WIDTH, (c + 1) * MXU_WIDTH), a)
              for c in range(DIL_WIDTH // MXU_WIDTH) for a in range(0, tm, rb)]
    proj = lambda cs, a: jnp.dot(h_ref[a:a + rb, :], w_ref[:, cs], preferred_element_type=F32)
    ahead = proj(*blocks[0])
    for n, (cs, a) in enumerate(blocks):
        pc = ahead
        if n + 1 < len(blocks):
            ahead = proj(*blocks[n + 1])
        if norm:
            ss = jnp.dot((pc * pc).astype(BF16), bd_ref[...], preferred_element_type=F32)
            pc = pc * lax.rsqrt(ss * (1.0 / DIL_HD) + EPS) * cg_ref[:, cs]
        o_ref[a:a + rb, cs] = pc.astype(o_ref.dtype)


def proj_rows(h, w, layer, seg, colgain, bd, norm, tm):
    M, D = h.shape
    return pl.pallas_call(
        functools.partial(_projrows_body, tm=tm, norm=norm),
        grid=(M // tm,),
        in_specs=[pl.BlockSpec((tm, D), lambda i: (i, 0)),
                  pl.BlockSpec((None, D, DIL_WIDTH), lambda i: (layer, 0, seg)),
                  pl.BlockSpec((1, DIL_WIDTH), lambda i: (0, seg)),
                  pl.BlockSpec((MXU_WIDTH, MXU_WIDTH), lambda i: (0, 0))],
        out_specs=pl.BlockSpec((tm, DIL_WIDTH), lambda i: (i, 0)),
        out_shape=jax.ShapeDtypeStruct((M, DIL_WIDTH), BF16),
        compiler_params=_params("parallel"),
        name="proj_rows",
    )(h, w, colgain, bd)


def _projcols_body(h_ref, wt_ref, cg_ref, bd_ref, o_ref, *, tl, norm):
    kt = lax.dot_general(wt_ref[...], h_ref[...], _NT, preferred_element_type=F32)
    for fb in range(DIL_WIDTH // MXU_WIDTH):
        fs = slice(fb * MXU_WIDTH, (fb + 1) * MXU_WIDTH)
        blk = kt[fs]
        if norm:
            ss = jnp.dot(bd_ref[...], (blk * blk).astype(BF16), preferred_element_type=F32)
            gain = jnp.concatenate([cg_ref[fs, :]] * (tl // LANES), axis=1)
            blk = blk * lax.rsqrt(ss * (1.0 / DIL_HD) + EPS) * gain
        o_ref[fs, :] = blk.astype(o_ref.dtype)


def proj_cols(h, wt, layer, seg, rowgain, bd, norm, out_dtype, tl, first=0, count=None):
    G, L, D = h.shape
    count = L // tl if count is None else count
    return pl.pallas_call(
        functools.partial(_projcols_body, tl=tl, norm=norm),
        grid=(G, count),
        in_specs=[pl.BlockSpec((None, tl, D), lambda g, i: (g, first + i, 0)),
                  pl.BlockSpec((None, DIL_WIDTH, D), lambda g, i: (layer, seg, 0)),
                  pl.BlockSpec((DIL_WIDTH, LANES), lambda g, i: (0, 0)),
                  pl.BlockSpec((MXU_WIDTH, MXU_WIDTH), lambda g, i: (0, 0))],
        out_specs=pl.BlockSpec((None, DIL_WIDTH, tl), lambda g, i: (g, 0, i)),
        out_shape=jax.ShapeDtypeStruct((G, DIL_WIDTH, count * tl), out_dtype),
        compiler_params=_params("parallel", "parallel"),
        name="proj_cols",
    )(h, wt, rowgain, bd)


def _mmres_body(y_ref, w_ref, x_ref, o_ref):
    o_ref[...] = x_ref[...] + jnp.dot(y_ref[...], w_ref[...], preferred_element_type=F32)


def matmul_residual(y, w, layer, x, tm, tn):
    M, K = y.shape
    N = w.shape[2]
    return pl.pallas_call(
        _mmres_body,
        grid=(M // tm, N // tn),
        in_specs=[pl.BlockSpec((tm, K), lambda i, j: (i, 0)),
                  pl.BlockSpec((None, K, tn), lambda i, j: (layer, 0, j)),
                  pl.BlockSpec((tm, tn), lambda i, j: (i, j))],
        out_specs=pl.BlockSpec((tm, tn), lambda i, j: (i, j)),
        out_shape=jax.ShapeDtypeStruct((M, N), F32),
        compiler_params=_params("parallel", "parallel"),
        name="matmul_residual",
    )(y, w, x)


def _gate_body(x_ref, g_ref, wgz_ref, wg2_ref, b_ref, o_ref):
    h = _rms(x_ref[...], g_ref[...]).astype(BF16)
    gz = jnp.dot(h, wgz_ref[...], preferred_element_type=F32)
    z = jnp.dot(gz.astype(BF16), wg2_ref[...], preferred_element_type=F32) + b_ref[...]
    o_ref[...] = (jnp.minimum(z, 0.0) - jnp.log(1.0 + jnp.exp(-jnp.abs(z)))) * (1.0 / GLA_GATE_NORM)


def gla_gate(x, g, wgz, wg2, b, tm):
    M, D = x.shape
    return pl.pallas_call(
        _gate_body,
        grid=(M // tm,),
        in_specs=[pl.BlockSpec((tm, D), lambda i: (i, 0)),
                  pl.BlockSpec((1, D), lambda i: (0, 0)),
                  pl.BlockSpec((D, LANES), lambda i: (0, 0)),
                  pl.BlockSpec((LANES, GLA_KD), lambda i: (0, 0)),
                  pl.BlockSpec((1, GLA_KD), lambda i: (0, 0))],
        out_specs=pl.BlockSpec((tm, GLA_KD), lambda i: (i, 0)),
        out_shape=jax.ShapeDtypeStruct((M, GLA_KD), F32),
        compiler_params=_params("parallel"),
        name="gla_gate",
    )(x, g, wgz, wg2, b)


def _gla_body(q_ref, k_ref, v_ref, r_ref, gl_ref, s0_ref, gn_ref, tri_ref, y_ref, st_ref, S_ref, *, TB, CH):
    c = pl.program_id(1)

    @pl.when(c == 0)
    def _():
        S_ref[...] = s0_ref[...]

    gl = gl_ref[...]
    g1 = gl.astype(BF16)
    r1 = gl - g1.astype(F32)
    g2 = r1.astype(BF16)
    g3 = (r1 - g2.astype(F32)).astype(BF16)
    tri = tri_ref[...]
    bfull = (jnp.dot(tri, g1, preferred_element_type=F32)
             + jnp.dot(tri, g2, preferred_element_type=F32)
             + jnp.dot(tri, g3, preferred_element_type=F32))

    row = lax.broadcasted_iota(jnp.int32, (CH, CH), 0)
    col = lax.broadcasted_iota(jnp.int32, (CH, CH), 1)
    causal = row >= col
    mid = CH // 2
    gn = gn_ref[...]

    states = [S_ref[h] for h in range(GLA_HEADS)]
    for sc in range(TB // CH):
        rs = slice(sc * CH, (sc + 1) * CH)
        for h in range(GLA_HEADS):
            ks = slice(h * GLA_DK, (h + 1) * GLA_DK)
            vs = slice(h * GLA_DV, (h + 1) * GLA_DV)
            S = states[h]
            b = bfull[rs, ks]
            if sc > 0:
                b = b - bfull[sc * CH - 1:sc * CH, ks]
            ref = b[mid:mid + 1]
            blast = b[CH - 1:CH]
            qf = q_ref[rs, ks].astype(F32) * (GLA_DK ** -0.5)
            kf = k_ref[rs, ks].astype(F32)
            vv = v_ref[rs, vs]
            qe = (qf * jnp.exp(b - ref)).astype(BF16)
            ke = (kf * jnp.exp(ref - b)).astype(BF16)
            a = lax.dot_general(qe, ke, _NT, preferred_element_type=F32)
            a = jnp.where(causal, a, 0.0)
            o = (jnp.dot(a.astype(BF16), vv, preferred_element_type=F32)
                 + lax.dot_general((qf * jnp.exp(b)).astype(BF16), S.astype(BF16), _NT,
                                   preferred_element_type=F32))
            kd = (kf * jnp.exp(blast - b)).astype(BF16)
            states[h] = jnp.exp(blast) * S + lax.dot_general(vv, kd, _TN, preferred_element_type=F32)
            y = _rms(o, gn) * _silu(r_ref[rs, vs].astype(F32))
            y_ref[rs, vs] = y.astype(y_ref.dtype)
    for h in range(GLA_HEADS):
        S_ref[h] = states[h]

    @pl.when(c == pl.num_programs(1) - 1)
    def _():
        st_ref[...] = S_ref[...]


def gla_scan(p, glog, s0t, gn, TB, CH):
    B, T, _ = p.shape
    tri = jnp.asarray(np.tril(np.ones((TB, TB), np.float32)), BF16)
    return pl.pallas_call(
        functools.partial(_gla_body, TB=TB, CH=CH),
        grid=(B, T // TB),
        in_specs=[pl.BlockSpec((None, TB, GLA_KD), lambda b, c: (b, c, 0)),
                  pl.BlockSpec((None, TB, GLA_KD), lambda b, c: (b, c, 1)),
                  pl.BlockSpec((None, TB, GLA_VD), lambda b, c: (b, c, 1)),
                  pl.BlockSpec((None, TB, GLA_VD), lambda b, c: (b, c, 2)),
                  pl.BlockSpec((None, TB, GLA_KD), lambda b, c: (b, c, 0)),
                  pl.BlockSpec((None, GLA_HEADS, GLA_DV, GLA_DK), lambda b, c: (b, 0, 0, 0)),
                  pl.BlockSpec((1, GLA_DV), lambda b, c: (0, 0)),
                  pl.BlockSpec((TB, TB), lambda b, c: (0, 0))],
        out_specs=[pl.BlockSpec((None, TB, GLA_VD), lambda b, c: (b, c, 0)),
                   pl.BlockSpec((None, GLA_HEADS, GLA_DV, GLA_DK), lambda b, c: (b, 0, 0, 0))],
        out_shape=[jax.ShapeDtypeStruct((B, T, GLA_VD), BF16),
                   jax.ShapeDtypeStruct((B, GLA_HEADS, GLA_DV, GLA_DK), F32)],
        scratch_shapes=[pltpu.VMEM((GLA_HEADS, GLA_DV, GLA_DK), F32)],
        compiler_params=_params("parallel", "arbitrary"),
        name="gla_scan",
    )(p, p, p, p, glog, s0t, gn, tri)


def _attn_body(q_ref, kp_ref, kc_ref, vp_ref, vc_ref, bias_ref, o_ref, lse_ref):
    var = jnp.where(pl.program_id(2) == 0, 0, 1)
    lane = lax.broadcasted_iota(jnp.int32, (DIL_BLOCK, LANES), 1)
    lo = lane < DIL_HD
    lse_acc = jnp.zeros((DIL_BLOCK, LANES), F32)
    zero = jnp.zeros((), BF16)
    def scores(hp):
        cs = slice(hp * LANES, (hp + 1) * LANES)
        q2 = q_ref[:, cs]
        qab = jnp.concatenate([jnp.where(lo, q2, zero), jnp.where(lo, zero, q2)], axis=0)
        k2t = jnp.concatenate([kp_ref[cs, :], kc_ref[cs, :]], axis=1)
        return jnp.dot(qab, k2t, preferred_element_type=F32)

    ahead = scores(0)
    for hp in range(DIL_HEADS // 2):
        cs = slice(hp * LANES, (hp + 1) * LANES)
        s = ahead + bias_ref[var, hp]
        if hp + 1 < DIL_HEADS // 2:
            ahead = scores(hp + 1)
        v2 = jnp.concatenate([vp_ref[:, cs], vc_ref[:, cs]], axis=0)
        m = jnp.max(s, axis=-1, keepdims=True)
        p = jnp.exp(s - m)
        l = jnp.sum(p, axis=-1, keepdims=True)
        o = jnp.dot(p.astype(BF16), v2, preferred_element_type=F32) * (1.0 / l)
        o_ref[:, cs] = jnp.where(lo, o[:DIL_BLOCK], o[DIL_BLOCK:])
        lse = m + jnp.log(l)
        lse_acc = jnp.where(lane == 2 * hp, lse[:DIL_BLOCK], lse_acc)
        lse_acc = jnp.where(lane == 2 * hp + 1, lse[DIL_BLOCK:], lse_acc)
    lse_ref[...] = lse_acc


def dil_attention(q, kt, v, bias):
    B, d, L, _ = q.shape
    nb = L // DIL_BLOCK
    blk = (None, None, DIL_BLOCK, DIL_WIDTH)
    tblk = (None, None, DIL_WIDTH, DIL_BLOCK)
    prev = lambda n: jnp.maximum(n - 1, 0)
    return pl.pallas_call(
        _attn_body,
        grid=(B, d, nb),
        in_specs=[pl.BlockSpec(blk, lambda b, r, n: (b, r, n, 0)),
                  pl.BlockSpec(tblk, lambda b, r, n: (b, r, 0, prev(n))),
                  pl.BlockSpec(tblk, lambda b, r, n: (b, r, 0, n)),
                  pl.BlockSpec(blk, lambda b, r, n: (b, r, prev(n), 0)),
                  pl.BlockSpec(blk, lambda b, r, n: (b, r, n, 0)),
                  pl.BlockSpec((2, DIL_HEADS // 2, 2 * DIL_BLOCK, 2 * DIL_BLOCK), lambda b, r, n: (0, 0, 0, 0))],
        out_specs=[pl.BlockSpec(blk, lambda b, r, n: (b, r, n, 0)),
                   pl.BlockSpec((None, None, DIL_BLOCK, LANES), lambda b, r, n: (b, r, n, 0))],
        out_shape=[jax.ShapeDtypeStruct((B, d, L, DIL_WIDTH), F32),
                   jax.ShapeDtypeStruct((B, d, L, LANES), F32)],
        compiler_params=_params("parallel", "parallel", "arbitrary"),
        name="dil_attention",
    )(q, kt, kt, v, v, bias)


def _dilout_body(o0_ref, o1_ref, o2_ref, l0_ref, l1_ref, l2_ref, x_ref, w_ref, e_ref, out_ref):
    l0, l1, l2 = l0_ref[...], l1_ref[...], l2_ref[...]
    mx = jnp.maximum(jnp.maximum(l0, l1), l2)
    e0, e1, e2 = jnp.exp(l0 - mx), jnp.exp(l1 - mx), jnp.exp(l2 - mx)
    den = e0 + e1 + e2
    ex = e_ref[...]

    def expand(w):
        w1 = w.astype(BF16)
        w2 = (w - w1.astype(F32)).astype(BF16)
        return (jnp.dot(w1, ex, preferred_element_type=F32) + jnp.dot(w2, ex, preferred_element_type=F32))

    o = (expand(e0 / den) * o0_ref[...] + expand(e1 / den) * o1_ref[...] + expand(e2 / den) * o2_ref[...])
    out_ref[...] = x_ref[...] + jnp.dot(o.astype(BF16), w_ref[...], preferred_element_type=F32)


def dil_out(os, ls, x, w, layer, expand, tm):
    M = x.shape[0]
    ospec = pl.BlockSpec((tm, DIL_WIDTH), lambda i: (i, 0))
    lspec = pl.BlockSpec((tm, LANES), lambda i: (i, 0))
    return pl.pallas_call(
        _dilout_body,
        grid=(M // tm,),
        in_specs=[ospec, ospec, ospec, lspec, lspec, lspec, ospec,
                  pl.BlockSpec((None, DIL_WIDTH, D_MODEL), lambda i: (layer, 0, 0)),
                  pl.BlockSpec((LANES, DIL_WIDTH), lambda i: (0, 0))],
        out_specs=pl.BlockSpec((tm, D_MODEL), lambda i: (i, 0)),
        out_shape=jax.ShapeDtypeStruct((M, D_MODEL), F32),
        compiler_params=_params("parallel"),
        name="dil_out",
    )(*os, *ls, x, w, expand)


S_CHUNK = 512
S_NEWPAD = LANES


def _sattn_body(qkv_ref, k0_ref, v0_ref, k1_ref, v1_ref, k2_ref, v2_ref, b0_ref, b1_ref, b2_ref, bn_ref,
                o_ref, m_ref, l_ref, acc_ref, *, seq):
    c = pl.program_id(1)
    R = seq * DIL_HEADS
    rowh = lax.broadcasted_iota(jnp.int32, (R, DIL_WIDTH), 0) % DIL_HEADS
    colh = lax.broadcasted_iota(jnp.int32, (R, DIL_WIDTH), 1) // DIL_HD
    hmask = rowh == colh

    def seg(g, part):
        return qkv_ref[:, (3 * g + part) * DIL_WIDTH:(3 * g + part + 1) * DIL_WIDTH]

    def qbd(g):
        q = seg(g, 0)
        qrep = jnp.concatenate([jnp.broadcast_to(q[s:s + 1], (DIL_HEADS, DIL_WIDTH)) for s in range(seq)], axis=0)
        return jnp.where(hmask, qrep, 0.0).astype(BF16)

    def update(parts):
        m_old = m_ref[...]
        m_new = m_old
        for s, _ in parts:
            m_new = jnp.maximum(m_new, jnp.max(s, axis=-1, keepdims=True))
        alpha = jnp.exp(m_old - m_new)
        l = alpha * l_ref[...]
        acc = alpha * acc_ref[...]
        for s, pv_fn in parts:
            p = jnp.exp(s - m_new)
            l = l + jnp.sum(p, axis=-1, keepdims=True)
            acc = acc + pv_fn(p.astype(BF16))
        m_ref[...] = m_new
        l_ref[...] = l
        acc_ref[...] = acc

    def cache_segment(g, kt_ref, vt_ref, b_ref):
        s = jnp.dot(qbd(g), kt_ref[...].astype(BF16), preferred_element_type=F32) + b_ref[...]
        return s, lambda p: lax.dot_general(p, vt_ref[...].astype(BF16), _NT, preferred_element_type=F32)

    def new_segment(g):
        pad = jnp.zeros((S_NEWPAD - seq, DIL_WIDTH), F32)
        kn = jnp.concatenate([seg(g, 1), pad], axis=0).astype(BF16)
        vn = jnp.concatenate([seg(g, 2), pad], axis=0).astype(BF16)
        s = lax.dot_general(qbd(g), kn, _NT, preferred_element_type=F32) + bn_ref[g]
        return s, lambda p: jnp.dot(p, vn, preferred_element_type=F32)

    @pl.when(c == 0)
    def _():
        m_ref[...] = jnp.full(m_ref.shape, 2 * NEG, F32)
        l_ref[...] = jnp.zeros(l_ref.shape, F32)
        acc_ref[...] = jnp.zeros(acc_ref.shape, F32)
        update([new_segment(g) for g in range(N_GROUPS)]
               + [cache_segment(0, k0_ref, v0_ref, b0_ref), cache_segment(1, k1_ref, v1_ref, b1_ref),
                  cache_segment(2, k2_ref, v2_ref, b2_ref)])

    @pl.when(c > 0)
    def _():
        update([cache_segment(2, k2_ref, v2_ref, b2_ref)])

    @pl.when(c == pl.num_programs(1) - 1)
    def _():
        on = jnp.where(hmask, acc_ref[...] / l_ref[...], 0.0).astype(BF16)
        srow = lax.broadcasted_iota(jnp.int32, (8, R), 0)
        scol = lax.broadcasted_iota(jnp.int32, (8, R), 1) // DIL_HEADS
        sel = jnp.where(srow == scol, 1.0, 0.0).astype(BF16)
        o_ref[...] = jnp.dot(sel, on, preferred_element_type=F32)[:seq]


def dil_sample_attention(li, qkv, kts, vts, bbs, bn):
    DB, S, _ = qkv.shape
    R = DIL_HEADS * S
    W2 = kts[2].shape[-1]
    in_specs = [pl.BlockSpec((None, S, qkv.shape[-1]), lambda b, c: (b, 0, 0))]
    args = [qkv]
    for g in range(2):
        for a in (kts[g], vts[g]):
            in_specs.append(pl.BlockSpec((None, None, DIL_WIDTH, a.shape[-1]), lambda b, c: (li, b, 0, 0)))
            args.append(a)
    for a in (kts[2], vts[2]):
        in_specs.append(pl.BlockSpec((None, None, DIL_WIDTH, S_CHUNK), lambda b, c: (li, b, 0, c)))
        args.append(a)
    for g in range(2):
        in_specs.append(pl.BlockSpec(bbs[g].shape, lambda b, c: (0, 0)))
        args.append(bbs[g])
    in_specs.append(pl.BlockSpec((R, S_CHUNK), lambda b, c: (0, c)))
    args.append(bbs[2])
    in_specs.append(pl.BlockSpec(bn.shape, lambda b, c: (0, 0, 0)))
    args.append(bn)
    return pl.pallas_call(
        functools.partial(_sattn_body, seq=S),
        grid=(DB, W2 // S_CHUNK),
        in_specs=in_specs,
        out_specs=pl.BlockSpec((None, S, DIL_WIDTH), lambda b, c: (b, 0, 0)),
        out_shape=jax.ShapeDtypeStruct((DB, S, DIL_WIDTH), F32),
        scratch_shapes=[pltpu.VMEM((R, 1), F32), pltpu.VMEM((R, 1), F32), pltpu.VMEM((R, DIL_WIDTH), F32)],
        compiler_params=_params("parallel", "arbitrary"),
        name="dil_sample_attention",
    )(*args)


FFN_CHUNK = MXU_WIDTH
HALO = 16


FFN_ROWBLOCK = 128


def _causal_conv(ext, cw, cb):
    u1 = pltpu.roll(ext, 1, axis=0)[HALO:]
    u2 = pltpu.roll(ext, 2, axis=0)[HALO:]
    return cb + (cw[0:1] * u2 + cw[1:2] * u1 + cw[2:3] * ext[HALO:])


def _ffnup_body(x_ref, xh_ref, g_ref, wg_ref, wv_ref, cw_ref, cb_ref, act_ref, tail_ref, h_ref, *, tm):
    first = pl.program_id(1) == 0
    hh = _rms(xh_ref[...], g_ref[...])
    h_ref[0:HALO] = jnp.where(first, 0.0, hh).astype(BF16)
    h_ref[HALO:HALO + tm] = _rms(x_ref[...], g_ref[...]).astype(BF16)
    for c in range(D_FF // FFN_CHUNK):
        cols = (slice(c * FFN_CHUNK, (c + 1) * FFN_CHUNK),
                slice(D_FF + c * FFN_CHUNK, D_FF + (c + 1) * FFN_CHUNK))
        tails = [None, None]
        for a in range(0, tm, FFN_ROWBLOCK):
            conv = []
            for idx, w_ref in enumerate((wg_ref, wv_ref)):
                if a == 0:
                    ext = jnp.dot(h_ref[0:HALO + FFN_ROWBLOCK], w_ref[:, cols[0]], preferred_element_type=F32)
                else:
                    u = jnp.dot(h_ref[HALO + a:HALO + a + FFN_ROWBLOCK], w_ref[:, cols[0]],
                                preferred_element_type=F32)
                    ext = jnp.concatenate([tails[idx], u], axis=0)
                tails[idx] = ext[FFN_ROWBLOCK:]
                conv.append(_causal_conv(ext, cw_ref[:, cols[idx]], cb_ref[:, cols[idx]]))
            act_ref[a:a + FFN_ROWBLOCK, cols[0]] = (_silu(conv[0]) * conv[1]).astype(act_ref.dtype)
        for idx in range(2):
            tail_ref[:, cols[idx]] = tails[idx][HALO - 8:]


def ffn_up_prompt(x, g, w, layer, cw, cb, tm):
    B, T, D = x.shape
    hblk = tm // HALO
    return pl.pallas_call(
        functools.partial(_ffnup_body, tm=tm),
        grid=(B, T // tm),
        in_specs=[pl.BlockSpec((None, tm, D), lambda b, i: (b, i, 0)),
                  pl.BlockSpec((None, HALO, D), lambda b, i: (b, jnp.maximum(i * hblk - 1, 0), 0)),
                  pl.BlockSpec((1, D), lambda b, i: (0, 0)),
                  pl.BlockSpec((None, D, D_FF), lambda b, i: (layer, 0, 0)),
                  pl.BlockSpec((None, D, D_FF), lambda b, i: (layer, 0, 1)),
                  pl.BlockSpec((CONV_WIDTH, 2 * D_FF), lambda b, i: (0, 0)),
                  pl.BlockSpec((1, 2 * D_FF), lambda b, i: (0, 0))],
        out_specs=[pl.BlockSpec((None, tm, D_FF), lambda b, i: (b, i, 0)),
                   pl.BlockSpec((None, 8, 2 * D_FF), lambda b, i: (b, 0, 0))],
        out_shape=[jax.ShapeDtypeStruct((B, T, D_FF), BF16),
                   jax.ShapeDtypeStruct((B, 8, 2 * D_FF), F32)],
        scratch_shapes=[pltpu.VMEM((HALO + tm, D), BF16)],
        compiler_params=_params("parallel", "arbitrary"),
        name="ffn_up_prompt",
    )(x, x, g, w, w, cw, cb)


def _ffnup_s_body(x_ref, g_ref, wg_ref, wv_ref, cw_ref, cb_ref, c1_ref, c2_ref, act_ref, u_ref, u_scr, *, rows, seq):
    h = _rms(x_ref[...], g_ref[...]).astype(BF16)
    pos = lax.broadcasted_iota(jnp.int32, (rows, FFN_CHUNK), 0) % seq
    u_scr[:, 0:HALO] = jnp.zeros((2, HALO, FFN_CHUNK), F32)
    for c in range(D_FF // FFN_CHUNK):
        acts = []
        for idx, off in enumerate((0, D_FF)):
            cs = slice(off + c * FFN_CHUNK, off + (c + 1) * FFN_CHUNK)
            w_ref = wg_ref if idx == 0 else wv_ref
            u = jnp.dot(h, w_ref[:, c * FFN_CHUNK:(c + 1) * FFN_CHUNK], preferred_element_type=F32)
            u_scr[idx, HALO:HALO + rows] = u
            u1 = jnp.where(pos >= 1, u_scr[idx, HALO - 1:HALO - 1 + rows], c1_ref[:, cs])
            u2 = jnp.where(pos >= 2, u_scr[idx, HALO - 2:HALO - 2 + rows], c2_ref[:, cs])
            acts.append(cb_ref[:, cs] + (cw_ref[0:1, cs] * u2 + cw_ref[1:2, cs] * u1 + cw_ref[2:3, cs] * u))
            u_ref[:, cs] = u
        act_ref[:, c * FFN_CHUNK:(c + 1) * FFN_CHUNK] = (_silu(acts[0]) * acts[1]).astype(act_ref.dtype)


def ffn_up_sample(x, g, w, layer, cw, cb, carry1, carry2, seq):
    M, D = x.shape
    full = lambda shape: pl.BlockSpec(shape, lambda i: tuple(0 for _ in shape))
    return pl.pallas_call(
        functools.partial(_ffnup_s_body, rows=M, seq=seq),
        grid=(1,),
        in_specs=[full((M, D)), full((1, D)),
                  pl.BlockSpec((None, D, D_FF), lambda i: (layer, 0, 0)),
                  pl.BlockSpec((None, D, D_FF), lambda i: (layer, 0, 1)),
                  full((CONV_WIDTH, 2 * D_FF)), full((1, 2 * D_FF)),
                  full((M, 2 * D_FF)), full((M, 2 * D_FF))],
        out_specs=[full((M, D_FF)), full((M, 2 * D_FF))],
        out_shape=[jax.ShapeDtypeStruct((M, D_FF), BF16),
                   jax.ShapeDtypeStruct((M, 2 * D_FF), F32)],
        scratch_shapes=[pltpu.VMEM((2, HALO + M, FFN_CHUNK), F32)],
        compiler_params=_params("arbitrary"),
        name="ffn_up_sample",
    )(x, g, w, w, cw, cb, carry1, carry2)


def _rel_bucket(dist):
    max_exact = NUM_BUCKETS // 2
    df = jnp.maximum(dist, 1).astype(F32)
    large = max_exact + (jnp.log(df / max_exact) / math.log(MAX_DISTANCE / max_exact)
                         * (NUM_BUCKETS - max_exact)).astype(jnp.int32)
    large = jnp.minimum(large, NUM_BUCKETS - 1)
    return jnp.where(dist < max_exact, dist, large)


def _step_table(rel_bias, g, d):
    J = DIL_GROUPS[g][0] // d
    tab = rel_bias[:, g * DIL_HEADS:(g + 1) * DIL_HEADS].astype(F32)
    return tab[_rel_bucket(jnp.arange(J + 1) * d)]


def _lookup(table, idx):
    onehot = jax.nn.one_hot(jnp.asarray(idx, jnp.int32), table.shape[0], dtype=F32)
    return jnp.einsum('...j,jh->...h', onehot, table, precision=lax.Precision.HIGHEST)


def _prompt_bias(rel_bias, g, d):
    J = DIL_GROUPS[g][0] // d
    qi = np.arange(DIL_BLOCK)[:, None]
    ki = np.arange(2 * DIL_BLOCK)[None, :]
    rel = qi + DIL_BLOCK - ki
    bias = jnp.moveaxis(_lookup(_step_table(rel_bias, g, d), np.clip(rel, 0, J)), -1, 0)
    out = []
    for first in (True, False):
        valid = (rel >= 0) & (rel <= J) & ((ki >= DIL_BLOCK) | (not first))
        b = jnp.where(jnp.asarray(valid)[None], bias, NEG)
        out.append(b.reshape(DIL_HEADS // 2, 2 * DIL_BLOCK, 2 * DIL_BLOCK))
    return jnp.stack(out)


def _sample_bias(rel_bias, g, d, seq):
    W = DIL_GROUPS[g][0]
    J = W // d
    table = _step_table(rel_bias, g, d)
    s = np.arange(seq)[:, None]
    dist = W + s - np.arange(W)[None, :]
    valid = (dist >= 0) & (dist <= W) & (dist % d == 0)
    bval = jnp.moveaxis(_lookup(table, np.where(valid, dist // d, 0)), -1, 1)
    bbuf = jnp.where(jnp.asarray(valid)[:, None, :], bval, NEG).reshape(seq * DIL_HEADS, W)
    dn = s - np.arange(S_NEWPAD)[None, :]
    vn = (dn >= 0) & (dn % d == 0) & (dn // d <= J) & (np.arange(S_NEWPAD)[None, :] < seq)
    bnew = jnp.moveaxis(_lookup(table, np.where(vn, dn // d, 0)), -1, 1)
    bnew = jnp.where(jnp.asarray(vn)[:, None, :], bnew, NEG).reshape(seq * DIL_HEADS, S_NEWPAD)
    return bbuf, bnew


def _colgain(q_gain, k_gain, ngroups):
    seg = jnp.concatenate([jnp.tile(q_gain.astype(F32) * DIL_SCALE, DIL_HEADS),
                           jnp.tile(k_gain.astype(F32), DIL_HEADS),
                           jnp.ones((DIL_WIDTH,), F32)])
    return jnp.tile(seg, ngroups)[None, :]


def kernel(x_prompt, x_sample, state_gla, cache_k_g0, cache_v_g0, cache_k_g1, cache_v_g1, cache_k_g2, cache_v_g2,
           state_ffn_conv, rel_bias, norm_mix, norm_ffn, gla_w_in, gla_w_gate2, gla_b_gate, gla_norm, gla_w_out,
           dil_w_in, dil_q_norm, dil_k_norm, dil_w_out, ffn_w_up, ffn_conv_w, ffn_conv_b, ffn_w_down):
    B, T, D = x_prompt.shape
    DB, S, _ = x_sample.shape
    MP, MS = B * T, DB * S
    assert all(c.shape[2] == w for c, (w, _) in zip((cache_k_g0, cache_k_g1, cache_k_g2), DIL_GROUPS))

    xp = x_prompt.reshape(MP, D)
    xs = x_sample.reshape(MS, D)
    fmajor = lambda c: c.transpose(0, 1, 3, 4, 2).reshape(c.shape[0], c.shape[1], DIL_WIDTH, c.shape[2])
    k_caches = tuple(fmajor(c) for c in (cache_k_g0, cache_k_g1, cache_k_g2))
    v_caches = tuple(fmajor(c) for c in (cache_v_g0, cache_v_g1, cache_v_g2))

    blockdiag = jnp.asarray(np.kron(np.eye(MXU_WIDTH // DIL_HD), np.ones((DIL_HD, DIL_HD))), BF16)
    expand = jnp.asarray(np.kron(np.eye(LANES, DIL_HEADS, dtype=np.float32).reshape(LANES, DIL_HEADS),
                                 np.ones((1, DIL_HD), np.float32)), BF16)
    p_bias = [_prompt_bias(rel_bias, g, d) for g, (_, d) in enumerate(DIL_GROUPS)]
    s_bias = [_sample_bias(rel_bias, g, d, S) for g, (_, d) in enumerate(DIL_GROUPS)]
    s_bias_buf = [b for b, _ in s_bias]
    s_bias_new = jnp.stack([b for _, b in s_bias])

    gla_p, gla_s = [], []
    kp = [[] for _ in DIL_GROUPS]
    vp = [[] for _ in DIL_GROUPS]
    kq = [[] for _ in DIL_GROUPS]
    vq = [[] for _ in DIL_GROUPS]
    conv_p, conv_s = [], []
    SPAD = 16

    gla_w_in_b = to_bf16(gla_w_in, 256)
    gla_w_out_b = to_bf16(gla_w_out, 256)
    dil_w_in_b = to_bf16(dil_w_in, 128)
    dil_w_out_b = to_bf16(dil_w_out, 256)
    ffn_w_up_b = to_bf16(ffn_w_up, 256)
    ffn_w_down_b = to_bf16(ffn_w_down, 256)
    dil_w_in_t = jnp.swapaxes(dil_w_in_b, 1, 2)
    n_main = 2 * GLA_KD + 2 * GLA_VD

    for i in range(DEPTH):
        li = i // N_MIXERS
        gmix = norm_mix[i][None, :].astype(F32)
        if i % N_MIXERS == 0:
            wgz = jnp.pad(gla_w_in[li][:, n_main:], ((0, 0), (0, LANES - GLA_GATE_RANK))).astype(BF16)
            wg2 = jnp.pad(gla_w_gate2[li], ((0, LANES - GLA_GATE_RANK), (0, 0))).astype(BF16)
            bg = gla_b_gate[li][None, :].astype(F32)
            gn = gla_norm[li][None, :].astype(F32)
            pp = norm_matmul(xp, gmix, gla_w_in_b, li, n_main, BF16, 1024, 1024).reshape(B, T, -1)
            glp = gla_gate(xp, gmix, wgz, wg2, bg, 1024).reshape(B, T, GLA_KD)
            s0 = jnp.zeros((B, GLA_HEADS, GLA_DV, GLA_DK), F32)
            yp, stp = gla_scan(pp, glp, s0, gn, 128, GLA_CHUNK)
            xp = matmul_residual(yp.reshape(MP, GLA_VD), gla_w_out_b, li, xp, 1024, D)
            gla_p.append(jnp.swapaxes(stp, -1, -2))
            ps = norm_matmul(xs, gmix, gla_w_in_b, li, n_main, BF16, MS, 1024).reshape(DB, S, -1)
            gls = gla_gate(xs, gmix, wgz, wg2, bg, MS).reshape(DB, S, GLA_KD)
            ps = jnp.pad(ps, ((0, 0), (0, SPAD - S), (0, 0)))
            gls = jnp.pad(gls, ((0, 0), (0, SPAD - S), (0, 0)))
            ys, sts = gla_scan(ps, gls, jnp.swapaxes(state_gla[li].astype(F32), -1, -2), gn, SPAD, SPAD)
            xs = matmul_residual(ys[:, :S].reshape(MS, GLA_VD), gla_w_out_b, li, xs, MS, D)
            gla_s.append(jnp.swapaxes(sts, -1, -2))
        else:
            cg = _colgain(dil_q_norm[li], dil_k_norm[li], N_GROUPS)
            kgain = jnp.broadcast_to(jnp.tile(dil_k_norm[li].astype(F32), DIL_HEADS)[:, None], (DIL_WIDTH, LANES))
            os_, ls_ = [], []
            hs = {d: rms_prep(xp.reshape(B, T, D), gmix, d, 1024) for _, d in DIL_GROUPS}
            for g, (W, d) in enumerate(DIL_GROUPS):
                L = T // d
                hg = hs[d].reshape(MP, D)
                q = proj_rows(hg, dil_w_in_b, li, 3 * g, cg, blockdiag, True, 2048).reshape(B, d, L, DIL_WIDTH)
                v = proj_rows(hg, dil_w_in_b, li, 3 * g + 2, cg, blockdiag, False, 2048).reshape(B, d, L, DIL_WIDTH)
                kt = proj_cols(hs[d].reshape(B * d, L, D), dil_w_in_t, li, 3 * g + 1, kgain, blockdiag, True, BF16,
                               min(L, 512)).reshape(B, d, DIL_WIDTH, L)
                o, lse = dil_attention(q, kt, v, p_bias[g])
                if d > 1:
                    o = o.transpose(0, 2, 1, 3)
                    lse = lse.transpose(0, 2, 1, 3)
                os_.append(o.reshape(MP, DIL_WIDTH))
                ls_.append(lse.reshape(MP, LANES))
                keep = min(W, T)
                tl = min(keep, 512)
                tail = dict(first=(T - keep) // tl, count=keep // tl)
                kp[g].append(proj_cols(hs[1].reshape(B, T, D), dil_w_in_t, li, 3 * g + 1, kgain, blockdiag, True, F32,
                                       tl, **tail))
                vp[g].append(proj_cols(hs[1].reshape(B, T, D), dil_w_in_t, li, 3 * g + 2, kgain, blockdiag, False, F32,
                                       tl, **tail))
            xp = dil_out(os_, ls_, xp, dil_w_out_b, li, expand, 512)
            qkvs = norm_matmul_qk(xs.reshape(1, MS, D), gmix, dil_w_in_b, li, cg, blockdiag, F32, MS, 1024)
            osamp = dil_sample_attention(li, qkvs.reshape(DB, S, -1), k_caches, v_caches, s_bias_buf, s_bias_new)
            qkvs = qkvs.reshape(DB, S, N_GROUPS, 3, DIL_HEADS, DIL_HD)
            for g in range(N_GROUPS):
                kq[g].append(qkvs[:, :, g, 1])
                vq[g].append(qkvs[:, :, g, 2])
            osamp = osamp.reshape(MS, DIL_WIDTH).astype(BF16)
            xs = matmul_residual(osamp, dil_w_out_b, li, xs, MS, D)

        gffn = norm_ffn[i][None, :].astype(F32)
        cw = ffn_conv_w[i].astype(F32)
        cb = ffn_conv_b[i][None, :].astype(F32)
        act, tail = ffn_up_prompt(xp.reshape(B, T, D), gffn, ffn_w_up_b, i, cw, cb, 512)
        conv_p.append(tail[:, 8 - (CONV_WIDTH - 1):])
        xp = matmul_residual(act.reshape(MP, D_FF), ffn_w_down_b, i, xp, 1024, D)
        buf = state_ffn_conv[i].astype(F32)
        zeros = jnp.zeros((DB, S - 1, 2 * D_FF), F32)
        carry1 = jnp.concatenate([buf[:, 1:2], zeros], axis=1).reshape(MS, 2 * D_FF)
        carry2 = jnp.concatenate([buf, zeros[:, 1:]], axis=1).reshape(MS, 2 * D_FF)
        acts, us = ffn_up_sample(xs, gffn, ffn_w_up_b, i, cw, cb, carry1, carry2, S)
        conv_s.append(us.reshape(DB, S, 2 * D_FF)[:, S - (CONV_WIDTH - 1):])
        xs = matmul_residual(acts, ffn_w_down_b, i, xs, MS, D)

    outs = [xp.reshape(B, T, D), xs.reshape(DB, S, D), jnp.stack(gla_p), jnp.stack(gla_s)]
    pmajor = lambda c: c.reshape(c.shape[0], B, DIL_HEADS, DIL_HD, c.shape[-1]).transpose(0, 1, 4, 2, 3)
    for g in range(N_GROUPS):
        outs += [pmajor(jnp.stack(kp[g])), jnp.stack(kq[g]), pmajor(jnp.stack(vp[g])), jnp.stack(vq[g])]
    outs += [jnp.stack(conv_p), jnp.stack(conv_s)]
    return tuple(outs)
```

```python
import functools
import math

import numpy as np
import jax
import jax.numpy as jnp
from jax import lax
from jax.experimental import pallas as pl
from jax.experimental.pallas import tpu as pltpu

F32 = jnp.float32
BF16 = jnp.bfloat16

D_MODEL = 1024
DEPTH = 4
N_MIXERS = 2
GLA_HEADS = 4
GLA_KD = 512
GLA_VD = 1024
GLA_DK = 128
GLA_DV = 256
GLA_GATE_RANK = 16
GLA_GATE_NORM = 16.0
GLA_CHUNK = 32
DIL_GROUPS = ((128, 1), (512, 4), (2048, 16))
N_GROUPS = 3
DIL_HEADS = 16
DIL_HD = 64
DIL_WIDTH = 1024
DIL_BLOCK = 128
DIL_SCALE = DIL_HD ** -0.5
NUM_BUCKETS = 32
MAX_DISTANCE = 2048
D_FF = 2816
CONV_WIDTH = 3
EPS = 1e-6
NEG = -1e30

LANES = 128
MXU_WIDTH = 256
VMEM_LIMIT = 48 * 1024 * 1024

_NT = (((1,), (1,)), ((), ()))
_TN = (((0,), (0,)), ((), ()))


def _params(*sem):
    return pltpu.CompilerParams(dimension_semantics=sem, vmem_limit_bytes=VMEM_LIMIT)


def _rms(x, g):
    return x * lax.rsqrt(jnp.mean(x * x, axis=-1, keepdims=True) + EPS) * g


def _silu(x):
    return x * (1.0 / (1.0 + jnp.exp(-x)))


def _cast_body(w_ref, o_ref):
    o_ref[...] = w_ref[...].astype(o_ref.dtype)


def to_bf16(w, rows):
    L, R, C = w.shape
    return pl.pallas_call(
        _cast_body,
        grid=(L, R // rows),
        in_specs=[pl.BlockSpec((None, rows, C), lambda l, i: (l, i, 0))],
        out_specs=pl.BlockSpec((None, rows, C), lambda l, i: (l, i, 0)),
        out_shape=jax.ShapeDtypeStruct((L, R, C), BF16),
        compiler_params=_params("parallel", "parallel"),
        name="to_bf16",
    )(w)


def _nm_body(x_ref, g_ref, w_ref, o_ref, h_ref):
    @pl.when(pl.program_id(1) == 0)
    def _():
        h_ref[...] = _rms(x_ref[...], g_ref[...]).astype(BF16)

    o_ref[...] = jnp.dot(h_ref[...], w_ref[...], preferred_element_type=F32).astype(o_ref.dtype)


def norm_matmul(x, g, w, layer, ncols, out_dtype, tm, tn):
    M, D = x.shape
    return pl.pallas_call(
        _nm_body,
        grid=(M // tm, ncols // tn),
        in_specs=[pl.BlockSpec((tm, D), lambda i, j: (i, 0)),
                  pl.BlockSpec((1, D), lambda i, j: (0, 0)),
                  pl.BlockSpec((None, D, tn), lambda i, j: (layer, 0, j))],
        out_specs=pl.BlockSpec((tm, tn), lambda i, j: (i, j)),
        out_shape=jax.ShapeDtypeStruct((M, ncols), out_dtype),
        scratch_shapes=[pltpu.VMEM((tm, D), BF16)],
        compiler_params=_params("parallel", "arbitrary"),
        name="norm_matmul",
    )(x, g, w)


PERM_ROWS = 16
QK_ROWBLOCK = 256


def _nmqk_body(x_ref, g_ref, w_ref, cg_ref, bd_ref, perm_ref, o_ref, h_ref, *, tm, tn, d):
    j = pl.program_id(2)
    rows = tm // d

    @pl.when(j == 0)
    def _():
        h = _rms(x_ref[...], g_ref[...]).astype(BF16)
        if d == 1:
            h_ref[...] = h
        else:
            sub = PERM_ROWS * d
            for s in range(tm // sub):
                hs = jnp.dot(perm_ref[...], h[s * sub:(s + 1) * sub], preferred_element_type=F32).astype(BF16)
                for r in range(d):
                    dst = r * rows + PERM_ROWS * s
                    h_ref[dst:dst + PERM_ROWS, :] = hs[r * PERM_ROWS:(r + 1) * PERM_ROWS]

    is_norm = ((j * tn) // DIL_WIDTH) % 3 != 2

    rb = min(QK_ROWBLOCK, tm)

    def emit(a, cs, y):
        yb = y.astype(o_ref.dtype)
        for r in range(d):
            lo, hi = max(a, r * rows), min(a + rb, (r + 1) * rows)
            if lo < hi:
                o_ref[r, lo - r * rows:hi - r * rows, cs] = yb[lo - a:hi - a]

    def chunks(norm):
        blocks = [(slice(c * MXU_WIDTH, (c + 1) * MXU_WIDTH), a)
                  for c in range(tn // MXU_WIDTH) for a in range(0, tm, rb)]
        proj = lambda cs, a: jnp.dot(h_ref[a:a + rb, :], w_ref[:, cs], preferred_element_type=F32)
        ahead = proj(*blocks[0])
        for n, (cs, a) in enumerate(blocks):
            pc = ahead
            if n + 1 < len(blocks):
                ahead = proj(*blocks[n + 1])
            if norm:
                ss = jnp.dot((pc * pc).astype(BF16), bd_ref[...], preferred_element_type=F32)
                pc = pc * lax.rsqrt(ss * (1.0 / DIL_HD) + EPS) * cg_ref[:, cs]
            emit(a, cs, pc)

    @pl.when(is_norm)
    def _():
        chunks(True)

    @pl.when(jnp.logical_not(is_norm))
    def _():
        chunks(False)


def norm_matmul_qk(x, g, w, layer, colgain, bd, out_dtype, tm, tn, d=1, group=None):
    B, T, D = x.shape
    N = w.shape[2] if group is None else 3 * DIL_WIDTH
    j0 = 0 if group is None else group * N // tn
    sub = PERM_ROWS * d
    return pl.pallas_call(
        functools.partial(_nmqk_body, tm=tm, tn=tn, d=d),
        grid=(B, T // tm, N // tn),
        in_specs=[pl.BlockSpec((None, tm, D), lambda b, i, j: (b, i, 0)),
                  pl.BlockSpec((1, D), lambda b, i, j: (0, 0)),
                  pl.BlockSpec((None, D, tn), lambda b, i, j: (layer, 0, j + j0)),
                  pl.BlockSpec((1, tn), lambda b, i, j: (0, j + j0)),
                  pl.BlockSpec((MXU_WIDTH, MXU_WIDTH), lambda b, i, j: (0, 0)),
                  pl.BlockSpec((sub, sub), lambda b, i, j: (0, 0))],
        out_specs=pl.BlockSpec((None, d, tm // d, tn), lambda b, i, j: (b, 0, i, j)),
        out_shape=jax.ShapeDtypeStruct((B, d, T // d, N), out_dtype),
        scratch_shapes=[pltpu.VMEM((tm, D), BF16)],
        compiler_params=_params("parallel", "parallel", "arbitrary"),
        name="norm_matmul_qk",
    )(x, g, w, colgain, bd, _perm_matrix(d))


def _perm_matrix(d):
    sub = PERM_ROWS * d
    pm = np.zeros((sub, sub), np.float32)
    for r in range(d):
        for i in range(PERM_ROWS):
            pm[r * PERM_ROWS + i, i * d + r] = 1.0
    return jnp.asarray(pm, BF16)


def _prep_body(x_ref, g_ref, *refs, tm, ds):
    perm_refs, o_refs = refs[:len(ds)], refs[len(ds):]
    h = _rms(x_ref[...], g_ref[...]).astype(BF16)
    for d, perm_ref, o_ref in zip(ds, perm_refs, o_refs):
        if d == 1:
            o_ref[0] = h
            continue
        sub = PERM_ROWS * d
        for s in range(tm // sub):
            hs = jnp.dot(perm_ref[...], h[s * sub:(s + 1) * sub], preferred_element_type=F32).astype(BF16)
            for r in range(d):
                o_ref[r, PERM_ROWS * s:PERM_ROWS * (s + 1), :] = hs[r * PERM_ROWS:(r + 1) * PERM_ROWS]


def rms_prep(x, g, ds, tm):
    B, T, D = x.shape
    return pl.pallas_call(
        functools.partial(_prep_body, tm=tm, ds=ds),
        grid=(B, T // tm),
        in_specs=[pl.BlockSpec((None, tm, D), lambda b, i: (b, i, 0)),
                  pl.BlockSpec((1, D), lambda b, i: (0, 0))]
                 + [pl.BlockSpec((PERM_ROWS * d, PERM_ROWS * d), lambda b, i: (0, 0)) for d in ds],
        out_specs=[pl.BlockSpec((None, d, tm // d, D), lambda b, i: (b, 0, i, 0)) for d in ds],
        out_shape=[jax.ShapeDtypeStruct((B, d, T // d, D), BF16) for d in ds],
        compiler_params=_params("parallel", "parallel"),
        name="rms_prep",
    )(x, g, *[_perm_matrix(d) for d in ds])


def _projrows_body(h_ref, w_ref, cg_ref, bd_ref, o_ref, *, tm, norm):
    rb = min(QK_ROWBLOCK, tm)
    blocks = [(slice(c * MXU_WIDTH, (c + 1) * MXU_WIDTH), a)
              for c in range(DIL_WIDTH // MXU_WIDTH) for a in range(0, tm, rb)]
    proj = lambda cs, a: jnp.dot(h_ref[a:a + rb, :], w_ref[:, cs], preferred_element_type=F32)
    ahead = proj(*blocks[0])
    for n, (cs, a) in enumerate(blocks):
        pc = ahead
        if n + 1 < len(blocks):
            ahead = proj(*blocks[n + 1])
        if norm:
            ss = jnp.dot((pc * pc).astype(BF16), bd_ref[...], preferred_element_type=F32)
            pc = pc * lax.rsqrt(ss * (1.0 / DIL_HD) + EPS) * cg_ref[:, cs]
        o_ref[a:a + rb, cs] = pc.astype(o_ref.dtype)


def proj_rows(h, w, layer, seg, colgain, bd, norm, tm):
    M, D = h.shape
    return pl.pallas_call(
        functools.partial(_projrows_body, tm=tm, norm=norm),
        grid=(M // tm,),
        in_specs=[pl.BlockSpec((tm, D), lambda i: (i, 0)),
                  pl.BlockSpec((None, D, DIL_WIDTH), lambda i: (layer, 0, seg)),
                  pl.BlockSpec((1, DIL_WIDTH), lambda i: (0, seg)),
                  pl.BlockSpec((MXU_WIDTH, MXU_WIDTH), lambda i: (0, 0))],
        out_specs=pl.BlockSpec((tm, DIL_WIDTH), lambda i: (i, 0)),
        out_shape=jax.ShapeDtypeStruct((M, DIL_WIDTH), BF16),
        compiler_params=_params("parallel"),
        name="proj_rows",
    )(h, w, colgain, bd)


def _projcols_body(h_ref, wt_ref, cg_ref, bd_ref, o_ref, *, tl, norm):
    kt = lax.dot_general(wt_ref[...], h_ref[...], _NT, preferred_element_type=F32)
    for fb in range(DIL_WIDTH // MXU_WIDTH):
        fs = slice(fb * MXU_WIDTH, (fb + 1) * MXU_WIDTH)
        blk = kt[fs]
        if norm:
            ss = jnp.dot(bd_ref[...], (blk * blk).astype(BF16), preferred_element_type=F32)
            gain = jnp.concatenate([cg_ref[fs, :]] * (tl // LANES), axis=1)
            blk = blk * lax.rsqrt(ss * (1.0 / DIL_HD) + EPS) * gain
        o_ref[fs, :] = blk.astype(o_ref.dtype)


def proj_cols(h, wt, layer, seg, rowgain, bd, norm, out_dtype, tl, first=0, count=None):
    G, L, D = h.shape
    count = L // tl if count is None else count
    return pl.pallas_call(
        functools.partial(_projcols_body, tl=tl, norm=norm),
        grid=(G, count),
        in_specs=[pl.BlockSpec((None, tl, D), lambda g, i: (g, first + i, 0)),
                  pl.BlockSpec((None, DIL_WIDTH, D), lambda g, i: (layer, seg, 0)),
                  pl.BlockSpec((DIL_WIDTH, LANES), lambda g, i: (0, 0)),
                  pl.BlockSpec((MXU_WIDTH, MXU_WIDTH), lambda g, i: (0, 0))],
        out_specs=pl.BlockSpec((None, DIL_WIDTH, tl), lambda g, i: (g, 0, i)),
        out_shape=jax.ShapeDtypeStruct((G, DIL_WIDTH, count * tl), out_dtype),
        compiler_params=_params("parallel", "parallel"),
        name="proj_cols",
    )(h, wt, rowgain, bd)


def _mmres_body(y_ref, w_ref, x_ref, o_ref):
    o_ref[...] = x_ref[...] + jnp.dot(y_ref[...], w_ref[...], preferred_element_type=F32)


def matmul_residual(y, w, layer, x, tm, tn):
    M, K = y.shape
    N = w.shape[2]
    return pl.pallas_call(
        _mmres_body,
        grid=(M // tm, N // tn),
        in_specs=[pl.BlockSpec((tm, K), lambda i, j: (i, 0)),
                  pl.BlockSpec((None, K, tn), lambda i, j: (layer, 0, j)),
                  pl.BlockSpec((tm, tn), lambda i, j: (i, j))],
        out_specs=pl.BlockSpec((tm, tn), lambda i, j: (i, j)),
        out_shape=jax.ShapeDtypeStruct((M, N), F32),
        compiler_params=_params("parallel", "parallel"),
        name="matmul_residual",
    )(y, w, x)


def _gate_body(x_ref, g_ref, wgz_ref, wg2_ref, b_ref, o_ref):
    h = _rms(x_ref[...], g_ref[...]).astype(BF16)
    gz = jnp.dot(h, wgz_ref[...], preferred_element_type=F32)
    z = jnp.dot(gz.astype(BF16), wg2_ref[...], preferred_element_type=F32) + b_ref[...]
    o_ref[...] = (jnp.minimum(z, 0.0) - jnp.log(1.0 + jnp.exp(-jnp.abs(z)))) * (1.0 / GLA_GATE_NORM)


def gla_gate(x, g, wgz, wg2, b, tm):
    M, D = x.shape
    return pl.pallas_call(
        _gate_body,
        grid=(M // tm,),
        in_specs=[pl.BlockSpec((tm, D), lambda i: (i, 0)),
                  pl.BlockSpec((1, D), lambda i: (0, 0)),
                  pl.BlockSpec((D, LANES), lambda i: (0, 0)),
                  pl.BlockSpec((LANES, GLA_KD), lambda i: (0, 0)),
                  pl.BlockSpec((1, GLA_KD), lambda i: (0, 0))],
        out_specs=pl.BlockSpec((tm, GLA_KD), lambda i: (i, 0)),
        out_shape=jax.ShapeDtypeStruct((M, GLA_KD), F32),
        compiler_params=_params("parallel"),
        name="gla_gate",
    )(x, g, wgz, wg2, b)


def _gla_body(q_ref, k_ref, v_ref, r_ref, gl_ref, s0_ref, gn_ref, tri_ref, y_ref, st_ref, S_ref, *, TB, CH):
    c = pl.program_id(1)

    @pl.when(c == 0)
    def _():
        S_ref[...] = s0_ref[...]

    gl = gl_ref[...]
    g1 = gl.astype(BF16)
    r1 = gl - g1.astype(F32)
    g2 = r1.astype(BF16)
    g3 = (r1 - g2.astype(F32)).astype(BF16)
    tri = tri_ref[...]
    bfull = (jnp.dot(tri, g1, preferred_element_type=F32)
             + jnp.dot(tri, g2, preferred_element_type=F32)
             + jnp.dot(tri, g3, preferred_element_type=F32))

    H = GLA_HEADS
    R = H * CH
    own = (lax.broadcasted_iota(jnp.int32, (R, GLA_KD), 0) // CH
           == lax.broadcasted_iota(jnp.int32, (R, GLA_KD), 1) // GLA_DK)
    arow = lax.broadcasted_iota(jnp.int32, (R, R), 0)
    acol = lax.broadcasted_iota(jnp.int32, (R, R), 1)
    amask = (arow // CH == acol // CH) & (arow % CH >= acol % CH)
    heads = lambda a: jnp.concatenate([a] * H, axis=0)
    split = lambda ref, rs: jnp.concatenate([ref[rs, h * GLA_DV:(h + 1) * GLA_DV] for h in range(H)], axis=0)
    mid = CH // 2
    gn = gn_ref[...]
    S = S_ref[...]
    for sc in range(TB // CH):
        rs = slice(sc * CH, (sc + 1) * CH)
        b = bfull[rs]
        if sc > 0:
            b = b - bfull[sc * CH - 1:sc * CH]
        ref = b[mid:mid + 1]
        blast = b[CH - 1:CH]
        qf = q_ref[rs, :].astype(F32) * (GLA_DK ** -0.5)
        kf = k_ref[rs, :].astype(F32)
        qe = (qf * jnp.exp(b - ref)).astype(BF16)
        ke = jnp.where(own, heads(kf * jnp.exp(ref - b)), 0.0).astype(BF16)
        kd = jnp.where(own, heads(kf * jnp.exp(blast - b)), 0.0).astype(BF16)
        qb = jnp.where(own, heads(qf * jnp.exp(b)), 0.0).astype(BF16)
        vst = split(v_ref, rs)
        a = heads(lax.dot_general(qe, ke, _NT, preferred_element_type=F32))
        a = jnp.where(amask, a, 0.0).astype(BF16)
        o = (jnp.dot(a, vst, preferred_element_type=F32)
             + lax.dot_general(qb, S.astype(BF16), _NT, preferred_element_type=F32))
        S = jnp.exp(blast) * S + lax.dot_general(vst, kd, _TN, preferred_element_type=F32)
        y = (_rms(o, gn) * _silu(split(r_ref, rs).astype(F32))).astype(y_ref.dtype)
        for h in range(H):
            y_ref[rs, h * GLA_DV:(h + 1) * GLA_DV] = y[h * CH:(h + 1) * CH]
    S_ref[...] = S

    @pl.when(c == pl.num_programs(1) - 1)
    def _():
        st_ref[...] = S_ref[...]


def _pack_state(s):
    return s.transpose(0, 3, 1, 2).reshape(s.shape[0], GLA_DV, GLA_HEADS * GLA_DK)


def _unpack_state(s):
    return s.reshape(s.shape[0], GLA_DV, GLA_HEADS, GLA_DK).transpose(0, 2, 3, 1)


def gla_scan(p, glog, s0t, gn, TB, CH):
    B, T, _ = p.shape
    sblk = (None, GLA_DV, GLA_HEADS * GLA_DK)
    tri = jnp.asarray(np.tril(np.ones((TB, TB), np.float32)), BF16)
    return pl.pallas_call(
        functools.partial(_gla_body, TB=TB, CH=CH),
        grid=(B, T // TB),
        in_specs=[pl.BlockSpec((None, TB, GLA_KD), lambda b, c: (b, c, 0)),
                  pl.BlockSpec((None, TB, GLA_KD), lambda b, c: (b, c, 1)),
                  pl.BlockSpec((None, TB, GLA_VD), lambda b, c: (b, c, 1)),
                  pl.BlockSpec((None, TB, GLA_VD), lambda b, c: (b, c, 2)),
                  pl.BlockSpec((None, TB, GLA_KD), lambda b, c: (b, c, 0)),
                  pl.BlockSpec(sblk, lambda b, c: (b, 0, 0)),
                  pl.BlockSpec((1, GLA_DV), lambda b, c: (0, 0)),
                  pl.BlockSpec((TB, TB), lambda b, c: (0, 0))],
        out_specs=[pl.BlockSpec((None, TB, GLA_VD), lambda b, c: (b, c, 0)),
                   pl.BlockSpec(sblk, lambda b, c: (b, 0, 0))],
        out_shape=[jax.ShapeDtypeStruct((B, T, GLA_VD), BF16),
                   jax.ShapeDtypeStruct((B,) + sblk[1:], F32)],
        scratch_shapes=[pltpu.VMEM(sblk[1:], F32)],
        compiler_params=_params("parallel", "arbitrary"),
        name="gla_scan",
    )(p, p, p, p, glog, s0t, gn, tri)


def _attn_body(q_ref, kp_ref, kc_ref, vp_ref, vc_ref, bias_ref, o_ref, lse_ref):
    var = jnp.where(pl.program_id(2) == 0, 0, 1)
    lane = lax.broadcasted_iota(jnp.int32, (DIL_BLOCK, LANES), 1)
    lo = lane < DIL_HD
    lse_acc = jnp.zeros((DIL_BLOCK, LANES), F32)
    zero = jnp.zeros((), BF16)
    def scores(hp):
        cs = slice(hp * LANES, (hp + 1) * LANES)
        q2 = q_ref[:, cs]
        qab = jnp.concatenate([jnp.where(lo, q2, zero), jnp.where(lo, zero, q2)], axis=0)
        k2t = jnp.concatenate([kp_ref[cs, :], kc_ref[cs, :]], axis=1)
        return jnp.dot(qab, k2t, preferred_element_type=F32)

    ahead = scores(0)
    for hp in range(DIL_HEADS // 2):
        cs = slice(hp * LANES, (hp + 1) * LANES)
        s = ahead + bias_ref[var, hp]
        if hp + 1 < DIL_HEADS // 2:
            ahead = scores(hp + 1)
        v2 = jnp.concatenate([vp_ref[:, cs], vc_ref[:, cs]], axis=0)
        m = jnp.max(s, axis=-1, keepdims=True)
        p = jnp.exp(s - m)
        l = jnp.sum(p, axis=-1, keepdims=True)
        o = jnp.dot(p.astype(BF16), v2, preferred_element_type=F32) * (1.0 / l)
        o_ref[:, cs] = jnp.where(lo, o[:DIL_BLOCK], o[DIL_BLOCK:]).astype(o_ref.dtype)
        lse = m + jnp.log(l)
        lse_acc = jnp.where(lane == 2 * hp, lse[:DIL_BLOCK], lse_acc)
        lse_acc = jnp.where(lane == 2 * hp + 1, lse[DIL_BLOCK:], lse_acc)
    lse_ref[...] = lse_acc


def dil_attention(q, kt, v, bias):
    B, d, L, _ = q.shape
    nb = L // DIL_BLOCK
    blk = (None, None, DIL_BLOCK, DIL_WIDTH)
    tblk = (None, None, DIL_WIDTH, DIL_BLOCK)
    prev = lambda n: jnp.maximum(n - 1, 0)
    return pl.pallas_call(
        _attn_body,
        grid=(B, d, nb),
        in_specs=[pl.BlockSpec(blk, lambda b, r, n: (b, r, n, 0)),
                  pl.BlockSpec(tblk, lambda b, r, n: (b, r, 0, prev(n))),
                  pl.BlockSpec(tblk, lambda b, r, n: (b, r, 0, n)),
                  pl.BlockSpec(blk, lambda b, r, n: (b, r, prev(n), 0)),
                  pl.BlockSpec(blk, lambda b, r, n: (b, r, n, 0)),
                  pl.BlockSpec((2, DIL_HEADS // 2, 2 * DIL_BLOCK, 2 * DIL_BLOCK), lambda b, r, n: (0, 0, 0, 0))],
        out_specs=[pl.BlockSpec(blk, lambda b, r, n: (b, r, n, 0)),
                   pl.BlockSpec((None, None, DIL_BLOCK, LANES), lambda b, r, n: (b, r, n, 0))],
        out_shape=[jax.ShapeDtypeStruct((B, d, L, DIL_WIDTH), BF16),
                   jax.ShapeDtypeStruct((B, d, L, LANES), F32)],
        compiler_params=_params("parallel", "parallel", "arbitrary"),
        name="dil_attention",
    )(q, kt, kt, v, v, bias)


def _dilout_body(o0_ref, o1_ref, o2_ref, l0_ref, l1_ref, l2_ref, x_ref, w_ref, e_ref, out_ref):
    l0, l1, l2 = l0_ref[...], l1_ref[...], l2_ref[...]
    mx = jnp.maximum(jnp.maximum(l0, l1), l2)
    e0, e1, e2 = jnp.exp(l0 - mx), jnp.exp(l1 - mx), jnp.exp(l2 - mx)
    den = e0 + e1 + e2
    ex = e_ref[...]

    def expand(w):
        w1 = w.astype(BF16)
        w2 = (w - w1.astype(F32)).astype(BF16)
        return (jnp.dot(w1, ex, preferred_element_type=F32) + jnp.dot(w2, ex, preferred_element_type=F32))

    o = (expand(e0 / den) * o0_ref[...] + expand(e1 / den) * o1_ref[...] + expand(e2 / den) * o2_ref[...])
    out_ref[...] = x_ref[...] + jnp.dot(o.astype(BF16), w_ref[...], preferred_element_type=F32)


def dil_out(os, ls, x, w, layer, expand, tm):
    M = x.shape[0]
    ospec = pl.BlockSpec((tm, DIL_WIDTH), lambda i: (i, 0))
    lspec = pl.BlockSpec((tm, LANES), lambda i: (i, 0))
    return pl.pallas_call(
        _dilout_body,
        grid=(M // tm,),
        in_specs=[ospec, ospec, ospec, lspec, lspec, lspec, ospec,
                  pl.BlockSpec((None, DIL_WIDTH, D_MODEL), lambda i: (layer, 0, 0)),
                  pl.BlockSpec((LANES, DIL_WIDTH), lambda i: (0, 0))],
        out_specs=pl.BlockSpec((tm, D_MODEL), lambda i: (i, 0)),
        out_shape=jax.ShapeDtypeStruct((M, D_MODEL), F32),
        compiler_params=_params("parallel"),
        name="dil_out",
    )(*os, *ls, x, w, expand)


S_CHUNK = 512
S_NEWPAD = LANES


def _sattn_body(qkv_ref, k0_ref, v0_ref, k1_ref, v1_ref, k2_ref, v2_ref, b0_ref, b1_ref, b2_ref, bn_ref,
                o_ref, m_ref, l_ref, acc_ref, *, seq):
    c = pl.program_id(1)
    R = seq * DIL_HEADS
    rowh = lax.broadcasted_iota(jnp.int32, (R, DIL_WIDTH), 0) % DIL_HEADS
    colh = lax.broadcasted_iota(jnp.int32, (R, DIL_WIDTH), 1) // DIL_HD
    hmask = rowh == colh

    def seg(g, part):
        return qkv_ref[:, (3 * g + part) * DIL_WIDTH:(3 * g + part + 1) * DIL_WIDTH]

    def qbd(g):
        q = seg(g, 0)
        qrep = jnp.concatenate([jnp.broadcast_to(q[s:s + 1], (DIL_HEADS, DIL_WIDTH)) for s in range(seq)], axis=0)
        return jnp.where(hmask, qrep, 0.0).astype(BF16)

    def update(parts):
        m_old = m_ref[...]
        m_new = m_old
        for s, _ in parts:
            m_new = jnp.maximum(m_new, jnp.max(s, axis=-1, keepdims=True))
        alpha = jnp.exp(m_old - m_new)
        l = alpha * l_ref[...]
        acc = alpha * acc_ref[...]
        for s, pv_fn in parts:
            p = jnp.exp(s - m_new)
            l = l + jnp.sum(p, axis=-1, keepdims=True)
            acc = acc + pv_fn(p.astype(BF16))
        m_ref[...] = m_new
        l_ref[...] = l
        acc_ref[...] = acc

    def cache_segment(g, kt_ref, vt_ref, b_ref):
        s = jnp.dot(qbd(g), kt_ref[...].astype(BF16), preferred_element_type=F32) + b_ref[...]
        return s, lambda p: lax.dot_general(p, vt_ref[...].astype(BF16), _NT, preferred_element_type=F32)

    def new_segment(g):
        pad = jnp.zeros((S_NEWPAD - seq, DIL_WIDTH), F32)
        kn = jnp.concatenate([seg(g, 1), pad], axis=0).astype(BF16)
        vn = jnp.concatenate([seg(g, 2), pad], axis=0).astype(BF16)
        s = lax.dot_general(qbd(g), kn, _NT, preferred_element_type=F32) + bn_ref[g]
        return s, lambda p: jnp.dot(p, vn, preferred_element_type=F32)

    @pl.when(c == 0)
    def _():
        m_ref[...] = jnp.full(m_ref.shape, 2 * NEG, F32)
        l_ref[...] = jnp.zeros(l_ref.shape, F32)
        acc_ref[...] = jnp.zeros(acc_ref.shape, F32)
        update([new_segment(g) for g in range(N_GROUPS)]
               + [cache_segment(0, k0_ref, v0_ref, b0_ref), cache_segment(1, k1_ref, v1_ref, b1_ref),
                  cache_segment(2, k2_ref, v2_ref, b2_ref)])

    @pl.when(c > 0)
    def _():
        update([cache_segment(2, k2_ref, v2_ref, b2_ref)])

    @pl.when(c == pl.num_programs(1) - 1)
    def _():
        on = jnp.where(hmask, acc_ref[...] / l_ref[...], 0.0).astype(BF16)
        srow = lax.broadcasted_iota(jnp.int32, (8, R), 0)
        scol = lax.broadcasted_iota(jnp.int32, (8, R), 1) // DIL_HEADS
        sel = jnp.where(srow == scol, 1.0, 0.0).astype(BF16)
        o_ref[...] = jnp.dot(sel, on, preferred_element_type=F32)[:seq]


def dil_sample_attention(li, qkv, kts, vts, bbs, bn):
    DB, S, _ = qkv.shape
    R = DIL_HEADS * S
    W2 = kts[2].shape[-1]
    in_specs = [pl.BlockSpec((None, S, qkv.shape[-1]), lambda b, c: (b, 0, 0))]
    args = [qkv]
    for g in range(2):
        for a in (kts[g], vts[g]):
            in_specs.append(pl.BlockSpec((None, None, DIL_WIDTH, a.shape[-1]), lambda b, c: (li, b, 0, 0)))
            args.append(a)
    for a in (kts[2], vts[2]):
        in_specs.append(pl.BlockSpec((None, None, DIL_WIDTH, S_CHUNK), lambda b, c: (li, b, 0, c)))
        args.append(a)
    for g in range(2):
        in_specs.append(pl.BlockSpec(bbs[g].shape, lambda b, c: (0, 0)))
        args.append(bbs[g])
    in_specs.append(pl.BlockSpec((R, S_CHUNK), lambda b, c: (0, c)))
    args.append(bbs[2])
    in_specs.append(pl.BlockSpec(bn.shape, lambda b, c: (0, 0, 0)))
    args.append(bn)
    return pl.pallas_call(
        functools.partial(_sattn_body, seq=S),
        grid=(DB, W2 // S_CHUNK),
        in_specs=in_specs,
        out_specs=pl.BlockSpec((None, S, DIL_WIDTH), lambda b, c: (b, 0, 0)),
        out_shape=jax.ShapeDtypeStruct((DB, S, DIL_WIDTH), F32),
        scratch_shapes=[pltpu.VMEM((R, 1), F32), pltpu.VMEM((R, 1), F32), pltpu.VMEM((R, DIL_WIDTH), F32)],
        compiler_params=_params("parallel", "arbitrary"),
        name="dil_sample_attention",
    )(*args)


FFN_CHUNK = MXU_WIDTH
HALO = 16


FFN_ROWBLOCK = 128


def _causal_conv(ext, cw, cb):
    u1 = pltpu.roll(ext, 1, axis=0)[HALO:]
    u2 = pltpu.roll(ext, 2, axis=0)[HALO:]
    return cb + (cw[0:1] * u2 + cw[1:2] * u1 + cw[2:3] * ext[HALO:])


def _ffnup_body(x_ref, xh_ref, g_ref, wg_ref, wv_ref, cw_ref, cb_ref, act_ref, tail_ref, h_ref, *, tm):
    first = pl.program_id(1) == 0
    hh = _rms(xh_ref[...], g_ref[...])
    h_ref[0:HALO] = jnp.where(first, 0.0, hh).astype(BF16)
    h_ref[HALO:HALO + tm] = _rms(x_ref[...], g_ref[...]).astype(BF16)
    for c in range(D_FF // FFN_CHUNK):
        cols = (slice(c * FFN_CHUNK, (c + 1) * FFN_CHUNK),
                slice(D_FF + c * FFN_CHUNK, D_FF + (c + 1) * FFN_CHUNK))
        tails = [None, None]
        for a in range(0, tm, FFN_ROWBLOCK):
            conv = []
            for idx, w_ref in enumerate((wg_ref, wv_ref)):
                if a == 0:
                    ext = jnp.dot(h_ref[0:HALO + FFN_ROWBLOCK], w_ref[:, cols[0]], preferred_element_type=F32)
                else:
                    u = jnp.dot(h_ref[HALO + a:HALO + a + FFN_ROWBLOCK], w_ref[:, cols[0]],
                                preferred_element_type=F32)
                    ext = jnp.concatenate([tails[idx], u], axis=0)
                tails[idx] = ext[FFN_ROWBLOCK:]
                conv.append(_causal_conv(ext, cw_ref[:, cols[idx]], cb_ref[:, cols[idx]]))
            act_ref[a:a + FFN_ROWBLOCK, cols[0]] = (_silu(conv[0]) * conv[1]).astype(act_ref.dtype)
        for idx in range(2):
            tail_ref[:, cols[idx]] = tails[idx][HALO - 8:]


def ffn_up_prompt(x, g, w, layer, cw, cb, tm):
    B, T, D = x.shape
    hblk = tm // HALO
    return pl.pallas_call(
        functools.partial(_ffnup_body, tm=tm),
        grid=(B, T // tm),
        in_specs=[pl.BlockSpec((None, tm, D), lambda b, i: (b, i, 0)),
                  pl.BlockSpec((None, HALO, D), lambda b, i: (b, jnp.maximum(i * hblk - 1, 0), 0)),
                  pl.BlockSpec((1, D), lambda b, i: (0, 0)),
                  pl.BlockSpec((None, D, D_FF), lambda b, i: (layer, 0, 0)),
                  pl.BlockSpec((None, D, D_FF), lambda b, i: (layer, 0, 1)),
                  pl.BlockSpec((CONV_WIDTH, 2 * D_FF), lambda b, i: (0, 0)),
                  pl.BlockSpec((1, 2 * D_FF), lambda b, i: (0, 0))],
        out_specs=[pl.BlockSpec((None, tm, D_FF), lambda b, i: (b, i, 0)),
                   pl.BlockSpec((None, 8, 2 * D_FF), lambda b, i: (b, 0, 0))],
        out_shape=[jax.ShapeDtypeStruct((B, T, D_FF), BF16),
                   jax.ShapeDtypeStruct((B, 8, 2 * D_FF), F32)],
        scratch_shapes=[pltpu.VMEM((HALO + tm, D), BF16)],
        compiler_params=_params("parallel", "arbitrary"),
        name="ffn_up_prompt",
    )(x, x, g, w, w, cw, cb)


def _ffnup_s_body(x_ref, g_ref, wg_ref, wv_ref, cw_ref, cb_ref, c1_ref, c2_ref, act_ref, u_ref, u_scr, *, rows, seq):
    h = _rms(x_ref[...], g_ref[...]).astype(BF16)
    pos = lax.broadcasted_iota(jnp.int32, (rows, FFN_CHUNK), 0) % seq
    u_scr[:, 0:HALO] = jnp.zeros((2, HALO, FFN_CHUNK), F32)
    for c in range(D_FF // FFN_CHUNK):
        acts = []
        for idx, off in enumerate((0, D_FF)):
            cs = slice(off + c * FFN_CHUNK, off + (c + 1) * FFN_CHUNK)
            w_ref = wg_ref if idx == 0 else wv_ref
            u = jnp.dot(h, w_ref[:, c * FFN_CHUNK:(c + 1) * FFN_CHUNK], preferred_element_type=F32)
            u_scr[idx, HALO:HALO + rows] = u
            u1 = jnp.where(pos >= 1, u_scr[idx, HALO - 1:HALO - 1 + rows], c1_ref[:, cs])
            u2 = jnp.where(pos >= 2, u_scr[idx, HALO - 2:HALO - 2 + rows], c2_ref[:, cs])
            acts.append(cb_ref[:, cs] + (cw_ref[0:1, cs] * u2 + cw_ref[1:2, cs] * u1 + cw_ref[2:3, cs] * u))
            u_ref[:, cs] = u
        act_ref[:, c * FFN_CHUNK:(c + 1) * FFN_CHUNK] = (_silu(acts[0]) * acts[1]).astype(act_ref.dtype)


def ffn_up_sample(x, g, w, layer, cw, cb, carry1, carry2, seq):
    M, D = x.shape
    full = lambda shape: pl.BlockSpec(shape, lambda i: tuple(0 for _ in shape))
    return pl.pallas_call(
        functools.partial(_ffnup_s_body, rows=M, seq=seq),
        grid=(1,),
        in_specs=[full((M, D)), full((1, D)),
                  pl.BlockSpec((None, D, D_FF), lambda i: (layer, 0, 0)),
                  pl.BlockSpec((None, D, D_FF), lambda i: (layer, 0, 1)),
                  full((CONV_WIDTH, 2 * D_FF)), full((1, 2 * D_FF)),
                  full((M, 2 * D_FF)), full((M, 2 * D_FF))],
        out_specs=[full((M, D_FF)), full((M, 2 * D_FF))],
        out_shape=[jax.ShapeDtypeStruct((M, D_FF), BF16),
                   jax.ShapeDtypeStruct((M, 2 * D_FF), F32)],
        scratch_shapes=[pltpu.VMEM((2, HALO + M, FFN_CHUNK), F32)],
        compiler_params=_params("arbitrary"),
        name="ffn_up_sample",
    )(x, g, w, w, cw, cb, carry1, carry2)


def _rel_bucket(dist):
    max_exact = NUM_BUCKETS // 2
    df = jnp.maximum(dist, 1).astype(F32)
    large = max_exact + (jnp.log(df / max_exact) / math.log(MAX_DISTANCE / max_exact)
                         * (NUM_BUCKETS - max_exact)).astype(jnp.int32)
    large = jnp.minimum(large, NUM_BUCKETS - 1)
    return jnp.where(dist < max_exact, dist, large)


def _step_table(rel_bias, g, d):
    J = DIL_GROUPS[g][0] // d
    tab = rel_bias[:, g * DIL_HEADS:(g + 1) * DIL_HEADS].astype(F32)
    return tab[_rel_bucket(jnp.arange(J + 1) * d)]


def _lookup(table, idx):
    onehot = jax.nn.one_hot(jnp.asarray(idx, jnp.int32), table.shape[0], dtype=F32)
    return jnp.einsum('...j,jh->...h', onehot, table, precision=lax.Precision.HIGHEST)


def _prompt_bias(rel_bias, g, d):
    J = DIL_GROUPS[g][0] // d
    qi = np.arange(DIL_BLOCK)[:, None]
    ki = np.arange(2 * DIL_BLOCK)[None, :]
    rel = qi + DIL_BLOCK - ki
    bias = jnp.moveaxis(_lookup(_step_table(rel_bias, g, d), np.clip(rel, 0, J)), -1, 0)
    out = []
    for first in (True, False):
        valid = (rel >= 0) & (rel <= J) & ((ki >= DIL_BLOCK) | (not first))
        b = jnp.where(jnp.asarray(valid)[None], bias, NEG)
        out.append(b.reshape(DIL_HEADS // 2, 2 * DIL_BLOCK, 2 * DIL_BLOCK))
    return jnp.stack(out)


def _sample_bias(rel_bias, g, d, seq):
    W = DIL_GROUPS[g][0]
    J = W // d
    table = _step_table(rel_bias, g, d)
    s = np.arange(seq)[:, None]
    dist = W + s - np.arange(W)[None, :]
    valid = (dist >= 0) & (dist <= W) & (dist % d == 0)
    bval = jnp.moveaxis(_lookup(table, np.where(valid, dist // d, 0)), -1, 1)
    bbuf = jnp.where(jnp.asarray(valid)[:, None, :], bval, NEG).reshape(seq * DIL_HEADS, W)
    dn = s - np.arange(S_NEWPAD)[None, :]
    vn = (dn >= 0) & (dn % d == 0) & (dn // d <= J) & (np.arange(S_NEWPAD)[None, :] < seq)
    bnew = jnp.moveaxis(_lookup(table, np.where(vn, dn // d, 0)), -1, 1)
    bnew = jnp.where(jnp.asarray(vn)[:, None, :], bnew, NEG).reshape(seq * DIL_HEADS, S_NEWPAD)
    return bbuf, bnew


def _colgain(q_gain, k_gain, ngroups):
    seg = jnp.concatenate([jnp.tile(q_gain.astype(F32) * DIL_SCALE, DIL_HEADS),
                           jnp.tile(k_gain.astype(F32), DIL_HEADS),
                           jnp.ones((DIL_WIDTH,), F32)])
    return jnp.tile(seg, ngroups)[None, :]


def kernel(x_prompt, x_sample, state_gla, cache_k_g0, cache_v_g0, cache_k_g1, cache_v_g1, cache_k_g2, cache_v_g2,
           state_ffn_conv, rel_bias, norm_mix, norm_ffn, gla_w_in, gla_w_gate2, gla_b_gate, gla_norm, gla_w_out,
           dil_w_in, dil_q_norm, dil_k_norm, dil_w_out, ffn_w_up, ffn_conv_w, ffn_conv_b, ffn_w_down):
    B, T, D = x_prompt.shape
    DB, S, _ = x_sample.shape
    MP, MS = B * T, DB * S
    assert all(c.shape[2] == w for c, (w, _) in zip((cache_k_g0, cache_k_g1, cache_k_g2), DIL_GROUPS))

    xp = x_prompt.reshape(MP, D)
    xs = x_sample.reshape(MS, D)
    fmajor = lambda c: c.transpose(0, 1, 3, 4, 2).reshape(c.shape[0], c.shape[1], DIL_WIDTH, c.shape[2])
    k_caches = tuple(fmajor(c) for c in (cache_k_g0, cache_k_g1, cache_k_g2))
    v_caches = tuple(fmajor(c) for c in (cache_v_g0, cache_v_g1, cache_v_g2))

    blockdiag = jnp.asarray(np.kron(np.eye(MXU_WIDTH // DIL_HD), np.ones((DIL_HD, DIL_HD))), BF16)
    expand = jnp.asarray(np.kron(np.eye(LANES, DIL_HEADS, dtype=np.float32).reshape(LANES, DIL_HEADS),
                                 np.ones((1, DIL_HD), np.float32)), BF16)
    p_bias = [_prompt_bias(rel_bias, g, d) for g, (_, d) in enumerate(DIL_GROUPS)]
    s_bias = [_sample_bias(rel_bias, g, d, S) for g, (_, d) in enumerate(DIL_GROUPS)]
    s_bias_buf = [b for b, _ in s_bias]
    s_bias_new = jnp.stack([b for _, b in s_bias])

    gla_p, gla_s = [], []
    kp = [[] for _ in DIL_GROUPS]
    vp = [[] for _ in DIL_GROUPS]
    kq = [[] for _ in DIL_GROUPS]
    vq = [[] for _ in DIL_GROUPS]
    conv_p, conv_s = [], []
    SPAD = 16

    gla_w_in_b = to_bf16(gla_w_in, 256)
    gla_w_out_b = to_bf16(gla_w_out, 256)
    dil_w_in_b = to_bf16(dil_w_in, 128)
    dil_w_out_b = to_bf16(dil_w_out, 256)
    ffn_w_up_b = to_bf16(ffn_w_up, 256)
    ffn_w_down_b = to_bf16(ffn_w_down, 256)
    dil_w_in_t = jnp.swapaxes(dil_w_in_b, 1, 2)
    n_main = 2 * GLA_KD + 2 * GLA_VD

    for i in range(DEPTH):
        li = i // N_MIXERS
        gmix = norm_mix[i][None, :].astype(F32)
        if i % N_MIXERS == 0:
            wgz = jnp.pad(gla_w_in[li][:, n_main:], ((0, 0), (0, LANES - GLA_GATE_RANK))).astype(BF16)
            wg2 = jnp.pad(gla_w_gate2[li], ((0, LANES - GLA_GATE_RANK), (0, 0))).astype(BF16)
            bg = gla_b_gate[li][None, :].astype(F32)
            gn = gla_norm[li][None, :].astype(F32)
            pp = norm_matmul(xp, gmix, gla_w_in_b, li, n_main, BF16, 1024, 1024).reshape(B, T, -1)
            glp = gla_gate(xp, gmix, wgz, wg2, bg, 1024).reshape(B, T, GLA_KD)
            s0 = jnp.zeros((B, GLA_DV, GLA_HEADS * GLA_DK), F32)
            yp, stp = gla_scan(pp, glp, s0, gn, 128, GLA_CHUNK)
            xp = matmul_residual(yp.reshape(MP, GLA_VD), gla_w_out_b, li, xp, 1024, D)
            gla_p.append(_unpack_state(stp))
            ps = norm_matmul(xs, gmix, gla_w_in_b, li, n_main, BF16, MS, 1024).reshape(DB, S, -1)
            gls = gla_gate(xs, gmix, wgz, wg2, bg, MS).reshape(DB, S, GLA_KD)
            ps = jnp.pad(ps, ((0, 0), (0, SPAD - S), (0, 0)))
            gls = jnp.pad(gls, ((0, 0), (0, SPAD - S), (0, 0)))
            ys, sts = gla_scan(ps, gls, _pack_state(state_gla[li].astype(F32)), gn, SPAD, SPAD)
            xs = matmul_residual(ys[:, :S].reshape(MS, GLA_VD), gla_w_out_b, li, xs, MS, D)
            gla_s.append(_unpack_state(sts))
        else:
            cg = _colgain(dil_q_norm[li], dil_k_norm[li], N_GROUPS)
            kgain = jnp.broadcast_to(jnp.tile(dil_k_norm[li].astype(F32), DIL_HEADS)[:, None], (DIL_WIDTH, LANES))
            os_, ls_ = [], []
            dils = tuple(d for _, d in DIL_GROUPS)
            hs = dict(zip(dils, rms_prep(xp.reshape(B, T, D), gmix, dils, 1024)))
            for g, (W, d) in enumerate(DIL_GROUPS):
                L = T // d
                hg = hs[d].reshape(MP, D)
                q = proj_rows(hg, dil_w_in_b, li, 3 * g, cg, blockdiag, True, 2048).reshape(B, d, L, DIL_WIDTH)
                v = proj_rows(hg, dil_w_in_b, li, 3 * g + 2, cg, blockdiag, False, 2048).reshape(B, d, L, DIL_WIDTH)
                kt = proj_cols(hs[d].reshape(B * d, L, D), dil_w_in_t, li, 3 * g + 1, kgain, blockdiag, True, BF16,
                               min(L, 512)).reshape(B, d, DIL_WIDTH, L)
                o, lse = dil_attention(q, kt, v, p_bias[g])
                if d > 1:
                    o = o.transpose(0, 2, 1, 3)
                    lse = lse.transpose(0, 2, 1, 3)
                os_.append(o.reshape(MP, DIL_WIDTH))
                ls_.append(lse.reshape(MP, LANES))
                keep = min(W, T)
                tl = min(keep, 512)
                tail = dict(first=(T - keep) // tl, count=keep // tl)
                kp[g].append(proj_cols(hs[1].reshape(B, T, D), dil_w_in_t, li, 3 * g + 1, kgain, blockdiag, True, F32,
                                       tl, **tail))
                vp[g].append(proj_cols(hs[1].reshape(B, T, D), dil_w_in_t, li, 3 * g + 2, kgain, blockdiag, False, F32,
                                       tl, **tail))
            xp = dil_out(os_, ls_, xp, dil_w_out_b, li, expand, 512)
            qkvs = norm_matmul_qk(xs.reshape(1, MS, D), gmix, dil_w_in_b, li, cg, blockdiag, F32, MS, 1024)
            osamp = dil_sample_attention(li, qkvs.reshape(DB, S, -1), k_caches, v_caches, s_bias_buf, s_bias_new)
            qkvs = qkvs.reshape(DB, S, N_GROUPS, 3, DIL_HEADS, DIL_HD)
            for g in range(N_GROUPS):
                kq[g].append(qkvs[:, :, g, 1])
                vq[g].append(qkvs[:, :, g, 2])
            osamp = osamp.reshape(MS, DIL_WIDTH).astype(BF16)
            xs = matmul_residual(osamp, dil_w_out_b, li, xs, MS, D)

        gffn = norm_ffn[i][None, :].astype(F32)
        cw = ffn_conv_w[i].astype(F32)
        cb = ffn_conv_b[i][None, :].astype(F32)
        act, tail = ffn_up_prompt(xp.reshape(B, T, D), gffn, ffn_w_up_b, i, cw, cb, 512)
        conv_p.append(tail[:, 8 - (CONV_WIDTH - 1):])
        xp = matmul_residual(act.reshape(MP, D_FF), ffn_w_down_b, i, xp, 1024, D)
        buf = state_ffn_conv[i].astype(F32)
        zeros = jnp.zeros((DB, S - 1, 2 * D_FF), F32)
        carry1 = jnp.concatenate([buf[:, 1:2], zeros], axis=1).reshape(MS, 2 * D_FF)
        carry2 = jnp.concatenate([buf, zeros[:, 1:]], axis=1).reshape(MS, 2 * D_FF)
        acts, us = ffn_up_sample(xs, gffn, ffn_w_up_b, i, cw, cb, carry1, carry2, S)
        conv_s.append(us.reshape(DB, S, 2 * D_FF)[:, S - (CONV_WIDTH - 1):])
        xs = matmul_residual(acts, ffn_w_down_b, i, xs, MS, D)

    outs = [xp.reshape(B, T, D), xs.reshape(DB, S, D), jnp.stack(gla_p), jnp.stack(gla_s)]
    pmajor = lambda c: c.reshape(c.shape[0], B, DIL_HEADS, DIL_HD, c.shape[-1]).transpose(0, 1, 4, 2, 3)
    for g in range(N_GROUPS):
        outs += [pmajor(jnp.stack(kp[g])), jnp.stack(kq[g]), pmajor(jnp.stack(vp[g])), jnp.stack(vq[g])]
    outs += [jnp.stack(conv_p), jnp.stack(conv_s)]
    return tuple(outs)
```

```python
import functools
import math

import numpy as np
import jax
import jax.numpy as jnp
from jax import lax
from jax.experimental import pallas as pl
from jax.experimental.pallas import tpu as pltpu

F32 = jnp.float32
BF16 = jnp.bfloat16

D_MODEL = 1024
DEPTH = 4
N_MIXERS = 2
GLA_HEADS = 4
GLA_KD = 512
GLA_VD = 1024
GLA_DK = 128
GLA_DV = 256
GLA_GATE_RANK = 16
GLA_GATE_NORM = 16.0
GLA_CHUNK = 32
DIL_GROUPS = ((128, 1), (512, 4), (2048, 16))
N_GROUPS = 3
DIL_HEADS = 16
DIL_HD = 64
DIL_WIDTH = 1024
DIL_BLOCK = 128
DIL_SCALE = DIL_HD ** -0.5
NUM_BUCKETS = 32
MAX_DISTANCE = 2048
D_FF = 2816
CONV_WIDTH = 3
EPS = 1e-6
NEG = -1e30

LANES = 128
MXU_WIDTH = 256
VMEM_LIMIT = 48 * 1024 * 1024

_NT = (((1,), (1,)), ((), ()))
_TN = (((0,), (0,)), ((), ()))


def _params(*sem):
    return pltpu.CompilerParams(dimension_semantics=sem, vmem_limit_bytes=VMEM_LIMIT)


def _rms(x, g):
    return x * lax.rsqrt(jnp.mean(x * x, axis=-1, keepdims=True) + EPS) * g


def _silu(x):
    return x * (1.0 / (1.0 + jnp.exp(-x)))


def _cast_body(w_ref, o_ref):
    o_ref[...] = w_ref[...].astype(o_ref.dtype)


def to_bf16(w, rows):
    L, R, C = w.shape
    return pl.pallas_call(
        _cast_body,
        grid=(L, R // rows),
        in_specs=[pl.BlockSpec((None, rows, C), lambda l, i: (l, i, 0))],
        out_specs=pl.BlockSpec((None, rows, C), lambda l, i: (l, i, 0)),
        out_shape=jax.ShapeDtypeStruct((L, R, C), BF16),
        compiler_params=_params("parallel", "parallel"),
        name="to_bf16",
    )(w)


def _glain_body(x_ref, g_ref, w_ref, wgz_ref, wg2_ref, b_ref, o_ref, gl_ref, h_ref):
    @pl.when(pl.program_id(1) == 0)
    def _():
        h = _rms(x_ref[...], g_ref[...]).astype(BF16)
        h_ref[...] = h
        gz = jnp.dot(h, wgz_ref[...], preferred_element_type=F32)
        z = jnp.dot(gz.astype(BF16), wg2_ref[...], preferred_element_type=F32) + b_ref[...]
        gl_ref[...] = (jnp.minimum(z, 0.0) - jnp.log(1.0 + jnp.exp(-jnp.abs(z)))) * (1.0 / GLA_GATE_NORM)

    o_ref[...] = jnp.dot(h_ref[...], w_ref[...], preferred_element_type=F32).astype(o_ref.dtype)


def gla_in_proj(x, g, w, layer, ncols, wgz, wg2, b, tm, tn):
    M, D = x.shape
    return pl.pallas_call(
        _glain_body,
        grid=(M // tm, ncols // tn),
        in_specs=[pl.BlockSpec((tm, D), lambda i, j: (i, 0)),
                  pl.BlockSpec((1, D), lambda i, j: (0, 0)),
                  pl.BlockSpec((None, D, tn), lambda i, j: (layer, 0, j)),
                  pl.BlockSpec((D, LANES), lambda i, j: (0, 0)),
                  pl.BlockSpec((LANES, GLA_KD), lambda i, j: (0, 0)),
                  pl.BlockSpec((1, GLA_KD), lambda i, j: (0, 0))],
        out_specs=[pl.BlockSpec((tm, tn), lambda i, j: (i, j)),
                   pl.BlockSpec((tm, GLA_KD), lambda i, j: (i, 0))],
        out_shape=[jax.ShapeDtypeStruct((M, ncols), BF16),
                   jax.ShapeDtypeStruct((M, GLA_KD), F32)],
        scratch_shapes=[pltpu.VMEM((tm, D), BF16)],
        compiler_params=_params("parallel", "arbitrary"),
        name="gla_in_proj",
    )(x, g, w, wgz, wg2, b)


PERM_ROWS = 16
QK_ROWBLOCK = 256


def _nmqk_body(x_ref, g_ref, w_ref, cg_ref, bd_ref, perm_ref, o_ref, h_ref, *, tm, tn, d):
    j = pl.program_id(2)
    rows = tm // d

    @pl.when(j == 0)
    def _():
        h = _rms(x_ref[...], g_ref[...]).astype(BF16)
        if d == 1:
            h_ref[...] = h
        else:
            sub = PERM_ROWS * d
            for s in range(tm // sub):
                hs = jnp.dot(perm_ref[...], h[s * sub:(s + 1) * sub], preferred_element_type=F32).astype(BF16)
                for r in range(d):
                    dst = r * rows + PERM_ROWS * s
                    h_ref[dst:dst + PERM_ROWS, :] = hs[r * PERM_ROWS:(r + 1) * PERM_ROWS]

    is_norm = ((j * tn) // DIL_WIDTH) % 3 != 2

    rb = min(QK_ROWBLOCK, tm)

    def emit(a, cs, y):
        yb = y.astype(o_ref.dtype)
        for r in range(d):
            lo, hi = max(a, r * rows), min(a + rb, (r + 1) * rows)
            if lo < hi:
                o_ref[r, lo - r * rows:hi - r * rows, cs] = yb[lo - a:hi - a]

    def chunks(norm):
        blocks = [(slice(c * MXU_WIDTH, (c + 1) * MXU_WIDTH), a)
                  for c in range(tn // MXU_WIDTH) for a in range(0, tm, rb)]
        proj = lambda cs, a: jnp.dot(h_ref[a:a + rb, :], w_ref[:, cs], preferred_element_type=F32)
        ahead = proj(*blocks[0])
        for n, (cs, a) in enumerate(blocks):
            pc = ahead
            if n + 1 < len(blocks):
                ahead = proj(*blocks[n + 1])
            if norm:
                ss = jnp.dot((pc * pc).astype(BF16), bd_ref[...], preferred_element_type=F32)
                pc = pc * lax.rsqrt(ss * (1.0 / DIL_HD) + EPS) * cg_ref[:, cs]
            emit(a, cs, pc)

    @pl.when(is_norm)
    def _():
        chunks(True)

    @pl.when(jnp.logical_not(is_norm))
    def _():
        chunks(False)


def norm_matmul_qk(x, g, w, layer, colgain, bd, out_dtype, tm, tn, d=1, group=None):
    B, T, D = x.shape
    N = w.shape[2] if group is None else 3 * DIL_WIDTH
    j0 = 0 if group is None else group * N // tn
    sub = PERM_ROWS * d
    return pl.pallas_call(
        functools.partial(_nmqk_body, tm=tm, tn=tn, d=d),
        grid=(B, T // tm, N // tn),
        in_specs=[pl.BlockSpec((None, tm, D), lambda b, i, j: (b, i, 0)),
                  pl.BlockSpec((1, D), lambda b, i, j: (0, 0)),
                  pl.BlockSpec((None, D, tn), lambda b, i, j: (layer, 0, j + j0)),
                  pl.BlockSpec((1, tn), lambda b, i, j: (0, j + j0)),
                  pl.BlockSpec((MXU_WIDTH, MXU_WIDTH), lambda b, i, j: (0, 0)),
                  pl.BlockSpec((sub, sub), lambda b, i, j: (0, 0))],
        out_specs=pl.BlockSpec((None, d, tm // d, tn), lambda b, i, j: (b, 0, i, j)),
        out_shape=jax.ShapeDtypeStruct((B, d, T // d, N), out_dtype),
        scratch_shapes=[pltpu.VMEM((tm, D), BF16)],
        compiler_params=_params("parallel", "parallel", "arbitrary"),
        name="norm_matmul_qk",
    )(x, g, w, colgain, bd, _perm_matrix(d))


def _perm_matrix(d):
    sub = PERM_ROWS * d
    pm = np.zeros((sub, sub), np.float32)
    for r in range(d):
        for i in range(PERM_ROWS):
            pm[r * PERM_ROWS + i, i * d + r] = 1.0
    return jnp.asarray(pm, BF16)


def _prep_body(x_ref, g_ref, *refs, tm, ds):
    perm_refs, o_refs = refs[:len(ds)], refs[len(ds):]
    h = _rms(x_ref[...], g_ref[...]).astype(BF16)
    for d, perm_ref, o_ref in zip(ds, perm_refs, o_refs):
        if d == 1:
            o_ref[0] = h
            continue
        sub = PERM_ROWS * d
        for s in range(tm // sub):
            hs = jnp.dot(perm_ref[...], h[s * sub:(s + 1) * sub], preferred_element_type=F32).astype(BF16)
            for r in range(d):
                o_ref[r, PERM_ROWS * s:PERM_ROWS * (s + 1), :] = hs[r * PERM_ROWS:(r + 1) * PERM_ROWS]


def rms_prep(x, g, ds, tm):
    B, T, D = x.shape
    return pl.pallas_call(
        functools.partial(_prep_body, tm=tm, ds=ds),
        grid=(B, T // tm),
        in_specs=[pl.BlockSpec((None, tm, D), lambda b, i: (b, i, 0)),
                  pl.BlockSpec((1, D), lambda b, i: (0, 0))]
                 + [pl.BlockSpec((PERM_ROWS * d, PERM_ROWS * d), lambda b, i: (0, 0)) for d in ds],
        out_specs=[pl.BlockSpec((None, d, tm // d, D), lambda b, i: (b, 0, i, 0)) for d in ds],
        out_shape=[jax.ShapeDtypeStruct((B, d, T // d, D), BF16) for d in ds],
        compiler_params=_params("parallel", "parallel"),
        name="rms_prep",
    )(x, g, *[_perm_matrix(d) for d in ds])


def _projrows_body(h_ref, w_ref, cg_ref, bd_ref, o_ref, *, tm, norm):
    rb = min(QK_ROWBLOCK, tm)
    blocks = [(slice(c * MXU_WIDTH, (c + 1) * MXU_WIDTH), a)
              for c in range(DIL_WIDTH // MXU_WIDTH) for a in range(0, tm, rb)]
    proj = lambda cs, a: jnp.dot(h_ref[a:a + rb, :], w_ref[:, cs], preferred_element_type=F32)
    ahead = proj(*blocks[0])
    for n, (cs, a) in enumerate(blocks):
        pc = ahead
        if n + 1 < len(blocks):
            ahead = proj(*blocks[n + 1])
        if norm:
            ss = jnp.dot((pc * pc).astype(BF16), bd_ref[...], preferred_element_type=F32)
            pc = pc * lax.rsqrt(ss * (1.0 / DIL_HD) + EPS) * cg_ref[:, cs]
        o_ref[a:a + rb, cs] = pc.astype(o_ref.dtype)


def proj_rows(h, w, layer, seg, colgain, bd, norm, tm):
    M, D = h.shape
    return pl.pallas_call(
        functools.partial(_projrows_body, tm=tm, norm=norm),
        grid=(M // tm,),
        in_specs=[pl.BlockSpec((tm, D), lambda i: (i, 0)),
                  pl.BlockSpec((None, D, DIL_WIDTH), lambda i: (layer, 0, seg)),
                  pl.BlockSpec((1, DIL_WIDTH), lambda i: (0, seg)),
                  pl.BlockSpec((MXU_WIDTH, MXU_WIDTH), lambda i: (0, 0))],
        out_specs=pl.BlockSpec((tm, DIL_WIDTH), lambda i: (i, 0)),
        out_shape=jax.ShapeDtypeStruct((M, DIL_WIDTH), BF16),
        compiler_params=_params("parallel"),
        name="proj_rows",
    )(h, w, colgain, bd)


def _projcols_body(h_ref, wt_ref, cg_ref, bd_ref, o_ref, *, tl, norm, blocked):
    kt = lax.dot_general(wt_ref[...], h_ref[...], _NT, preferred_element_type=F32)
    for fb in range(DIL_WIDTH // MXU_WIDTH):
        fs = slice(fb * MXU_WIDTH, (fb + 1) * MXU_WIDTH)
        blk = kt[fs]
        if norm:
            ss = jnp.dot(bd_ref[...], (blk * blk).astype(BF16), preferred_element_type=F32)
            gain = jnp.concatenate([cg_ref[fs, :]] * (tl // LANES), axis=1)
            blk = blk * lax.rsqrt(ss * (1.0 / DIL_HD) + EPS) * gain
        blk = blk.astype(o_ref.dtype)
        if blocked:
            for jb in range(tl // DIL_BLOCK):
                o_ref[jb, fs, :] = blk[:, jb * DIL_BLOCK:(jb + 1) * DIL_BLOCK]
        else:
            o_ref[fs, :] = blk


def proj_cols(h, wt, layer, seg, rowgain, bd, norm, out_dtype, tl, first=0, count=None, blocked=False):
    G, L, D = h.shape
    count = L // tl if count is None else count
    if blocked:
        nb = tl // DIL_BLOCK
        out_spec = pl.BlockSpec((None, nb, DIL_WIDTH, DIL_BLOCK), lambda g, i: (g, i, 0, 0))
        out_shape = jax.ShapeDtypeStruct((G, count * nb, DIL_WIDTH, DIL_BLOCK), out_dtype)
    else:
        out_spec = pl.BlockSpec((None, DIL_WIDTH, tl), lambda g, i: (g, 0, i))
        out_shape = jax.ShapeDtypeStruct((G, DIL_WIDTH, count * tl), out_dtype)
    return pl.pallas_call(
        functools.partial(_projcols_body, tl=tl, norm=norm, blocked=blocked),
        grid=(G, count),
        in_specs=[pl.BlockSpec((None, tl, D), lambda g, i: (g, first + i, 0)),
                  pl.BlockSpec((None, DIL_WIDTH, D), lambda g, i: (layer, seg, 0)),
                  pl.BlockSpec((DIL_WIDTH, LANES), lambda g, i: (0, 0)),
                  pl.BlockSpec((MXU_WIDTH, MXU_WIDTH), lambda g, i: (0, 0))],
        out_specs=out_spec,
        out_shape=out_shape,
        compiler_params=_params("parallel", "parallel"),
        name="proj_cols",
    )(h, wt, rowgain, bd)


def _mmres_body(y_ref, w_ref, x_ref, o_ref):
    o_ref[...] = x_ref[...] + jnp.dot(y_ref[...], w_ref[...], preferred_element_type=F32)


def matmul_residual(y, w, layer, x, tm, tn):
    M, K = y.shape
    N = w.shape[2]
    return pl.pallas_call(
        _mmres_body,
        grid=(M // tm, N // tn),
        in_specs=[pl.BlockSpec((tm, K), lambda i, j: (i, 0)),
                  pl.BlockSpec((None, K, tn), lambda i, j: (layer, 0, j)),
                  pl.BlockSpec((tm, tn), lambda i, j: (i, j))],
        out_specs=pl.BlockSpec((tm, tn), lambda i, j: (i, j)),
        out_shape=jax.ShapeDtypeStruct((M, N), F32),
        compiler_params=_params("parallel", "parallel"),
        name="matmul_residual",
    )(y, w, x)


def _gla_body(q_ref, k_ref, v_ref, r_ref, gl_ref, s0_ref, gn_ref, tri_ref, y_ref, st_ref, S_ref, *, TB, CH):
    c = pl.program_id(1)

    @pl.when(c == 0)
    def _():
        S_ref[...] = s0_ref[...]

    gl = gl_ref[...]
    g1 = gl.astype(BF16)
    r1 = gl - g1.astype(F32)
    g2 = r1.astype(BF16)
    g3 = (r1 - g2.astype(F32)).astype(BF16)
    tri = tri_ref[...]
    bfull = (jnp.dot(tri, g1, preferred_element_type=F32)
             + jnp.dot(tri, g2, preferred_element_type=F32)
             + jnp.dot(tri, g3, preferred_element_type=F32))

    H = GLA_HEADS
    R = H * CH
    own = (lax.broadcasted_iota(jnp.int32, (R, GLA_KD), 0) // CH
           == lax.broadcasted_iota(jnp.int32, (R, GLA_KD), 1) // GLA_DK)
    arow = lax.broadcasted_iota(jnp.int32, (R, R), 0)
    acol = lax.broadcasted_iota(jnp.int32, (R, R), 1)
    amask = (arow // CH == acol // CH) & (arow % CH >= acol % CH)
    heads = lambda a: jnp.concatenate([a] * H, axis=0)
    split = lambda ref, rs: jnp.concatenate([ref[rs, h * GLA_DV:(h + 1) * GLA_DV] for h in range(H)], axis=0)
    mid = CH // 2
    gn = gn_ref[...]
    S = S_ref[...]
    for sc in range(TB // CH):
        rs = slice(sc * CH, (sc + 1) * CH)
        b = bfull[rs]
        if sc > 0:
            b = b - bfull[sc * CH - 1:sc * CH]
        ref = b[mid:mid + 1]
        blast = b[CH - 1:CH]
        qf = q_ref[rs, :].astype(F32) * (GLA_DK ** -0.5)
        kf = k_ref[rs, :].astype(F32)
        qe = (qf * jnp.exp(b - ref)).astype(BF16)
        ke = jnp.where(own, heads(kf * jnp.exp(ref - b)), 0.0).astype(BF16)
        kd = jnp.where(own, heads(kf * jnp.exp(blast - b)), 0.0).astype(BF16)
        qb = jnp.where(own, heads(qf * jnp.exp(b)), 0.0).astype(BF16)
        vst = split(v_ref, rs)
        a = heads(lax.dot_general(qe, ke, _NT, preferred_element_type=F32))
        a = jnp.where(amask, a, 0.0).astype(BF16)
        o = (jnp.dot(a, vst, preferred_element_type=F32)
             + lax.dot_general(qb, S.astype(BF16), _NT, preferred_element_type=F32))
        S = jnp.exp(blast) * S + lax.dot_general(vst, kd, _TN, preferred_element_type=F32)
        y = (_rms(o, gn) * _silu(split(r_ref, rs).astype(F32))).astype(y_ref.dtype)
        for h in range(H):
            y_ref[rs, h * GLA_DV:(h + 1) * GLA_DV] = y[h * CH:(h + 1) * CH]
    S_ref[...] = S

    @pl.when(c == pl.num_programs(1) - 1)
    def _():
        st_ref[...] = S_ref[...]


def _pack_state(s):
    return s.transpose(0, 3, 1, 2).reshape(s.shape[0], GLA_DV, GLA_HEADS * GLA_DK)


def _unpack_state(s):
    return s.reshape(s.shape[0], GLA_DV, GLA_HEADS, GLA_DK).transpose(0, 2, 3, 1)


def gla_scan(p, glog, s0t, gn, TB, CH):
    B, T, _ = p.shape
    sblk = (None, GLA_DV, GLA_HEADS * GLA_DK)
    tri = jnp.asarray(np.tril(np.ones((TB, TB), np.float32)), BF16)
    return pl.pallas_call(
        functools.partial(_gla_body, TB=TB, CH=CH),
        grid=(B, T // TB),
        in_specs=[pl.BlockSpec((None, TB, GLA_KD), lambda b, c: (b, c, 0)),
                  pl.BlockSpec((None, TB, GLA_KD), lambda b, c: (b, c, 1)),
                  pl.BlockSpec((None, TB, GLA_VD), lambda b, c: (b, c, 1)),
                  pl.BlockSpec((None, TB, GLA_VD), lambda b, c: (b, c, 2)),
                  pl.BlockSpec((None, TB, GLA_KD), lambda b, c: (b, c, 0)),
                  pl.BlockSpec(sblk, lambda b, c: (b, 0, 0)),
                  pl.BlockSpec((1, GLA_DV), lambda b, c: (0, 0)),
                  pl.BlockSpec((TB, TB), lambda b, c: (0, 0))],
        out_specs=[pl.BlockSpec((None, TB, GLA_VD), lambda b, c: (b, c, 0)),
                   pl.BlockSpec(sblk, lambda b, c: (b, 0, 0))],
        out_shape=[jax.ShapeDtypeStruct((B, T, GLA_VD), BF16),
                   jax.ShapeDtypeStruct((B,) + sblk[1:], F32)],
        scratch_shapes=[pltpu.VMEM(sblk[1:], F32)],
        compiler_params=_params("parallel", "arbitrary"),
        name="gla_scan",
    )(p, p, p, p, glog, s0t, gn, tri)


def _attn_body(q_ref, kp_ref, kc_ref, vp_ref, vc_ref, bias_ref, o_ref, lse_ref):
    var = jnp.where(pl.program_id(2) == 0, 0, 1)
    lane = lax.broadcasted_iota(jnp.int32, (DIL_BLOCK, LANES), 1)
    lo = lane < DIL_HD
    lse_acc = jnp.zeros((DIL_BLOCK, LANES), F32)
    zero = jnp.zeros((), BF16)
    def scores(hp):
        cs = slice(hp * LANES, (hp + 1) * LANES)
        q2 = q_ref[:, cs]
        qab = jnp.concatenate([jnp.where(lo, q2, zero), jnp.where(lo, zero, q2)], axis=0)
        k2t = jnp.concatenate([kp_ref[cs, :], kc_ref[cs, :]], axis=1)
        return jnp.dot(qab, k2t, preferred_element_type=F32)

    ahead = scores(0)
    for hp in range(DIL_HEADS // 2):
        cs = slice(hp * LANES, (hp + 1) * LANES)
        s = ahead + bias_ref[var, hp]
        if hp + 1 < DIL_HEADS // 2:
            ahead = scores(hp + 1)
        v2 = jnp.concatenate([vp_ref[:, cs], vc_ref[:, cs]], axis=0)
        m = jnp.max(s, axis=-1, keepdims=True)
        p = jnp.exp(s - m)
        l = jnp.sum(p, axis=-1, keepdims=True)
        o = jnp.dot(p.astype(BF16), v2, preferred_element_type=F32) * (1.0 / l)
        o_ref[:, cs] = jnp.where(lo, o[:DIL_BLOCK], o[DIL_BLOCK:]).astype(o_ref.dtype)
        lse = m + jnp.log(l)
        lse_acc = jnp.where(lane == 2 * hp, lse[:DIL_BLOCK], lse_acc)
        lse_acc = jnp.where(lane == 2 * hp + 1, lse[DIL_BLOCK:], lse_acc)
    lse_ref[...] = lse_acc


def dil_attention(q, kt, v, bias):
    B, d, L, _ = q.shape
    nb = L // DIL_BLOCK
    blk = (None, None, DIL_BLOCK, DIL_WIDTH)
    tblk = (None, None, None, DIL_WIDTH, DIL_BLOCK)
    prev = lambda n: jnp.maximum(n - 1, 0)
    return pl.pallas_call(
        _attn_body,
        grid=(B, d, nb),
        in_specs=[pl.BlockSpec(blk, lambda b, r, n: (b, r, n, 0)),
                  pl.BlockSpec(tblk, lambda b, r, n: (b, r, prev(n), 0, 0)),
                  pl.BlockSpec(tblk, lambda b, r, n: (b, r, n, 0, 0)),
                  pl.BlockSpec(blk, lambda b, r, n: (b, r, prev(n), 0)),
                  pl.BlockSpec(blk, lambda b, r, n: (b, r, n, 0)),
                  pl.BlockSpec((2, DIL_HEADS // 2, 2 * DIL_BLOCK, 2 * DIL_BLOCK), lambda b, r, n: (0, 0, 0, 0))],
        out_specs=[pl.BlockSpec(blk, lambda b, r, n: (b, r, n, 0)),
                   pl.BlockSpec((None, None, DIL_BLOCK, LANES), lambda b, r, n: (b, r, n, 0))],
        out_shape=[jax.ShapeDtypeStruct((B, d, L, DIL_WIDTH), BF16),
                   jax.ShapeDtypeStruct((B, d, L, LANES), F32)],
        compiler_params=_params("parallel", "parallel", "arbitrary"),
        name="dil_attention",
    )(q, kt, kt, v, v, bias)


def _dilout_body(o0_ref, o1_ref, o2_ref, l0_ref, l1_ref, l2_ref, x_ref, w_ref, e_ref, out_ref):
    l0, l1, l2 = l0_ref[...], l1_ref[...], l2_ref[...]
    mx = jnp.maximum(jnp.maximum(l0, l1), l2)
    e0, e1, e2 = jnp.exp(l0 - mx), jnp.exp(l1 - mx), jnp.exp(l2 - mx)
    den = e0 + e1 + e2
    ex = e_ref[...]

    def expand(w):
        w1 = w.astype(BF16)
        w2 = (w - w1.astype(F32)).astype(BF16)
        return (jnp.dot(w1, ex, preferred_element_type=F32) + jnp.dot(w2, ex, preferred_element_type=F32))

    o = (expand(e0 / den) * o0_ref[...] + expand(e1 / den) * o1_ref[...] + expand(e2 / den) * o2_ref[...])
    out_ref[...] = x_ref[...] + jnp.dot(o.astype(BF16), w_ref[...], preferred_element_type=F32)


def dil_out(os, ls, x, w, layer, expand, tm):
    M = x.shape[0]
    ospec = pl.BlockSpec((tm, DIL_WIDTH), lambda i: (i, 0))
    lspec = pl.BlockSpec((tm, LANES), lambda i: (i, 0))
    return pl.pallas_call(
        _dilout_body,
        grid=(M // tm,),
        in_specs=[ospec, ospec, ospec, lspec, lspec, lspec, ospec,
                  pl.BlockSpec((None, DIL_WIDTH, D_MODEL), lambda i: (layer, 0, 0)),
                  pl.BlockSpec((LANES, DIL_WIDTH), lambda i: (0, 0))],
        out_specs=pl.BlockSpec((tm, D_MODEL), lambda i: (i, 0)),
        out_shape=jax.ShapeDtypeStruct((M, D_MODEL), F32),
        compiler_params=_params("parallel"),
        name="dil_out",
    )(*os, *ls, x, w, expand)


S_CHUNK = 1024
S_NEWPAD = LANES


def _sattn_body(qkv_ref, k0_ref, v0_ref, k1_ref, v1_ref, k2_ref, v2_ref, b0_ref, b1_ref, b2_ref, bn_ref,
                o_ref, m_ref, l_ref, acc_ref, *, seq):
    c = pl.program_id(1)
    R = seq * DIL_HEADS
    rowh = lax.broadcasted_iota(jnp.int32, (R, DIL_WIDTH), 0) % DIL_HEADS
    colh = lax.broadcasted_iota(jnp.int32, (R, DIL_WIDTH), 1) // DIL_HD
    hmask = rowh == colh

    def seg(g, part):
        return qkv_ref[:, (3 * g + part) * DIL_WIDTH:(3 * g + part + 1) * DIL_WIDTH]

    def qbd(g):
        q = seg(g, 0)
        qrep = jnp.concatenate([jnp.broadcast_to(q[s:s + 1], (DIL_HEADS, DIL_WIDTH)) for s in range(seq)], axis=0)
        return jnp.where(hmask, qrep, 0.0).astype(BF16)

    def update(parts):
        m_old = m_ref[...]
        m_new = m_old
        for s, _ in parts:
            m_new = jnp.maximum(m_new, jnp.max(s, axis=-1, keepdims=True))
        alpha = jnp.exp(m_old - m_new)
        l = alpha * l_ref[...]
        acc = alpha * acc_ref[...]
        for s, pv_fn in parts:
            p = jnp.exp(s - m_new)
            l = l + jnp.sum(p, axis=-1, keepdims=True)
            acc = acc + pv_fn(p.astype(BF16))
        m_ref[...] = m_new
        l_ref[...] = l
        acc_ref[...] = acc

    def cache_segment(g, kt_ref, vt_ref, b_ref):
        s = jnp.dot(qbd(g), kt_ref[...].astype(BF16), preferred_element_type=F32) + b_ref[...]
        return s, lambda p: lax.dot_general(p, vt_ref[...].astype(BF16), _NT, preferred_element_type=F32)

    def new_segment(g):
        pad = jnp.zeros((S_NEWPAD - seq, DIL_WIDTH), F32)
        kn = jnp.concatenate([seg(g, 1), pad], axis=0).astype(BF16)
        vn = jnp.concatenate([seg(g, 2), pad], axis=0).astype(BF16)
        s = lax.dot_general(qbd(g), kn, _NT, preferred_element_type=F32) + bn_ref[g]
        return s, lambda p: jnp.dot(p, vn, preferred_element_type=F32)

    @pl.when(c == 0)
    def _():
        m_ref[...] = jnp.full(m_ref.shape, 2 * NEG, F32)
        l_ref[...] = jnp.zeros(l_ref.shape, F32)
        acc_ref[...] = jnp.zeros(acc_ref.shape, F32)
        update([new_segment(g) for g in range(N_GROUPS)]
               + [cache_segment(0, k0_ref, v0_ref, b0_ref), cache_segment(1, k1_ref, v1_ref, b1_ref),
                  cache_segment(2, k2_ref, v2_ref, b2_ref)])

    @pl.when(c > 0)
    def _():
        update([cache_segment(2, k2_ref, v2_ref, b2_ref)])

    @pl.when(c == pl.num_programs(1) - 1)
    def _():
        on = jnp.where(hmask, acc_ref[...] / l_ref[...], 0.0).astype(BF16)
        srow = lax.broadcasted_iota(jnp.int32, (8, R), 0)
        scol = lax.broadcasted_iota(jnp.int32, (8, R), 1) // DIL_HEADS
        sel = jnp.where(srow == scol, 1.0, 0.0).astype(BF16)
        o_ref[...] = jnp.dot(sel, on, preferred_element_type=F32)[:seq]


def dil_sample_attention(li, qkv, kts, vts, bbs, bn):
    DB, S, _ = qkv.shape
    R = DIL_HEADS * S
    W2 = kts[2].shape[-1]
    in_specs = [pl.BlockSpec((None, S, qkv.shape[-1]), lambda b, c: (b, 0, 0))]
    args = [qkv]
    for g in range(2):
        for a in (kts[g], vts[g]):
            in_specs.append(pl.BlockSpec((None, None, DIL_WIDTH, a.shape[-1]), lambda b, c: (li, b, 0, 0)))
            args.append(a)
    for a in (kts[2], vts[2]):
        in_specs.append(pl.BlockSpec((None, None, DIL_WIDTH, S_CHUNK), lambda b, c: (li, b, 0, c)))
        args.append(a)
    for g in range(2):
        in_specs.append(pl.BlockSpec(bbs[g].shape, lambda b, c: (0, 0)))
        args.append(bbs[g])
    in_specs.append(pl.BlockSpec((R, S_CHUNK), lambda b, c: (0, c)))
    args.append(bbs[2])
    in_specs.append(pl.BlockSpec(bn.shape, lambda b, c: (0, 0, 0)))
    args.append(bn)
    return pl.pallas_call(
        functools.partial(_sattn_body, seq=S),
        grid=(DB, W2 // S_CHUNK),
        in_specs=in_specs,
        out_specs=pl.BlockSpec((None, S, DIL_WIDTH), lambda b, c: (b, 0, 0)),
        out_shape=jax.ShapeDtypeStruct((DB, S, DIL_WIDTH), F32),
        scratch_shapes=[pltpu.VMEM((R, 1), F32), pltpu.VMEM((R, 1), F32), pltpu.VMEM((R, DIL_WIDTH), F32)],
        compiler_params=_params("parallel", "arbitrary"),
        name="dil_sample_attention",
    )(*args)


FFN_CHUNK = MXU_WIDTH
HALO = 16


FFN_ROWBLOCK = 128


def _causal_conv(ext, cw, cb):
    u1 = pltpu.roll(ext, 1, axis=0)[HALO:]
    u2 = pltpu.roll(ext, 2, axis=0)[HALO:]
    return cb + (cw[0:1] * u2 + cw[1:2] * u1 + cw[2:3] * ext[HALO:])


def _ffnup_body(x_ref, xh_ref, g_ref, wg_ref, wv_ref, cw_ref, cb_ref, act_ref, tail_ref, h_ref, *, tm):
    first = pl.program_id(1) == 0
    hh = _rms(xh_ref[...], g_ref[...])
    h_ref[0:HALO] = jnp.where(first, 0.0, hh).astype(BF16)
    h_ref[HALO:HALO + tm] = _rms(x_ref[...], g_ref[...]).astype(BF16)
    for c in range(D_FF // FFN_CHUNK):
        cols = (slice(c * FFN_CHUNK, (c + 1) * FFN_CHUNK),
                slice(D_FF + c * FFN_CHUNK, D_FF + (c + 1) * FFN_CHUNK))
        tails = [None, None]
        for a in range(0, tm, FFN_ROWBLOCK):
            conv = []
            for idx, w_ref in enumerate((wg_ref, wv_ref)):
                if a == 0:
                    ext = jnp.dot(h_ref[0:HALO + FFN_ROWBLOCK], w_ref[:, cols[0]], preferred_element_type=F32)
                else:
                    u = jnp.dot(h_ref[HALO + a:HALO + a + FFN_ROWBLOCK], w_ref[:, cols[0]],
                                preferred_element_type=F32)
                    ext = jnp.concatenate([tails[idx], u], axis=0)
                tails[idx] = ext[FFN_ROWBLOCK:]
                conv.append(_causal_conv(ext, cw_ref[:, cols[idx]], cb_ref[:, cols[idx]]))
            act_ref[a:a + FFN_ROWBLOCK, cols[0]] = (_silu(conv[0]) * conv[1]).astype(act_ref.dtype)
        for idx in range(2):
            tail_ref[:, cols[idx]] = tails[idx][HALO - 8:]


def ffn_up_prompt(x, g, w, layer, cw, cb, tm):
    B, T, D = x.shape
    hblk = tm // HALO
    return pl.pallas_call(
        functools.partial(_ffnup_body, tm=tm),
        grid=(B, T // tm),
        in_specs=[pl.BlockSpec((None, tm, D), lambda b, i: (b, i, 0)),
                  pl.BlockSpec((None, HALO, D), lambda b, i: (b, jnp.maximum(i * hblk - 1, 0), 0)),
                  pl.BlockSpec((1, D), lambda b, i: (0, 0)),
                  pl.BlockSpec((None, D, D_FF), lambda b, i: (layer, 0, 0)),
                  pl.BlockSpec((None, D, D_FF), lambda b, i: (layer, 0, 1)),
                  pl.BlockSpec((CONV_WIDTH, 2 * D_FF), lambda b, i: (0, 0)),
                  pl.BlockSpec((1, 2 * D_FF), lambda b, i: (0, 0))],
        out_specs=[pl.BlockSpec((None, tm, D_FF), lambda b, i: (b, i, 0)),
                   pl.BlockSpec((None, 8, 2 * D_FF), lambda b, i: (b, 0, 0))],
        out_shape=[jax.ShapeDtypeStruct((B, T, D_FF), BF16),
                   jax.ShapeDtypeStruct((B, 8, 2 * D_FF), F32)],
        scratch_shapes=[pltpu.VMEM((HALO + tm, D), BF16)],
        compiler_params=_params("parallel", "arbitrary"),
        name="ffn_up_prompt",
    )(x, x, g, w, w, cw, cb)


def _ffnup_s_body(x_ref, g_ref, wg_ref, wv_ref, cw_ref, cb_ref, c1_ref, c2_ref, act_ref, u_ref, u_scr, *, rows, seq):
    h = _rms(x_ref[...], g_ref[...]).astype(BF16)
    pos = lax.broadcasted_iota(jnp.int32, (rows, FFN_CHUNK), 0) % seq
    u_scr[:, 0:HALO] = jnp.zeros((2, HALO, FFN_CHUNK), F32)
    for c in range(D_FF // FFN_CHUNK):
        acts = []
        for idx, off in enumerate((0, D_FF)):
            cs = slice(off + c * FFN_CHUNK, off + (c + 1) * FFN_CHUNK)
            w_ref = wg_ref if idx == 0 else wv_ref
            u = jnp.dot(h, w_ref[:, c * FFN_CHUNK:(c + 1) * FFN_CHUNK], preferred_element_type=F32)
            u_scr[idx, HALO:HALO + rows] = u
            u1 = jnp.where(pos >= 1, u_scr[idx, HALO - 1:HALO - 1 + rows], c1_ref[:, cs])
            u2 = jnp.where(pos >= 2, u_scr[idx, HALO - 2:HALO - 2 + rows], c2_ref[:, cs])
            acts.append(cb_ref[:, cs] + (cw_ref[0:1, cs] * u2 + cw_ref[1:2, cs] * u1 + cw_ref[2:3, cs] * u))
            u_ref[:, cs] = u
        act_ref[:, c * FFN_CHUNK:(c + 1) * FFN_CHUNK] = (_silu(acts[0]) * acts[1]).astype(act_ref.dtype)


def ffn_up_sample(x, g, w, layer, cw, cb, carry1, carry2, seq):
    M, D = x.shape
    full = lambda shape: pl.BlockSpec(shape, lambda i: tuple(0 for _ in shape))
    return pl.pallas_call(
        functools.partial(_ffnup_s_body, rows=M, seq=seq),
        grid=(1,),
        in_specs=[full((M, D)), full((1, D)),
                  pl.BlockSpec((None, D, D_FF), lambda i: (layer, 0, 0)),
                  pl.BlockSpec((None, D, D_FF), lambda i: (layer, 0, 1)),
                  full((CONV_WIDTH, 2 * D_FF)), full((1, 2 * D_FF)),
                  full((M, 2 * D_FF)), full((M, 2 * D_FF))],
        out_specs=[full((M, D_FF)), full((M, 2 * D_FF))],
        out_shape=[jax.ShapeDtypeStruct((M, D_FF), BF16),
                   jax.ShapeDtypeStruct((M, 2 * D_FF), F32)],
        scratch_shapes=[pltpu.VMEM((2, HALO + M, FFN_CHUNK), F32)],
        compiler_params=_params("arbitrary"),
        name="ffn_up_sample",
    )(x, g, w, w, cw, cb, carry1, carry2)


def _rel_bucket(dist):
    max_exact = NUM_BUCKETS // 2
    df = jnp.maximum(dist, 1).astype(F32)
    large = max_exact + (jnp.log(df / max_exact) / math.log(MAX_DISTANCE / max_exact)
                         * (NUM_BUCKETS - max_exact)).astype(jnp.int32)
    large = jnp.minimum(large, NUM_BUCKETS - 1)
    return jnp.where(dist < max_exact, dist, large)


def _step_table(rel_bias, g, d):
    J = DIL_GROUPS[g][0] // d
    tab = rel_bias[:, g * DIL_HEADS:(g + 1) * DIL_HEADS].astype(F32)
    return tab[_rel_bucket(jnp.arange(J + 1) * d)]


def _lookup(table, idx):
    onehot = jax.nn.one_hot(jnp.asarray(idx, jnp.int32), table.shape[0], dtype=F32)
    return jnp.einsum('...j,jh->...h', onehot, table, precision=lax.Precision.HIGHEST)


def _prompt_bias(rel_bias, g, d):
    J = DIL_GROUPS[g][0] // d
    qi = np.arange(DIL_BLOCK)[:, None]
    ki = np.arange(2 * DIL_BLOCK)[None, :]
    rel = qi + DIL_BLOCK - ki
    bias = jnp.moveaxis(_lookup(_step_table(rel_bias, g, d), np.clip(rel, 0, J)), -1, 0)
    out = []
    for first in (True, False):
        valid = (rel >= 0) & (rel <= J) & ((ki >= DIL_BLOCK) | (not first))
        b = jnp.where(jnp.asarray(valid)[None], bias, NEG)
        out.append(b.reshape(DIL_HEADS // 2, 2 * DIL_BLOCK, 2 * DIL_BLOCK))
    return jnp.stack(out)


def _sample_bias(rel_bias, g, d, seq):
    W = DIL_GROUPS[g][0]
    J = W // d
    table = _step_table(rel_bias, g, d)
    s = np.arange(seq)[:, None]
    dist = W + s - np.arange(W)[None, :]
    valid = (dist >= 0) & (dist <= W) & (dist % d == 0)
    bval = jnp.moveaxis(_lookup(table, np.where(valid, dist // d, 0)), -1, 1)
    bbuf = jnp.where(jnp.asarray(valid)[:, None, :], bval, NEG).reshape(seq * DIL_HEADS, W)
    dn = s - np.arange(S_NEWPAD)[None, :]
    vn = (dn >= 0) & (dn % d == 0) & (dn // d <= J) & (np.arange(S_NEWPAD)[None, :] < seq)
    bnew = jnp.moveaxis(_lookup(table, np.where(vn, dn // d, 0)), -1, 1)
    bnew = jnp.where(jnp.asarray(vn)[:, None, :], bnew, NEG).reshape(seq * DIL_HEADS, S_NEWPAD)
    return bbuf, bnew


def _colgain(q_gain, k_gain, ngroups):
    seg = jnp.concatenate([jnp.tile(q_gain.astype(F32) * DIL_SCALE, DIL_HEADS),
                           jnp.tile(k_gain.astype(F32), DIL_HEADS),
                           jnp.ones((DIL_WIDTH,), F32)])
    return jnp.tile(seg, ngroups)[None, :]


def kernel(x_prompt, x_sample, state_gla, cache_k_g0, cache_v_g0, cache_k_g1, cache_v_g1, cache_k_g2, cache_v_g2,
           state_ffn_conv, rel_bias, norm_mix, norm_ffn, gla_w_in, gla_w_gate2, gla_b_gate, gla_norm, gla_w_out,
           dil_w_in, dil_q_norm, dil_k_norm, dil_w_out, ffn_w_up, ffn_conv_w, ffn_conv_b, ffn_w_down):
    B, T, D = x_prompt.shape
    DB, S, _ = x_sample.shape
    MP, MS = B * T, DB * S
    assert all(c.shape[2] == w for c, (w, _) in zip((cache_k_g0, cache_k_g1, cache_k_g2), DIL_GROUPS))

    xp = x_prompt.reshape(MP, D)
    xs = x_sample.reshape(MS, D)
    fmajor = lambda c: c.transpose(0, 1, 3, 4, 2).reshape(c.shape[0], c.shape[1], DIL_WIDTH, c.shape[2])
    k_caches = tuple(fmajor(c) for c in (cache_k_g0, cache_k_g1, cache_k_g2))
    v_caches = tuple(fmajor(c) for c in (cache_v_g0, cache_v_g1, cache_v_g2))

    blockdiag = jnp.asarray(np.kron(np.eye(MXU_WIDTH // DIL_HD), np.ones((DIL_HD, DIL_HD))), BF16)
    expand = jnp.asarray(np.kron(np.eye(LANES, DIL_HEADS, dtype=np.float32).reshape(LANES, DIL_HEADS),
                                 np.ones((1, DIL_HD), np.float32)), BF16)
    p_bias = [_prompt_bias(rel_bias, g, d) for g, (_, d) in enumerate(DIL_GROUPS)]
    s_bias = [_sample_bias(rel_bias, g, d, S) for g, (_, d) in enumerate(DIL_GROUPS)]
    s_bias_buf = [b for b, _ in s_bias]
    s_bias_new = jnp.stack([b for _, b in s_bias])

    gla_p, gla_s = [], []
    kp = [[] for _ in DIL_GROUPS]
    vp = [[] for _ in DIL_GROUPS]
    kq = [[] for _ in DIL_GROUPS]
    vq = [[] for _ in DIL_GROUPS]
    conv_p, conv_s = [], []
    SPAD = 16

    gla_w_in_b = to_bf16(gla_w_in, 256)
    gla_w_out_b = to_bf16(gla_w_out, 256)
    dil_w_in_b = to_bf16(dil_w_in, 128)
    dil_w_out_b = to_bf16(dil_w_out, 256)
    ffn_w_up_b = to_bf16(ffn_w_up, 256)
    ffn_w_down_b = to_bf16(ffn_w_down, 256)
    dil_w_in_t = jnp.swapaxes(dil_w_in_b, 1, 2)
    n_main = 2 * GLA_KD + 2 * GLA_VD

    for i in range(DEPTH):
        li = i // N_MIXERS
        gmix = norm_mix[i][None, :].astype(F32)
        if i % N_MIXERS == 0:
            wgz = jnp.pad(gla_w_in[li][:, n_main:], ((0, 0), (0, LANES - GLA_GATE_RANK))).astype(BF16)
            wg2 = jnp.pad(gla_w_gate2[li], ((0, LANES - GLA_GATE_RANK), (0, 0))).astype(BF16)
            bg = gla_b_gate[li][None, :].astype(F32)
            gn = gla_norm[li][None, :].astype(F32)
            pp, glp = gla_in_proj(xp, gmix, gla_w_in_b, li, n_main, wgz, wg2, bg, 1024, 1024)
            pp, glp = pp.reshape(B, T, -1), glp.reshape(B, T, GLA_KD)
            s0 = jnp.zeros((B, GLA_DV, GLA_HEADS * GLA_DK), F32)
            yp, stp = gla_scan(pp, glp, s0, gn, 128, GLA_CHUNK)
            xp = matmul_residual(yp.reshape(MP, GLA_VD), gla_w_out_b, li, xp, 1024, D)
            gla_p.append(_unpack_state(stp))
            ps, gls = gla_in_proj(xs, gmix, gla_w_in_b, li, n_main, wgz, wg2, bg, MS, 1024)
            ps, gls = ps.reshape(DB, S, -1), gls.reshape(DB, S, GLA_KD)
            ps = jnp.pad(ps, ((0, 0), (0, SPAD - S), (0, 0)))
            gls = jnp.pad(gls, ((0, 0), (0, SPAD - S), (0, 0)))
            ys, sts = gla_scan(ps, gls, _pack_state(state_gla[li].astype(F32)), gn, SPAD, SPAD)
            xs = matmul_residual(ys[:, :S].reshape(MS, GLA_VD), gla_w_out_b, li, xs, MS, D)
            gla_s.append(_unpack_state(sts))
        else:
            cg = _colgain(dil_q_norm[li], dil_k_norm[li], N_GROUPS)
            kgain = jnp.broadcast_to(jnp.tile(dil_k_norm[li].astype(F32), DIL_HEADS)[:, None], (DIL_WIDTH, LANES))
            os_, ls_ = [], []
            dils = tuple(d for _, d in DIL_GROUPS)
            hs = dict(zip(dils, rms_prep(xp.reshape(B, T, D), gmix, dils, 1024)))
            for g, (W, d) in enumerate(DIL_GROUPS):
                L = T // d
                hg = hs[d].reshape(MP, D)
                q = proj_rows(hg, dil_w_in_b, li, 3 * g, cg, blockdiag, True, 2048).reshape(B, d, L, DIL_WIDTH)
                v = proj_rows(hg, dil_w_in_b, li, 3 * g + 2, cg, blockdiag, False, 2048).reshape(B, d, L, DIL_WIDTH)
                kt = proj_cols(hs[d].reshape(1, MP, D), dil_w_in_t, li, 3 * g + 1, kgain, blockdiag, True, BF16,
                               1024, blocked=True).reshape(B, d, L // DIL_BLOCK, DIL_WIDTH, DIL_BLOCK)
                o, lse = dil_attention(q, kt, v, p_bias[g])
                if d > 1:
                    o = o.transpose(0, 2, 1, 3)
                    lse = lse.transpose(0, 2, 1, 3)
                os_.append(o.reshape(MP, DIL_WIDTH))
                ls_.append(lse.reshape(MP, LANES))
                keep = min(W, T)
                tl = min(keep, 512)
                tail = dict(first=(T - keep) // tl, count=keep // tl)
                kp[g].append(proj_cols(hs[1].reshape(B, T, D), dil_w_in_t, li, 3 * g + 1, kgain, blockdiag, True, F32,
                                       tl, **tail))
                vp[g].append(proj_cols(hs[1].reshape(B, T, D), dil_w_in_t, li, 3 * g + 2, kgain, blockdiag, False, F32,
                                       tl, **tail))
            xp = dil_out(os_, ls_, xp, dil_w_out_b, li, expand, 512)
            qkvs = norm_matmul_qk(xs.reshape(1, MS, D), gmix, dil_w_in_b, li, cg, blockdiag, F32, MS, 1024)
            osamp = dil_sample_attention(li, qkvs.reshape(DB, S, -1), k_caches, v_caches, s_bias_buf, s_bias_new)
            qkvs = qkvs.reshape(DB, S, N_GROUPS, 3, DIL_HEADS, DIL_HD)
            for g in range(N_GROUPS):
                kq[g].append(qkvs[:, :, g, 1])
                vq[g].append(qkvs[:, :, g, 2])
            osamp = osamp.reshape(MS, DIL_WIDTH).astype(BF16)
            xs = matmul_residual(osamp, dil_w_out_b, li, xs, MS, D)

        gffn = norm_ffn[i][None, :].astype(F32)
        cw = ffn_conv_w[i].astype(F32)
        cb = ffn_conv_b[i][None, :].astype(F32)
        act, tail = ffn_up_prompt(xp.reshape(B, T, D), gffn, ffn_w_up_b, i, cw, cb, 512)
        conv_p.append(tail[:, 8 - (CONV_WIDTH - 1):])
        xp = matmul_residual(act.reshape(MP, D_FF), ffn_w_down_b, i, xp, 1024, D)
        buf = state_ffn_conv[i].astype(F32)
        zeros = jnp.zeros((DB, S - 1, 2 * D_FF), F32)
        carry1 = jnp.concatenate([buf[:, 1:2], zeros], axis=1).reshape(MS, 2 * D_FF)
        carry2 = jnp.concatenate([buf, zeros[:, 1:]], axis=1).reshape(MS, 2 * D_FF)
        acts, us = ffn_up_sample(xs, gffn, ffn_w_up_b, i, cw, cb, carry1, carry2, S)
        conv_s.append(us.reshape(DB, S, 2 * D_FF)[:, S - (CONV_WIDTH - 1):])
        xs = matmul_residual(acts, ffn_w_down_b, i, xs, MS, D)

    outs = [xp.reshape(B, T, D), xs.reshape(DB, S, D), jnp.stack(gla_p), jnp.stack(gla_s)]
    pmajor = lambda c: c.reshape(c.shape[0], B, DIL_HEADS, DIL_HD, c.shape[-1]).transpose(0, 1, 4, 2, 3)
    for g in range(N_GROUPS):
        outs += [pmajor(jnp.stack(kp[g])), jnp.stack(kq[g]), pmajor(jnp.stack(vp[g])), jnp.stack(vq[g])]
    outs += [jnp.stack(conv_p), jnp.stack(conv_s)]
    return tuple(outs)
```

```python
import functools
import math

import numpy as np
import jax
import jax.numpy as jnp
from jax import lax
from jax.experimental import pallas as pl
from jax.experimental.pallas import tpu as pltpu

F32 = jnp.float32
BF16 = jnp.bfloat16

D_MODEL = 1024
DEPTH = 4
N_MIXERS = 2
GLA_HEADS = 4
GLA_KD = 512
GLA_VD = 1024
GLA_DK = 128
GLA_DV = 256
GLA_GATE_RANK = 16
GLA_GATE_NORM = 16.0
GLA_CHUNK = 32
DIL_GROUPS = ((128, 1), (512, 4), (2048, 16))
N_GROUPS = 3
DIL_HEADS = 16
DIL_HD = 64
DIL_WIDTH = 1024
DIL_BLOCK = 128
DIL_SCALE = DIL_HD ** -0.5
NUM_BUCKETS = 32
MAX_DISTANCE = 2048
D_FF = 2816
CONV_WIDTH = 3
EPS = 1e-6
NEG = -1e30

LANES = 128
MXU_WIDTH = 256
VMEM_LIMIT = 48 * 1024 * 1024

_NT = (((1,), (1,)), ((), ()))
_TN = (((0,), (0,)), ((), ()))


def _params(*sem):
    return pltpu.CompilerParams(dimension_semantics=sem, vmem_limit_bytes=VMEM_LIMIT)


def _rms(x, g):
    return x * lax.rsqrt(jnp.mean(x * x, axis=-1, keepdims=True) + EPS) * g


def _silu(x):
    return x * (1.0 / (1.0 + jnp.exp(-x)))


def _cast_body(w_ref, o_ref):
    o_ref[...] = w_ref[...].astype(o_ref.dtype)


def to_bf16(w, rows):
    L, R, C = w.shape
    return pl.pallas_call(
        _cast_body,
        grid=(L, R // rows),
        in_specs=[pl.BlockSpec((None, rows, C), lambda l, i: (l, i, 0))],
        out_specs=pl.BlockSpec((None, rows, C), lambda l, i: (l, i, 0)),
        out_shape=jax.ShapeDtypeStruct((L, R, C), BF16),
        compiler_params=_params("parallel", "parallel"),
        name="to_bf16",
    )(w)


def _glain_body(x_ref, g_ref, w_ref, wgz_ref, wg2_ref, b_ref, o_ref, gl_ref, h_ref):
    @pl.when(pl.program_id(1) == 0)
    def _():
        h = _rms(x_ref[...], g_ref[...]).astype(BF16)
        h_ref[...] = h
        gz = jnp.dot(h, wgz_ref[...], preferred_element_type=F32)
        z = jnp.dot(gz.astype(BF16), wg2_ref[...], preferred_element_type=F32) + b_ref[...]
        gl_ref[...] = (jnp.minimum(z, 0.0) - jnp.log(1.0 + jnp.exp(-jnp.abs(z)))) * (1.0 / GLA_GATE_NORM)

    o_ref[...] = jnp.dot(h_ref[...], w_ref[...], preferred_element_type=F32).astype(o_ref.dtype)


def gla_in_proj(x, g, w, layer, ncols, wgz, wg2, b, tm, tn):
    M, D = x.shape
    return pl.pallas_call(
        _glain_body,
        grid=(M // tm, ncols // tn),
        in_specs=[pl.BlockSpec((tm, D), lambda i, j: (i, 0)),
                  pl.BlockSpec((1, D), lambda i, j: (0, 0)),
                  pl.BlockSpec((None, D, tn), lambda i, j: (layer, 0, j)),
                  pl.BlockSpec((D, LANES), lambda i, j: (0, 0)),
                  pl.BlockSpec((LANES, GLA_KD), lambda i, j: (0, 0)),
                  pl.BlockSpec((1, GLA_KD), lambda i, j: (0, 0))],
        out_specs=[pl.BlockSpec((tm, tn), lambda i, j: (i, j)),
                   pl.BlockSpec((tm, GLA_KD), lambda i, j: (i, 0))],
        out_shape=[jax.ShapeDtypeStruct((M, ncols), BF16),
                   jax.ShapeDtypeStruct((M, GLA_KD), F32)],
        scratch_shapes=[pltpu.VMEM((tm, D), BF16)],
        compiler_params=_params("parallel", "arbitrary"),
        name="gla_in_proj",
    )(x, g, w, wgz, wg2, b)


PERM_ROWS = 16
QK_ROWBLOCK = 512


def _nmqk_body(x_ref, g_ref, w_ref, cg_ref, bd_ref, perm_ref, o_ref, h_ref, *, tm, tn, d):
    j = pl.program_id(2)
    rows = tm // d

    @pl.when(j == 0)
    def _():
        h = _rms(x_ref[...], g_ref[...]).astype(BF16)
        if d == 1:
            h_ref[...] = h
        else:
            sub = PERM_ROWS * d
            for s in range(tm // sub):
                hs = jnp.dot(perm_ref[...], h[s * sub:(s + 1) * sub], preferred_element_type=F32).astype(BF16)
                for r in range(d):
                    dst = r * rows + PERM_ROWS * s
                    h_ref[dst:dst + PERM_ROWS, :] = hs[r * PERM_ROWS:(r + 1) * PERM_ROWS]

    is_norm = ((j * tn) // DIL_WIDTH) % 3 != 2

    rb = min(QK_ROWBLOCK, tm)

    def emit(a, cs, y):
        yb = y.astype(o_ref.dtype)
        for r in range(d):
            lo, hi = max(a, r * rows), min(a + rb, (r + 1) * rows)
            if lo < hi:
                o_ref[r, lo - r * rows:hi - r * rows, cs] = yb[lo - a:hi - a]

    def chunks(norm):
        blocks = [(slice(c * MXU_WIDTH, (c + 1) * MXU_WIDTH), a)
                  for c in range(tn // MXU_WIDTH) for a in range(0, tm, rb)]
        proj = lambda cs, a: jnp.dot(h_ref[a:a + rb, :], w_ref[:, cs], preferred_element_type=F32)
        ahead = proj(*blocks[0])
        for n, (cs, a) in enumerate(blocks):
            pc = ahead
            if n + 1 < len(blocks):
                ahead = proj(*blocks[n + 1])
            if norm:
                ss = jnp.dot((pc * pc).astype(BF16), bd_ref[...], preferred_element_type=F32)
                pc = pc * lax.rsqrt(ss * (1.0 / DIL_HD) + EPS) * cg_ref[:, cs]
            emit(a, cs, pc)

    @pl.when(is_norm)
    def _():
        chunks(True)

    @pl.when(jnp.logical_not(is_norm))
    def _():
        chunks(False)


def norm_matmul_qk(x, g, w, layer, colgain, bd, out_dtype, tm, tn, d=1, group=None):
    B, T, D = x.shape
    N = w.shape[2] if group is None else 3 * DIL_WIDTH
    j0 = 0 if group is None else group * N // tn
    sub = PERM_ROWS * d
    return pl.pallas_call(
        functools.partial(_nmqk_body, tm=tm, tn=tn, d=d),
        grid=(B, T // tm, N // tn),
        in_specs=[pl.BlockSpec((None, tm, D), lambda b, i, j: (b, i, 0)),
                  pl.BlockSpec((1, D), lambda b, i, j: (0, 0)),
                  pl.BlockSpec((None, D, tn), lambda b, i, j: (layer, 0, j + j0)),
                  pl.BlockSpec((1, tn), lambda b, i, j: (0, j + j0)),
                  pl.BlockSpec((MXU_WIDTH, MXU_WIDTH), lambda b, i, j: (0, 0)),
                  pl.BlockSpec((sub, sub), lambda b, i, j: (0, 0))],
        out_specs=pl.BlockSpec((None, d, tm // d, tn), lambda b, i, j: (b, 0, i, j)),
        out_shape=jax.ShapeDtypeStruct((B, d, T // d, N), out_dtype),
        scratch_shapes=[pltpu.VMEM((tm, D), BF16)],
        compiler_params=_params("parallel", "parallel", "arbitrary"),
        name="norm_matmul_qk",
    )(x, g, w, colgain, bd, _perm_matrix(d))


def _perm_matrix(d, pr=PERM_ROWS):
    sub = pr * d
    pm = np.zeros((sub, sub), np.float32)
    for r in range(d):
        for i in range(pr):
            pm[r * pr + i, i * d + r] = 1.0
    return jnp.asarray(pm, BF16)


def _prep_body(x_ref, g_ref, *refs, tm, ds):
    perm_refs, o_refs = refs[:len(ds)], refs[len(ds):]
    h = _rms(x_ref[...], g_ref[...]).astype(BF16)
    for d, perm_ref, o_ref in zip(ds, perm_refs, o_refs):
        if d == 1:
            o_ref[0] = h
            continue
        sub = PERM_ROWS * d
        for s in range(tm // sub):
            hs = jnp.dot(perm_ref[...], h[s * sub:(s + 1) * sub], preferred_element_type=F32).astype(BF16)
            for r in range(d):
                o_ref[r, PERM_ROWS * s:PERM_ROWS * (s + 1), :] = hs[r * PERM_ROWS:(r + 1) * PERM_ROWS]


def rms_prep(x, g, ds, tm):
    B, T, D = x.shape
    return pl.pallas_call(
        functools.partial(_prep_body, tm=tm, ds=ds),
        grid=(B, T // tm),
        in_specs=[pl.BlockSpec((None, tm, D), lambda b, i: (b, i, 0)),
                  pl.BlockSpec((1, D), lambda b, i: (0, 0))]
                 + [pl.BlockSpec((PERM_ROWS * d, PERM_ROWS * d), lambda b, i: (0, 0)) for d in ds],
        out_specs=[pl.BlockSpec((None, d, tm // d, D), lambda b, i: (b, 0, i, 0)) for d in ds],
        out_shape=[jax.ShapeDtypeStruct((B, d, T // d, D), BF16) for d in ds],
        compiler_params=_params("parallel", "parallel"),
        name="rms_prep",
    )(x, g, *[_perm_matrix(d) for d in ds])


def _projrows_body(h_ref, w_ref, cg_ref, bd_ref, o_ref, *, tm, norm):
    rb = min(QK_ROWBLOCK, tm)
    blocks = [(slice(c * MXU_WIDTH, (c + 1) * MXU_WIDTH), a)
              for c in range(DIL_WIDTH // MXU_WIDTH) for a in range(0, tm, rb)]
    proj = lambda cs, a: jnp.dot(h_ref[a:a + rb, :], w_ref[:, cs], preferred_element_type=F32)
    ahead = proj(*blocks[0])
    for n, (cs, a) in enumerate(blocks):
        pc = ahead
        if n + 1 < len(blocks):
            ahead = proj(*blocks[n + 1])
        if norm:
            ss = jnp.dot((pc * pc).astype(BF16), bd_ref[...], preferred_element_type=F32)
            pc = pc * lax.rsqrt(ss * (1.0 / DIL_HD) + EPS) * cg_ref[:, cs]
        o_ref[a:a + rb, cs] = pc.astype(o_ref.dtype)


def proj_rows(h, w, layer, seg, colgain, bd, norm, tm):
    M, D = h.shape
    return pl.pallas_call(
        functools.partial(_projrows_body, tm=tm, norm=norm),
        grid=(M // tm,),
        in_specs=[pl.BlockSpec((tm, D), lambda i: (i, 0)),
                  pl.BlockSpec((None, D, DIL_WIDTH), lambda i: (layer, 0, seg)),
                  pl.BlockSpec((1, DIL_WIDTH), lambda i: (0, seg)),
                  pl.BlockSpec((MXU_WIDTH, MXU_WIDTH), lambda i: (0, 0))],
        out_specs=pl.BlockSpec((tm, DIL_WIDTH), lambda i: (i, 0)),
        out_shape=jax.ShapeDtypeStruct((M, DIL_WIDTH), BF16),
        compiler_params=_params("parallel"),
        name="proj_rows",
    )(h, w, colgain, bd)


def _projcols_body(h_ref, wt_ref, cg_ref, bd_ref, o_ref, *, tl, norm, blocked):
    _projcols_core(h_ref[...], wt_ref, cg_ref, bd_ref, o_ref, tl, norm, blocked)


def _cacherows_body(ha_ref, hb_ref, wt_ref, cg_ref, bd_ref, o_ref, *, tl, norm):
    h = jnp.where(pl.program_id(0) == 0, ha_ref[...], hb_ref[...])
    _projcols_core(h, wt_ref, cg_ref, bd_ref, o_ref, tl, norm, False)


def cache_rows(hs, wt, seg, rowgains, bd, norm, keep, tl):
    B, T, D = hs[0].shape
    first, count = (T - keep) // tl, keep // tl
    hspec = pl.BlockSpec((None, tl, D), lambda l, g, i: (g, first + i, 0))
    return pl.pallas_call(
        functools.partial(_cacherows_body, tl=tl, norm=norm),
        grid=(len(hs), B, count),
        in_specs=[hspec, hspec,
                  pl.BlockSpec((None, DIL_WIDTH, D), lambda l, g, i: (l, seg, 0)),
                  pl.BlockSpec((None, DIL_WIDTH, LANES), lambda l, g, i: (l, 0, 0)),
                  pl.BlockSpec((MXU_WIDTH, MXU_WIDTH), lambda l, g, i: (0, 0))],
        out_specs=pl.BlockSpec((None, None, DIL_WIDTH, tl), lambda l, g, i: (l, g, 0, i)),
        out_shape=jax.ShapeDtypeStruct((len(hs), B, DIL_WIDTH, keep), F32),
        compiler_params=_params("parallel", "parallel", "parallel"),
        name="cache_rows",
    )(hs[0], hs[1], wt, rowgains, bd)


def _projcols_core(h, wt_ref, cg_ref, bd_ref, o_ref, tl, norm, blocked):
    kt = lax.dot_general(wt_ref[...], h, _NT, preferred_element_type=F32)
    for fb in range(DIL_WIDTH // MXU_WIDTH):
        fs = slice(fb * MXU_WIDTH, (fb + 1) * MXU_WIDTH)
        blk = kt[fs]
        if norm:
            ss = jnp.dot(bd_ref[...], (blk * blk).astype(BF16), preferred_element_type=F32)
            gain = jnp.concatenate([cg_ref[fs, :]] * (tl // LANES), axis=1)
            blk = blk * lax.rsqrt(ss * (1.0 / DIL_HD) + EPS) * gain
        blk = blk.astype(o_ref.dtype)
        if blocked:
            for jb in range(tl // DIL_BLOCK):
                o_ref[jb, fs, :] = blk[:, jb * DIL_BLOCK:(jb + 1) * DIL_BLOCK]
        else:
            o_ref[fs, :] = blk


def proj_cols(h, wt, layer, seg, rowgain, bd, norm, out_dtype, tl, first=0, count=None, blocked=False):
    G, L, D = h.shape
    count = L // tl if count is None else count
    if blocked:
        nb = tl // DIL_BLOCK
        out_spec = pl.BlockSpec((None, nb, DIL_WIDTH, DIL_BLOCK), lambda g, i: (g, i, 0, 0))
        out_shape = jax.ShapeDtypeStruct((G, count * nb, DIL_WIDTH, DIL_BLOCK), out_dtype)
    else:
        out_spec = pl.BlockSpec((None, DIL_WIDTH, tl), lambda g, i: (g, 0, i))
        out_shape = jax.ShapeDtypeStruct((G, DIL_WIDTH, count * tl), out_dtype)
    return pl.pallas_call(
        functools.partial(_projcols_body, tl=tl, norm=norm, blocked=blocked),
        grid=(G, count),
        in_specs=[pl.BlockSpec((None, tl, D), lambda g, i: (g, first + i, 0)),
                  pl.BlockSpec((None, DIL_WIDTH, D), lambda g, i: (layer, seg, 0)),
                  pl.BlockSpec((DIL_WIDTH, LANES), lambda g, i: (0, 0)),
                  pl.BlockSpec((MXU_WIDTH, MXU_WIDTH), lambda g, i: (0, 0))],
        out_specs=out_spec,
        out_shape=out_shape,
        compiler_params=_params("parallel", "parallel"),
        name="proj_cols",
    )(h, wt, rowgain, bd)


def _mmres_body(y_ref, w_ref, x_ref, o_ref):
    o_ref[...] = x_ref[...] + jnp.dot(y_ref[...], w_ref[...], preferred_element_type=F32)


def matmul_residual(y, w, layer, x, tm, tn):
    M, K = y.shape
    N = w.shape[2]
    return pl.pallas_call(
        _mmres_body,
        grid=(M // tm, N // tn),
        in_specs=[pl.BlockSpec((tm, K), lambda i, j: (i, 0)),
                  pl.BlockSpec((None, K, tn), lambda i, j: (layer, 0, j)),
                  pl.BlockSpec((tm, tn), lambda i, j: (i, j))],
        out_specs=pl.BlockSpec((tm, tn), lambda i, j: (i, j)),
        out_shape=jax.ShapeDtypeStruct((M, N), F32),
        compiler_params=_params("parallel", "parallel"),
        name="matmul_residual",
    )(y, w, x)


def _gla_body(q_ref, k_ref, v_ref, r_ref, gl_ref, s0_ref, gn_ref, tri_ref, y_ref, st_ref, S_ref, *, TB, CH):
    c = pl.program_id(1)

    @pl.when(c == 0)
    def _():
        S_ref[...] = s0_ref[...]

    gl = gl_ref[...]
    g1 = gl.astype(BF16)
    r1 = gl - g1.astype(F32)
    g2 = r1.astype(BF16)
    g3 = (r1 - g2.astype(F32)).astype(BF16)
    tri = tri_ref[...]
    bfull = (jnp.dot(tri, g1, preferred_element_type=F32)
             + jnp.dot(tri, g2, preferred_element_type=F32)
             + jnp.dot(tri, g3, preferred_element_type=F32))

    H = GLA_HEADS
    R = H * CH
    own = (lax.broadcasted_iota(jnp.int32, (R, GLA_KD), 0) // CH
           == lax.broadcasted_iota(jnp.int32, (R, GLA_KD), 1) // GLA_DK)
    arow = lax.broadcasted_iota(jnp.int32, (R, R), 0)
    acol = lax.broadcasted_iota(jnp.int32, (R, R), 1)
    amask = (arow // CH == acol // CH) & (arow % CH >= acol % CH)
    heads = lambda a: jnp.concatenate([a] * H, axis=0)
    split = lambda ref, rs: jnp.concatenate([ref[rs, h * GLA_DV:(h + 1) * GLA_DV] for h in range(H)], axis=0)
    mid = CH // 2
    gn = gn_ref[...]
    S = S_ref[...]
    for sc in range(TB // CH):
        rs = slice(sc * CH, (sc + 1) * CH)
        b = bfull[rs]
        if sc > 0:
            b = b - bfull[sc * CH - 1:sc * CH]
        ref = b[mid:mid + 1]
        blast = b[CH - 1:CH]
        qf = q_ref[rs, :].astype(F32) * (GLA_DK ** -0.5)
        kf = k_ref[rs, :].astype(F32)
        qe = (qf * jnp.exp(b - ref)).astype(BF16)
        ke = jnp.where(own, heads(kf * jnp.exp(ref - b)), 0.0).astype(BF16)
        kd = jnp.where(own, heads(kf * jnp.exp(blast - b)), 0.0).astype(BF16)
        qb = jnp.where(own, heads(qf * jnp.exp(b)), 0.0).astype(BF16)
        vst = split(v_ref, rs)
        a = heads(lax.dot_general(qe, ke, _NT, preferred_element_type=F32))
        a = jnp.where(amask, a, 0.0).astype(BF16)
        o = (jnp.dot(a, vst, preferred_element_type=F32)
             + lax.dot_general(qb, S.astype(BF16), _NT, preferred_element_type=F32))
        S = jnp.exp(blast) * S + lax.dot_general(vst, kd, _TN, preferred_element_type=F32)
        y = (_rms(o, gn) * _silu(split(r_ref, rs).astype(F32))).astype(y_ref.dtype)
        for h in range(H):
            y_ref[rs, h * GLA_DV:(h + 1) * GLA_DV] = y[h * CH:(h + 1) * CH]
    S_ref[...] = S

    @pl.when(c == pl.num_programs(1) - 1)
    def _():
        st_ref[...] = S_ref[...]


def _pack_state(s):
    return s.transpose(0, 3, 1, 2).reshape(s.shape[0], GLA_DV, GLA_HEADS * GLA_DK)


def _unpack_state(s):
    return s.reshape(s.shape[0], GLA_DV, GLA_HEADS, GLA_DK).transpose(0, 2, 3, 1)


def gla_scan(p, glog, s0t, gn, TB, CH):
    B, T, _ = p.shape
    sblk = (None, GLA_DV, GLA_HEADS * GLA_DK)
    tri = jnp.asarray(np.tril(np.ones((TB, TB), np.float32)), BF16)
    return pl.pallas_call(
        functools.partial(_gla_body, TB=TB, CH=CH),
        grid=(B, T // TB),
        in_specs=[pl.BlockSpec((None, TB, GLA_KD), lambda b, c: (b, c, 0)),
                  pl.BlockSpec((None, TB, GLA_KD), lambda b, c: (b, c, 1)),
                  pl.BlockSpec((None, TB, GLA_VD), lambda b, c: (b, c, 1)),
                  pl.BlockSpec((None, TB, GLA_VD), lambda b, c: (b, c, 2)),
                  pl.BlockSpec((None, TB, GLA_KD), lambda b, c: (b, c, 0)),
                  pl.BlockSpec(sblk, lambda b, c: (b, 0, 0)),
                  pl.BlockSpec((1, GLA_DV), lambda b, c: (0, 0)),
                  pl.BlockSpec((TB, TB), lambda b, c: (0, 0))],
        out_specs=[pl.BlockSpec((None, TB, GLA_VD), lambda b, c: (b, c, 0)),
                   pl.BlockSpec(sblk, lambda b, c: (b, 0, 0))],
        out_shape=[jax.ShapeDtypeStruct((B, T, GLA_VD), BF16),
                   jax.ShapeDtypeStruct((B,) + sblk[1:], F32)],
        scratch_shapes=[pltpu.VMEM(sblk[1:], F32)],
        compiler_params=_params("parallel", "arbitrary"),
        name="gla_scan",
    )(p, p, p, p, glog, s0t, gn, tri)


def _attn_body(q_ref, kp_ref, kc_ref, vp_ref, vc_ref, bias_ref, o_ref, lse_ref):
    var = jnp.where(pl.program_id(2) == 0, 0, 1)
    lane = lax.broadcasted_iota(jnp.int32, (DIL_BLOCK, LANES), 1)
    lo = lane < DIL_HD
    lse_acc = jnp.zeros((DIL_BLOCK, LANES), F32)
    zero = jnp.zeros((), BF16)
    def scores(hp):
        cs = slice(hp * LANES, (hp + 1) * LANES)
        q2 = q_ref[:, cs]
        qab = jnp.concatenate([jnp.where(lo, q2, zero), jnp.where(lo, zero, q2)], axis=0)
        k2t = jnp.concatenate([kp_ref[cs, :], kc_ref[cs, :]], axis=1)
        return jnp.dot(qab, k2t, preferred_element_type=F32)

    ahead = scores(0)
    for hp in range(DIL_HEADS // 2):
        cs = slice(hp * LANES, (hp + 1) * LANES)
        s = ahead + bias_ref[var, hp]
        if hp + 1 < DIL_HEADS // 2:
            ahead = scores(hp + 1)
        v2 = jnp.concatenate([vp_ref[:, cs], vc_ref[:, cs]], axis=0)
        m = jnp.max(s, axis=-1, keepdims=True)
        p = jnp.exp(s - m)
        l = jnp.sum(p, axis=-1, keepdims=True)
        o = jnp.dot(p.astype(BF16), v2, preferred_element_type=F32) * (1.0 / l)
        o_ref[:, cs] = jnp.where(lo, o[:DIL_BLOCK], o[DIL_BLOCK:]).astype(o_ref.dtype)
        lse = m + jnp.log(l)
        lse_acc = jnp.where(lane == 2 * hp, lse[:DIL_BLOCK], lse_acc)
        lse_acc = jnp.where(lane == 2 * hp + 1, lse[DIL_BLOCK:], lse_acc)
    lse_ref[...] = lse_acc


def dil_attention(q, kt, v, bias):
    B, d, L, _ = q.shape
    nb = L // DIL_BLOCK
    blk = (None, None, DIL_BLOCK, DIL_WIDTH)
    tblk = (None, None, None, DIL_WIDTH, DIL_BLOCK)
    prev = lambda n: jnp.maximum(n - 1, 0)
    return pl.pallas_call(
        _attn_body,
        grid=(B, d, nb),
        in_specs=[pl.BlockSpec(blk, lambda b, r, n: (b, r, n, 0)),
                  pl.BlockSpec(tblk, lambda b, r, n: (b, r, prev(n), 0, 0)),
                  pl.BlockSpec(tblk, lambda b, r, n: (b, r, n, 0, 0)),
                  pl.BlockSpec(blk, lambda b, r, n: (b, r, prev(n), 0)),
                  pl.BlockSpec(blk, lambda b, r, n: (b, r, n, 0)),
                  pl.BlockSpec((2, DIL_HEADS // 2, 2 * DIL_BLOCK, 2 * DIL_BLOCK), lambda b, r, n: (0, 0, 0, 0))],
        out_specs=[pl.BlockSpec(blk, lambda b, r, n: (b, r, n, 0)),
                   pl.BlockSpec((None, None, DIL_BLOCK, LANES), lambda b, r, n: (b, r, n, 0))],
        out_shape=[jax.ShapeDtypeStruct((B, d, L, DIL_WIDTH), BF16),
                   jax.ShapeDtypeStruct((B, d, L, LANES), F32)],
        compiler_params=_params("parallel", "parallel", "arbitrary"),
        name="dil_attention",
    )(q, kt, kt, v, v, bias)


UNPERM_SUB = MXU_WIDTH


def _dilout_body(o0_ref, o1_ref, o2_ref, l0_ref, l1_ref, l2_ref, x_ref, w_ref, e_ref, p1_ref, p2_ref, out_ref,
                 *, tm):
    def position_order(o_ref, pt_ref, d):
        if d == 1:
            return o_ref[0].astype(F32)
        pr = UNPERM_SUB // d
        out = []
        for s in range(tm // UNPERM_SUB):
            blk = jnp.concatenate([o_ref[r, pr * s:pr * (s + 1), :] for r in range(d)], axis=0)
            out.append(jnp.dot(pt_ref[...], blk, preferred_element_type=F32))
        return jnp.concatenate(out, axis=0)

    dils = [d for _, d in DIL_GROUPS]
    o0 = position_order(o0_ref, None, dils[0])
    o1 = position_order(o1_ref, p1_ref, dils[1])
    o2 = position_order(o2_ref, p2_ref, dils[2])
    l0, l1, l2 = l0_ref[...], l1_ref[...], l2_ref[...]
    mx = jnp.maximum(jnp.maximum(l0, l1), l2)
    e0, e1, e2 = jnp.exp(l0 - mx), jnp.exp(l1 - mx), jnp.exp(l2 - mx)
    den = e0 + e1 + e2
    ex = e_ref[...]

    def expand(w):
        w1 = w.astype(BF16)
        w2 = (w - w1.astype(F32)).astype(BF16)
        return jnp.dot(jnp.concatenate([w1, w2], axis=1), ex, preferred_element_type=F32)

    o = expand(e0 / den) * o0 + expand(e1 / den) * o1 + expand(e2 / den) * o2
    out_ref[...] = x_ref[...] + jnp.dot(o.astype(BF16), w_ref[...], preferred_element_type=F32)


def dil_out(os, ls, x, w, layer, expand, tm):
    B, T, D = x.shape
    dils = [d for _, d in DIL_GROUPS]
    ospecs = [pl.BlockSpec((None, d, tm // d, DIL_WIDTH), lambda b, i: (b, 0, i, 0)) for d in dils]
    lspec = pl.BlockSpec((None, tm, LANES), lambda b, i: (b, i, 0))
    pts = [jnp.transpose(_perm_matrix(d, UNPERM_SUB // d)) for d in dils[1:]]
    return pl.pallas_call(
        functools.partial(_dilout_body, tm=tm),
        grid=(B, T // tm),
        in_specs=ospecs + [lspec, lspec, lspec,
                           pl.BlockSpec((None, tm, D), lambda b, i: (b, i, 0)),
                           pl.BlockSpec((None, DIL_WIDTH, D_MODEL), lambda b, i: (layer, 0, 0)),
                           pl.BlockSpec((2 * LANES, DIL_WIDTH), lambda b, i: (0, 0))]
                 + [pl.BlockSpec(p.shape, lambda b, i: (0, 0)) for p in pts],
        out_specs=pl.BlockSpec((None, tm, D_MODEL), lambda b, i: (b, i, 0)),
        out_shape=jax.ShapeDtypeStruct((B, T, D_MODEL), F32),
        compiler_params=_params("parallel", "parallel"),
        name="dil_out",
    )(*os, *ls, x, w, expand, *pts)


S_CHUNK = 1024
S_NEWPAD = LANES


def _sattn_body(qkv_ref, k0_ref, v0_ref, k1_ref, v1_ref, k2_ref, v2_ref, b0_ref, b1_ref, b2_ref, bn_ref,
                o_ref, m_ref, l_ref, acc_ref, *, seq):
    c = pl.program_id(1)
    R = seq * DIL_HEADS
    rowh = lax.broadcasted_iota(jnp.int32, (R, DIL_WIDTH), 0) % DIL_HEADS
    colh = lax.broadcasted_iota(jnp.int32, (R, DIL_WIDTH), 1) // DIL_HD
    hmask = rowh == colh

    def seg(g, part):
        return qkv_ref[:, (3 * g + part) * DIL_WIDTH:(3 * g + part + 1) * DIL_WIDTH]

    def qbd(g):
        q = seg(g, 0)
        qrep = jnp.concatenate([jnp.broadcast_to(q[s:s + 1], (DIL_HEADS, DIL_WIDTH)) for s in range(seq)], axis=0)
        return jnp.where(hmask, qrep, 0.0).astype(BF16)

    def update(parts):
        m_old = m_ref[...]
        m_new = m_old
        for s, _ in parts:
            m_new = jnp.maximum(m_new, jnp.max(s, axis=-1, keepdims=True))
        alpha = jnp.exp(m_old - m_new)
        l = alpha * l_ref[...]
        acc = alpha * acc_ref[...]
        for s, pv_fn in parts:
            p = jnp.exp(s - m_new)
            l = l + jnp.sum(p, axis=-1, keepdims=True)
            acc = acc + pv_fn(p.astype(BF16))
        m_ref[...] = m_new
        l_ref[...] = l
        acc_ref[...] = acc

    def cache_segment(g, kt_ref, vt_ref, b_ref):
        s = jnp.dot(qbd(g), kt_ref[...].astype(BF16), preferred_element_type=F32) + b_ref[...]
        return s, lambda p: lax.dot_general(p, vt_ref[...].astype(BF16), _NT, preferred_element_type=F32)

    def new_segment(g):
        pad = jnp.zeros((S_NEWPAD - seq, DIL_WIDTH), F32)
        kn = jnp.concatenate([seg(g, 1), pad], axis=0).astype(BF16)
        vn = jnp.concatenate([seg(g, 2), pad], axis=0).astype(BF16)
        s = lax.dot_general(qbd(g), kn, _NT, preferred_element_type=F32) + bn_ref[g]
        return s, lambda p: jnp.dot(p, vn, preferred_element_type=F32)

    @pl.when(c == 0)
    def _():
        m_ref[...] = jnp.full(m_ref.shape, 2 * NEG, F32)
        l_ref[...] = jnp.zeros(l_ref.shape, F32)
        acc_ref[...] = jnp.zeros(acc_ref.shape, F32)
        update([new_segment(g) for g in range(N_GROUPS)]
               + [cache_segment(0, k0_ref, v0_ref, b0_ref), cache_segment(1, k1_ref, v1_ref, b1_ref),
                  cache_segment(2, k2_ref, v2_ref, b2_ref)])

    @pl.when(c > 0)
    def _():
        update([cache_segment(2, k2_ref, v2_ref, b2_ref)])

    @pl.when(c == pl.num_programs(1) - 1)
    def _():
        on = jnp.where(hmask, acc_ref[...] / l_ref[...], 0.0).astype(BF16)
        srow = lax.broadcasted_iota(jnp.int32, (8, R), 0)
        scol = lax.broadcasted_iota(jnp.int32, (8, R), 1) // DIL_HEADS
        sel = jnp.where(srow == scol, 1.0, 0.0).astype(BF16)
        o_ref[...] = jnp.dot(sel, on, preferred_element_type=F32)[:seq]


def dil_sample_attention(li, qkv, kts, vts, bbs, bn):
    DB, S, _ = qkv.shape
    R = DIL_HEADS * S
    W2 = kts[2].shape[-1]
    in_specs = [pl.BlockSpec((None, S, qkv.shape[-1]), lambda b, c: (b, 0, 0))]
    args = [qkv]
    for g in range(2):
        for a in (kts[g], vts[g]):
            in_specs.append(pl.BlockSpec((None, None, DIL_WIDTH, a.shape[-1]), lambda b, c: (li, b, 0, 0)))
            args.append(a)
    for a in (kts[2], vts[2]):
        in_specs.append(pl.BlockSpec((None, None, DIL_WIDTH, S_CHUNK), lambda b, c: (li, b, 0, c)))
        args.append(a)
    for g in range(2):
        in_specs.append(pl.BlockSpec(bbs[g].shape, lambda b, c: (0, 0)))
        args.append(bbs[g])
    in_specs.append(pl.BlockSpec((R, S_CHUNK), lambda b, c: (0, c)))
    args.append(bbs[2])
    in_specs.append(pl.BlockSpec(bn.shape, lambda b, c: (0, 0, 0)))
    args.append(bn)
    return pl.pallas_call(
        functools.partial(_sattn_body, seq=S),
        grid=(DB, W2 // S_CHUNK),
        in_specs=in_specs,
        out_specs=pl.BlockSpec((None, S, DIL_WIDTH), lambda b, c: (b, 0, 0)),
        out_shape=jax.ShapeDtypeStruct((DB, S, DIL_WIDTH), F32),
        scratch_shapes=[pltpu.VMEM((R, 1), F32), pltpu.VMEM((R, 1), F32), pltpu.VMEM((R, DIL_WIDTH), F32)],
        compiler_params=_params("parallel", "arbitrary"),
        name="dil_sample_attention",
    )(*args)


FFN_CHUNK = MXU_WIDTH
HALO = 16


FFN_ROWBLOCK = 128


def _causal_conv(ext, cw, cb):
    u1 = pltpu.roll(ext, 1, axis=0)[HALO:]
    u2 = pltpu.roll(ext, 2, axis=0)[HALO:]
    return cb + (cw[0:1] * u2 + cw[1:2] * u1 + cw[2:3] * ext[HALO:])


def _ffnup_body(x_ref, xh_ref, g_ref, wg_ref, wv_ref, cw_ref, cb_ref, act_ref, tail_ref, h_ref, *, tm):
    first = pl.program_id(1) == 0
    hh = _rms(xh_ref[...], g_ref[...])
    h_ref[0:HALO] = jnp.where(first, 0.0, hh).astype(BF16)
    h_ref[HALO:HALO + tm] = _rms(x_ref[...], g_ref[...]).astype(BF16)
    for c in range(D_FF // FFN_CHUNK):
        cols = (slice(c * FFN_CHUNK, (c + 1) * FFN_CHUNK),
                slice(D_FF + c * FFN_CHUNK, D_FF + (c + 1) * FFN_CHUNK))
        tails = [None, None]
        for a in range(0, tm, FFN_ROWBLOCK):
            conv = []
            for idx, w_ref in enumerate((wg_ref, wv_ref)):
                if a == 0:
                    ext = jnp.dot(h_ref[0:HALO + FFN_ROWBLOCK], w_ref[:, cols[0]], preferred_element_type=F32)
                else:
                    u = jnp.dot(h_ref[HALO + a:HALO + a + FFN_ROWBLOCK], w_ref[:, cols[0]],
                                preferred_element_type=F32)
                    ext = jnp.concatenate([tails[idx], u], axis=0)
                tails[idx] = ext[FFN_ROWBLOCK:]
                conv.append(_causal_conv(ext, cw_ref[:, cols[idx]], cb_ref[:, cols[idx]]))
            act_ref[a:a + FFN_ROWBLOCK, cols[0]] = (_silu(conv[0]) * conv[1]).astype(act_ref.dtype)
        for idx in range(2):
            tail_ref[:, cols[idx]] = tails[idx][HALO - 8:]


def ffn_up_prompt(x, g, w, layer, cw, cb, tm):
    B, T, D = x.shape
    hblk = tm // HALO
    return pl.pallas_call(
        functools.partial(_ffnup_body, tm=tm),
        grid=(B, T // tm),
        in_specs=[pl.BlockSpec((None, tm, D), lambda b, i: (b, i, 0)),
                  pl.BlockSpec((None, HALO, D), lambda b, i: (b, jnp.maximum(i * hblk - 1, 0), 0)),
                  pl.BlockSpec((1, D), lambda b, i: (0, 0)),
                  pl.BlockSpec((None, D, D_FF), lambda b, i: (layer, 0, 0)),
                  pl.BlockSpec((None, D, D_FF), lambda b, i: (layer, 0, 1)),
                  pl.BlockSpec((CONV_WIDTH, 2 * D_FF), lambda b, i: (0, 0)),
                  pl.BlockSpec((1, 2 * D_FF), lambda b, i: (0, 0))],
        out_specs=[pl.BlockSpec((None, tm, D_FF), lambda b, i: (b, i, 0)),
                   pl.BlockSpec((None, 8, 2 * D_FF), lambda b, i: (b, 0, 0))],
        out_shape=[jax.ShapeDtypeStruct((B, T, D_FF), BF16),
                   jax.ShapeDtypeStruct((B, 8, 2 * D_FF), F32)],
        scratch_shapes=[pltpu.VMEM((HALO + tm, D), BF16)],
        compiler_params=_params("parallel", "arbitrary"),
        name="ffn_up_prompt",
    )(x, x, g, w, w, cw, cb)


def _ffnup_s_body(x_ref, g_ref, wg_ref, wv_ref, cw_ref, cb_ref, c1_ref, c2_ref, act_ref, u_ref, u_scr, *, rows, seq):
    h = _rms(x_ref[...], g_ref[...]).astype(BF16)
    pos = lax.broadcasted_iota(jnp.int32, (rows, FFN_CHUNK), 0) % seq
    u_scr[:, 0:HALO] = jnp.zeros((2, HALO, FFN_CHUNK), F32)
    for c in range(D_FF // FFN_CHUNK):
        acts = []
        for idx, off in enumerate((0, D_FF)):
            cs = slice(off + c * FFN_CHUNK, off + (c + 1) * FFN_CHUNK)
            w_ref = wg_ref if idx == 0 else wv_ref
            u = jnp.dot(h, w_ref[:, c * FFN_CHUNK:(c + 1) * FFN_CHUNK], preferred_element_type=F32)
            u_scr[idx, HALO:HALO + rows] = u
            u1 = jnp.where(pos >= 1, u_scr[idx, HALO - 1:HALO - 1 + rows], c1_ref[:, cs])
            u2 = jnp.where(pos >= 2, u_scr[idx, HALO - 2:HALO - 2 + rows], c2_ref[:, cs])
            acts.append(cb_ref[:, cs] + (cw_ref[0:1, cs] * u2 + cw_ref[1:2, cs] * u1 + cw_ref[2:3, cs] * u))
            u_ref[:, cs] = u
        act_ref[:, c * FFN_CHUNK:(c + 1) * FFN_CHUNK] = (_silu(acts[0]) * acts[1]).astype(act_ref.dtype)


def ffn_up_sample(x, g, w, layer, cw, cb, carry1, carry2, seq):
    M, D = x.shape
    full = lambda shape: pl.BlockSpec(shape, lambda i: tuple(0 for _ in shape))
    return pl.pallas_call(
        functools.partial(_ffnup_s_body, rows=M, seq=seq),
        grid=(1,),
        in_specs=[full((M, D)), full((1, D)),
                  pl.BlockSpec((None, D, D_FF), lambda i: (layer, 0, 0)),
                  pl.BlockSpec((None, D, D_FF), lambda i: (layer, 0, 1)),
                  full((CONV_WIDTH, 2 * D_FF)), full((1, 2 * D_FF)),
                  full((M, 2 * D_FF)), full((M, 2 * D_FF))],
        out_specs=[full((M, D_FF)), full((M, 2 * D_FF))],
        out_shape=[jax.ShapeDtypeStruct((M, D_FF), BF16),
                   jax.ShapeDtypeStruct((M, 2 * D_FF), F32)],
        scratch_shapes=[pltpu.VMEM((2, HALO + M, FFN_CHUNK), F32)],
        compiler_params=_params("arbitrary"),
        name="ffn_up_sample",
    )(x, g, w, w, cw, cb, carry1, carry2)


def _rel_bucket(dist):
    max_exact = NUM_BUCKETS // 2
    df = jnp.maximum(dist, 1).astype(F32)
    large = max_exact + (jnp.log(df / max_exact) / math.log(MAX_DISTANCE / max_exact)
                         * (NUM_BUCKETS - max_exact)).astype(jnp.int32)
    large = jnp.minimum(large, NUM_BUCKETS - 1)
    return jnp.where(dist < max_exact, dist, large)


def _step_table(rel_bias, g, d):
    J = DIL_GROUPS[g][0] // d
    tab = rel_bias[:, g * DIL_HEADS:(g + 1) * DIL_HEADS].astype(F32)
    return tab[_rel_bucket(jnp.arange(J + 1) * d)]


def _lookup(table, idx):
    onehot = jax.nn.one_hot(jnp.asarray(idx, jnp.int32), table.shape[0], dtype=F32)
    return jnp.einsum('...j,jh->...h', onehot, table, precision=lax.Precision.HIGHEST)


def _prompt_bias(rel_bias, g, d):
    J = DIL_GROUPS[g][0] // d
    qi = np.arange(DIL_BLOCK)[:, None]
    ki = np.arange(2 * DIL_BLOCK)[None, :]
    rel = qi + DIL_BLOCK - ki
    bias = jnp.moveaxis(_lookup(_step_table(rel_bias, g, d), np.clip(rel, 0, J)), -1, 0)
    out = []
    for first in (True, False):
        valid = (rel >= 0) & (rel <= J) & ((ki >= DIL_BLOCK) | (not first))
        b = jnp.where(jnp.asarray(valid)[None], bias, NEG)
        out.append(b.reshape(DIL_HEADS // 2, 2 * DIL_BLOCK, 2 * DIL_BLOCK))
    return jnp.stack(out)


def _sample_bias(rel_bias, g, d, seq):
    W = DIL_GROUPS[g][0]
    J = W // d
    table = _step_table(rel_bias, g, d)
    s = np.arange(seq)[:, None]
    dist = W + s - np.arange(W)[None, :]
    valid = (dist >= 0) & (dist <= W) & (dist % d == 0)
    bval = jnp.moveaxis(_lookup(table, np.where(valid, dist // d, 0)), -1, 1)
    bbuf = jnp.where(jnp.asarray(valid)[:, None, :], bval, NEG).reshape(seq * DIL_HEADS, W)
    dn = s - np.arange(S_NEWPAD)[None, :]
    vn = (dn >= 0) & (dn % d == 0) & (dn // d <= J) & (np.arange(S_NEWPAD)[None, :] < seq)
    bnew = jnp.moveaxis(_lookup(table, np.where(vn, dn // d, 0)), -1, 1)
    bnew = jnp.where(jnp.asarray(vn)[:, None, :], bnew, NEG).reshape(seq * DIL_HEADS, S_NEWPAD)
    return bbuf, bnew


def _colgain(q_gain, k_gain, ngroups):
    seg = jnp.concatenate([jnp.tile(q_gain.astype(F32) * DIL_SCALE, DIL_HEADS),
                           jnp.tile(k_gain.astype(F32), DIL_HEADS),
                           jnp.ones((DIL_WIDTH,), F32)])
    return jnp.tile(seg, ngroups)[None, :]


def kernel(x_prompt, x_sample, state_gla, cache_k_g0, cache_v_g0, cache_k_g1, cache_v_g1, cache_k_g2, cache_v_g2,
           state_ffn_conv, rel_bias, norm_mix, norm_ffn, gla_w_in, gla_w_gate2, gla_b_gate, gla_norm, gla_w_out,
           dil_w_in, dil_q_norm, dil_k_norm, dil_w_out, ffn_w_up, ffn_conv_w, ffn_conv_b, ffn_w_down):
    B, T, D = x_prompt.shape
    DB, S, _ = x_sample.shape
    MP, MS = B * T, DB * S
    assert all(c.shape[2] == w for c, (w, _) in zip((cache_k_g0, cache_k_g1, cache_k_g2), DIL_GROUPS))

    xp = x_prompt.reshape(MP, D)
    xs = x_sample.reshape(MS, D)
    fmajor = lambda c: c.transpose(0, 1, 3, 4, 2).reshape(c.shape[0], c.shape[1], DIL_WIDTH, c.shape[2])
    k_caches = tuple(fmajor(c) for c in (cache_k_g0, cache_k_g1, cache_k_g2))
    v_caches = tuple(fmajor(c) for c in (cache_v_g0, cache_v_g1, cache_v_g2))

    blockdiag = jnp.asarray(np.kron(np.eye(MXU_WIDTH // DIL_HD), np.ones((DIL_HD, DIL_HD))), BF16)
    expand1 = np.kron(np.eye(LANES, DIL_HEADS, dtype=np.float32), np.ones((1, DIL_HD), np.float32))
    expand = jnp.asarray(np.concatenate([expand1, expand1], axis=0), BF16)
    p_bias = [_prompt_bias(rel_bias, g, d) for g, (_, d) in enumerate(DIL_GROUPS)]
    s_bias = [_sample_bias(rel_bias, g, d, S) for g, (_, d) in enumerate(DIL_GROUPS)]
    s_bias_buf = [b for b, _ in s_bias]
    s_bias_new = jnp.stack([b for _, b in s_bias])

    gla_p, gla_s = [], []
    kp = [[] for _ in DIL_GROUPS]
    vp = [[] for _ in DIL_GROUPS]
    kq = [[] for _ in DIL_GROUPS]
    vq = [[] for _ in DIL_GROUPS]
    conv_p, conv_s = [], []
    cache_h, cache_gain = [], []
    SPAD = 16

    gla_w_in_b = to_bf16(gla_w_in, 256)
    gla_w_out_b = to_bf16(gla_w_out, 256)
    dil_w_in_b = to_bf16(dil_w_in, 128)
    dil_w_out_b = to_bf16(dil_w_out, 256)
    ffn_w_up_b = to_bf16(ffn_w_up, 256)
    ffn_w_down_b = to_bf16(ffn_w_down, 256)
    dil_w_in_t = jnp.swapaxes(dil_w_in_b, 1, 2)
    n_main = 2 * GLA_KD + 2 * GLA_VD

    for i in range(DEPTH):
        li = i // N_MIXERS
        gmix = norm_mix[i][None, :].astype(F32)
        if i % N_MIXERS == 0:
            wgz = jnp.pad(gla_w_in[li][:, n_main:], ((0, 0), (0, LANES - GLA_GATE_RANK))).astype(BF16)
            wg2 = jnp.pad(gla_w_gate2[li], ((0, LANES - GLA_GATE_RANK), (0, 0))).astype(BF16)
            bg = gla_b_gate[li][None, :].astype(F32)
            gn = gla_norm[li][None, :].astype(F32)
            pp, glp = gla_in_proj(xp, gmix, gla_w_in_b, li, n_main, wgz, wg2, bg, 1024, 1024)
            pp, glp = pp.reshape(B, T, -1), glp.reshape(B, T, GLA_KD)
            s0 = jnp.zeros((B, GLA_DV, GLA_HEADS * GLA_DK), F32)
            yp, stp = gla_scan(pp, glp, s0, gn, 128, GLA_CHUNK)
            xp = matmul_residual(yp.reshape(MP, GLA_VD), gla_w_out_b, li, xp, 1024, D)
            gla_p.append(_unpack_state(stp))
            ps, gls = gla_in_proj(xs, gmix, gla_w_in_b, li, n_main, wgz, wg2, bg, MS, 1024)
            ps, gls = ps.reshape(DB, S, -1), gls.reshape(DB, S, GLA_KD)
            ps = jnp.pad(ps, ((0, 0), (0, SPAD - S), (0, 0)))
            gls = jnp.pad(gls, ((0, 0), (0, SPAD - S), (0, 0)))
            ys, sts = gla_scan(ps, gls, _pack_state(state_gla[li].astype(F32)), gn, SPAD, SPAD)
            xs = matmul_residual(ys[:, :S].reshape(MS, GLA_VD), gla_w_out_b, li, xs, MS, D)
            gla_s.append(_unpack_state(sts))
        else:
            cg = _colgain(dil_q_norm[li], dil_k_norm[li], N_GROUPS)
            kgain = jnp.broadcast_to(jnp.tile(dil_k_norm[li].astype(F32), DIL_HEADS)[:, None], (DIL_WIDTH, LANES))
            os_, ls_ = [], []
            dils = tuple(d for _, d in DIL_GROUPS)
            hs = dict(zip(dils, rms_prep(xp.reshape(B, T, D), gmix, dils, 1024)))
            for g, (W, d) in enumerate(DIL_GROUPS):
                L = T // d
                hg = hs[d].reshape(MP, D)
                q = proj_rows(hg, dil_w_in_b, li, 3 * g, cg, blockdiag, True, 2048).reshape(B, d, L, DIL_WIDTH)
                v = proj_rows(hg, dil_w_in_b, li, 3 * g + 2, cg, blockdiag, False, 2048).reshape(B, d, L, DIL_WIDTH)
                kt = proj_cols(hs[d].reshape(1, MP, D), dil_w_in_t, li, 3 * g + 1, kgain, blockdiag, True, BF16,
                               1024, blocked=True).reshape(B, d, L // DIL_BLOCK, DIL_WIDTH, DIL_BLOCK)
                o, lse = dil_attention(q, kt, v, p_bias[g])
                if d > 1:
                    lse = lse.transpose(0, 2, 1, 3)
                os_.append(o)
                ls_.append(lse.reshape(B, T, LANES))
            xp = dil_out(os_, ls_, xp.reshape(B, T, D), dil_w_out_b, li, expand, 512).reshape(MP, D)
            cache_h.append(hs[1].reshape(B, T, D))
            cache_gain.append(kgain)
            qkvs = norm_matmul_qk(xs.reshape(1, MS, D), gmix, dil_w_in_b, li, cg, blockdiag, F32, MS, 1024)
            osamp = dil_sample_attention(li, qkvs.reshape(DB, S, -1), k_caches, v_caches, s_bias_buf, s_bias_new)
            qkvs = qkvs.reshape(DB, S, N_GROUPS, 3, DIL_HEADS, DIL_HD)
            for g in range(N_GROUPS):
                kq[g].append(qkvs[:, :, g, 1])
                vq[g].append(qkvs[:, :, g, 2])
            osamp = osamp.reshape(MS, DIL_WIDTH).astype(BF16)
            xs = matmul_residual(osamp, dil_w_out_b, li, xs, MS, D)

        gffn = norm_ffn[i][None, :].astype(F32)
        cw = ffn_conv_w[i].astype(F32)
        cb = ffn_conv_b[i][None, :].astype(F32)
        act, tail = ffn_up_prompt(xp.reshape(B, T, D), gffn, ffn_w_up_b, i, cw, cb, 512)
        conv_p.append(tail[:, 8 - (CONV_WIDTH - 1):])
        xp = matmul_residual(act.reshape(MP, D_FF), ffn_w_down_b, i, xp, 1024, D)
        buf = state_ffn_conv[i].astype(F32)
        zeros = jnp.zeros((DB, S - 1, 2 * D_FF), F32)
        carry1 = jnp.concatenate([buf[:, 1:2], zeros], axis=1).reshape(MS, 2 * D_FF)
        carry2 = jnp.concatenate([buf, zeros[:, 1:]], axis=1).reshape(MS, 2 * D_FF)
        acts, us = ffn_up_sample(xs, gffn, ffn_w_up_b, i, cw, cb, carry1, carry2, S)
        conv_s.append(us.reshape(DB, S, 2 * D_FF)[:, S - (CONV_WIDTH - 1):])
        xs = matmul_residual(acts, ffn_w_down_b, i, xs, MS, D)

    outs = [xp.reshape(B, T, D), xs.reshape(DB, S, D), jnp.stack(gla_p), jnp.stack(gla_s)]
    pmajor = lambda c: c.reshape(c.shape[0], B, DIL_HEADS, DIL_HD, c.shape[-1]).transpose(0, 1, 4, 2, 3)
    gains = jnp.stack(cache_gain)
    for g, (W, _) in enumerate(DIL_GROUPS):
        keep = min(W, T)
        kc = cache_rows(cache_h, dil_w_in_t, 3 * g + 1, gains, blockdiag, True, keep, min(keep, 512))
        vc = cache_rows(cache_h, dil_w_in_t, 3 * g + 2, gains, blockdiag, False, keep, min(keep, 512))
        outs += [pmajor(kc), jnp.stack(kq[g]), pmajor(vc), jnp.stack(vq[g])]
    outs += [jnp.stack(conv_p), jnp.stack(conv_s)]
    return tuple(outs)
```

```python
import functools
import math

import numpy as np
import jax
import jax.numpy as jnp
from jax import lax
from jax.experimental import pallas as pl
from jax.experimental.pallas import tpu as pltpu

F32 = jnp.float32
BF16 = jnp.bfloat16

D_MODEL = 1024
DEPTH = 4
N_MIXERS = 2
GLA_HEADS = 4
GLA_KD = 512
GLA_VD = 1024
GLA_DK = 128
GLA_DV = 256
GLA_GATE_RANK = 16
GLA_GATE_NORM = 16.0
GLA_CHUNK = 32
DIL_GROUPS = ((128, 1), (512, 4), (2048, 16))
N_GROUPS = 3
DIL_HEADS = 16
DIL_HD = 64
DIL_WIDTH = 1024
DIL_BLOCK = 128
DIL_SCALE = DIL_HD ** -0.5
NUM_BUCKETS = 32
MAX_DISTANCE = 2048
D_FF = 2816
CONV_WIDTH = 3
EPS = 1e-6
NEG = -1e30

LANES = 128
MXU_WIDTH = 256
VMEM_LIMIT = 48 * 1024 * 1024

_NT = (((1,), (1,)), ((), ()))
_TN = (((0,), (0,)), ((), ()))


def _params(*sem):
    return pltpu.CompilerParams(dimension_semantics=sem, vmem_limit_bytes=VMEM_LIMIT)


def _rms(x, g):
    return x * lax.rsqrt(jnp.mean(x * x, axis=-1, keepdims=True) + EPS) * g


def _silu(x):
    return x * (1.0 / (1.0 + jnp.exp(-x)))


def _cast_body(w_ref, o_ref):
    o_ref[...] = w_ref[...].astype(o_ref.dtype)


def to_bf16(w, rows):
    L, R, C = w.shape
    return pl.pallas_call(
        _cast_body,
        grid=(L, R // rows),
        in_specs=[pl.BlockSpec((None, rows, C), lambda l, i: (l, i, 0))],
        out_specs=pl.BlockSpec((None, rows, C), lambda l, i: (l, i, 0)),
        out_shape=jax.ShapeDtypeStruct((L, R, C), BF16),
        compiler_params=_params("parallel", "parallel"),
        name="to_bf16",
    )(w)


def _glain_body(x_ref, g_ref, w_ref, wgz_ref, wg2_ref, b_ref, o_ref, gl_ref, h_ref):
    @pl.when(pl.program_id(1) == 0)
    def _():
        h = _rms(x_ref[...], g_ref[...]).astype(BF16)
        h_ref[...] = h
        gz = jnp.dot(h, wgz_ref[...], preferred_element_type=F32)
        z = jnp.dot(gz.astype(BF16), wg2_ref[...], preferred_element_type=F32) + b_ref[...]
        gl_ref[...] = (jnp.minimum(z, 0.0) - jnp.log(1.0 + jnp.exp(-jnp.abs(z)))) * (1.0 / GLA_GATE_NORM)

    o_ref[...] = jnp.dot(h_ref[...], w_ref[...], preferred_element_type=F32).astype(o_ref.dtype)


def gla_in_proj(x, g, w, layer, ncols, wgz, wg2, b, tm, tn):
    M, D = x.shape
    return pl.pallas_call(
        _glain_body,
        grid=(M // tm, ncols // tn),
        in_specs=[pl.BlockSpec((tm, D), lambda i, j: (i, 0)),
                  pl.BlockSpec((1, D), lambda i, j: (0, 0)),
                  pl.BlockSpec((None, D, tn), lambda i, j: (layer, 0, j)),
                  pl.BlockSpec((D, LANES), lambda i, j: (0, 0)),
                  pl.BlockSpec((LANES, GLA_KD), lambda i, j: (0, 0)),
                  pl.BlockSpec((1, GLA_KD), lambda i, j: (0, 0))],
        out_specs=[pl.BlockSpec((tm, tn), lambda i, j: (i, j)),
                   pl.BlockSpec((tm, GLA_KD), lambda i, j: (i, 0))],
        out_shape=[jax.ShapeDtypeStruct((M, ncols), BF16),
                   jax.ShapeDtypeStruct((M, GLA_KD), F32)],
        scratch_shapes=[pltpu.VMEM((tm, D), BF16)],
        compiler_params=_params("parallel", "arbitrary"),
        name="gla_in_proj",
    )(x, g, w, wgz, wg2, b)


PERM_ROWS = 16
QK_ROWBLOCK = 512


def _nmqk_body(x_ref, g_ref, w_ref, cg_ref, bd_ref, perm_ref, o_ref, h_ref, *, tm, tn, d):
    j = pl.program_id(2)
    rows = tm // d

    @pl.when(j == 0)
    def _():
        h = _rms(x_ref[...], g_ref[...]).astype(BF16)
        if d == 1:
            h_ref[...] = h
        else:
            sub = PERM_ROWS * d
            for s in range(tm // sub):
                hs = jnp.dot(perm_ref[...], h[s * sub:(s + 1) * sub], preferred_element_type=F32).astype(BF16)
                for r in range(d):
                    dst = r * rows + PERM_ROWS * s
                    h_ref[dst:dst + PERM_ROWS, :] = hs[r * PERM_ROWS:(r + 1) * PERM_ROWS]

    is_norm = ((j * tn) // DIL_WIDTH) % 3 != 2

    rb = min(QK_ROWBLOCK, tm)

    def emit(a, cs, y):
        yb = y.astype(o_ref.dtype)
        for r in range(d):
            lo, hi = max(a, r * rows), min(a + rb, (r + 1) * rows)
            if lo < hi:
                o_ref[r, lo - r * rows:hi - r * rows, cs] = yb[lo - a:hi - a]

    def chunks(norm):
        blocks = [(slice(c * MXU_WIDTH, (c + 1) * MXU_WIDTH), a)
                  for c in range(tn // MXU_WIDTH) for a in range(0, tm, rb)]
        proj = lambda cs, a: jnp.dot(h_ref[a:a + rb, :], w_ref[:, cs], preferred_element_type=F32)
        ahead = proj(*blocks[0])
        for n, (cs, a) in enumerate(blocks):
            pc = ahead
            if n + 1 < len(blocks):
                ahead = proj(*blocks[n + 1])
            if norm:
                ss = jnp.dot((pc * pc).astype(BF16), bd_ref[...], preferred_element_type=F32)
                pc = pc * lax.rsqrt(ss * (1.0 / DIL_HD) + EPS) * cg_ref[:, cs]
            emit(a, cs, pc)

    @pl.when(is_norm)
    def _():
        chunks(True)

    @pl.when(jnp.logical_not(is_norm))
    def _():
        chunks(False)


def norm_matmul_qk(x, g, w, layer, colgain, bd, out_dtype, tm, tn, d=1, group=None):
    B, T, D = x.shape
    N = w.shape[2] if group is None else 3 * DIL_WIDTH
    j0 = 0 if group is None else group * N // tn
    sub = PERM_ROWS * d
    return pl.pallas_call(
        functools.partial(_nmqk_body, tm=tm, tn=tn, d=d),
        grid=(B, T // tm, N // tn),
        in_specs=[pl.BlockSpec((None, tm, D), lambda b, i, j: (b, i, 0)),
                  pl.BlockSpec((1, D), lambda b, i, j: (0, 0)),
                  pl.BlockSpec((None, D, tn), lambda b, i, j: (layer, 0, j + j0)),
                  pl.BlockSpec((1, tn), lambda b, i, j: (0, j + j0)),
                  pl.BlockSpec((MXU_WIDTH, MXU_WIDTH), lambda b, i, j: (0, 0)),
                  pl.BlockSpec((sub, sub), lambda b, i, j: (0, 0))],
        out_specs=pl.BlockSpec((None, d, tm // d, tn), lambda b, i, j: (b, 0, i, j)),
        out_shape=jax.ShapeDtypeStruct((B, d, T // d, N), out_dtype),
        scratch_shapes=[pltpu.VMEM((tm, D), BF16)],
        compiler_params=_params("parallel", "parallel", "arbitrary"),
        name="norm_matmul_qk",
    )(x, g, w, colgain, bd, _perm_matrix(d))


def _perm_matrix(d, pr=PERM_ROWS):
    sub = pr * d
    pm = np.zeros((sub, sub), np.float32)
    for r in range(d):
        for i in range(pr):
            pm[r * pr + i, i * d + r] = 1.0
    return jnp.asarray(pm, BF16)


def _prep_body(x_ref, g_ref, *refs, tm, ds):
    perm_refs, o_refs = refs[:len(ds)], refs[len(ds):]
    h = _rms(x_ref[...], g_ref[...]).astype(BF16)
    for d, perm_ref, o_ref in zip(ds, perm_refs, o_refs):
        if d == 1:
            o_ref[0] = h
            continue
        sub = PERM_ROWS * d
        for s in range(tm // sub):
            hs = jnp.dot(perm_ref[...], h[s * sub:(s + 1) * sub], preferred_element_type=F32).astype(BF16)
            for r in range(d):
                o_ref[r, PERM_ROWS * s:PERM_ROWS * (s + 1), :] = hs[r * PERM_ROWS:(r + 1) * PERM_ROWS]


def rms_prep(x, g, ds, tm):
    B, T, D = x.shape
    return pl.pallas_call(
        functools.partial(_prep_body, tm=tm, ds=ds),
        grid=(B, T // tm),
        in_specs=[pl.BlockSpec((None, tm, D), lambda b, i: (b, i, 0)),
                  pl.BlockSpec((1, D), lambda b, i: (0, 0))]
                 + [pl.BlockSpec((PERM_ROWS * d, PERM_ROWS * d), lambda b, i: (0, 0)) for d in ds],
        out_specs=[pl.BlockSpec((None, d, tm // d, D), lambda b, i: (b, 0, i, 0)) for d in ds],
        out_shape=[jax.ShapeDtypeStruct((B, d, T // d, D), BF16) for d in ds],
        compiler_params=_params("parallel", "parallel"),
        name="rms_prep",
    )(x, g, *[_perm_matrix(d) for d in ds])


def _projrows_body(h_ref, w_ref, cg_ref, bd_ref, o_ref, *, tm, norm):
    rb = min(QK_ROWBLOCK, tm)
    blocks = [(slice(c * MXU_WIDTH, (c + 1) * MXU_WIDTH), a)
              for c in range(DIL_WIDTH // MXU_WIDTH) for a in range(0, tm, rb)]
    proj = lambda cs, a: jnp.dot(h_ref[a:a + rb, :], w_ref[:, cs], preferred_element_type=F32)
    ahead = proj(*blocks[0])
    for n, (cs, a) in enumerate(blocks):
        pc = ahead
        if n + 1 < len(blocks):
            ahead = proj(*blocks[n + 1])
        if norm:
            ss = jnp.dot((pc * pc).astype(BF16), bd_ref[...], preferred_element_type=F32)
            pc = pc * lax.rsqrt(ss * (1.0 / DIL_HD) + EPS) * cg_ref[:, cs]
        o_ref[a:a + rb, cs] = pc.astype(o_ref.dtype)


def proj_rows(h, w, layer, seg, colgain, bd, norm, tm):
    M, D = h.shape
    return pl.pallas_call(
        functools.partial(_projrows_body, tm=tm, norm=norm),
        grid=(M // tm,),
        in_specs=[pl.BlockSpec((tm, D), lambda i: (i, 0)),
                  pl.BlockSpec((None, D, DIL_WIDTH), lambda i: (layer, 0, seg)),
                  pl.BlockSpec((1, DIL_WIDTH), lambda i: (0, seg)),
                  pl.BlockSpec((MXU_WIDTH, MXU_WIDTH), lambda i: (0, 0))],
        out_specs=pl.BlockSpec((tm, DIL_WIDTH), lambda i: (i, 0)),
        out_shape=jax.ShapeDtypeStruct((M, DIL_WIDTH), BF16),
        compiler_params=_params("parallel"),
        name="proj_rows",
    )(h, w, colgain, bd)


def _projcols_body(h_ref, wt_ref, cg_ref, bd_ref, o_ref, *, tl, norm, blocked):
    _projcols_core(h_ref[...], wt_ref, cg_ref, bd_ref, o_ref, tl, norm, blocked)


def _cacherows_body(ha_ref, hb_ref, wt_ref, cg_ref, bd_ref, o_ref, *, tl, norm):
    h = jnp.where(pl.program_id(0) == 0, ha_ref[...], hb_ref[...])
    _projcols_core(h, wt_ref, cg_ref, bd_ref, o_ref, tl, norm, False)


def cache_rows(hs, wt, seg, rowgains, bd, norm, keep, tl):
    B, T, D = hs[0].shape
    first, count = (T - keep) // tl, keep // tl
    hspec = pl.BlockSpec((None, tl, D), lambda l, g, i: (g, first + i, 0))
    return pl.pallas_call(
        functools.partial(_cacherows_body, tl=tl, norm=norm),
        grid=(len(hs), B, count),
        in_specs=[hspec, hspec,
                  pl.BlockSpec((None, DIL_WIDTH, D), lambda l, g, i: (l, seg, 0)),
                  pl.BlockSpec((None, DIL_WIDTH, LANES), lambda l, g, i: (l, 0, 0)),
                  pl.BlockSpec((MXU_WIDTH, MXU_WIDTH), lambda l, g, i: (0, 0))],
        out_specs=pl.BlockSpec((None, None, DIL_WIDTH, tl), lambda l, g, i: (l, g, 0, i)),
        out_shape=jax.ShapeDtypeStruct((len(hs), B, DIL_WIDTH, keep), F32),
        compiler_params=_params("parallel", "parallel", "parallel"),
        name="cache_rows",
    )(hs[0], hs[1], wt, rowgains, bd)


def _projcols_core(h, wt_ref, cg_ref, bd_ref, o_ref, tl, norm, blocked):
    kt = lax.dot_general(wt_ref[...], h, _NT, preferred_element_type=F32)
    for fb in range(DIL_WIDTH // MXU_WIDTH):
        fs = slice(fb * MXU_WIDTH, (fb + 1) * MXU_WIDTH)
        blk = kt[fs]
        if norm:
            ss = jnp.dot(bd_ref[...], (blk * blk).astype(BF16), preferred_element_type=F32)
            gain = jnp.concatenate([cg_ref[fs, :]] * (tl // LANES), axis=1)
            blk = blk * lax.rsqrt(ss * (1.0 / DIL_HD) + EPS) * gain
        blk = blk.astype(o_ref.dtype)
        if blocked:
            for jb in range(tl // DIL_BLOCK):
                o_ref[jb, fs, :] = blk[:, jb * DIL_BLOCK:(jb + 1) * DIL_BLOCK]
        else:
            o_ref[fs, :] = blk


def proj_cols(h, wt, layer, seg, rowgain, bd, norm, out_dtype, tl, first=0, count=None, blocked=False):
    G, L, D = h.shape
    count = L // tl if count is None else count
    if blocked:
        nb = tl // DIL_BLOCK
        out_spec = pl.BlockSpec((None, nb, DIL_WIDTH, DIL_BLOCK), lambda g, i: (g, i, 0, 0))
        out_shape = jax.ShapeDtypeStruct((G, count * nb, DIL_WIDTH, DIL_BLOCK), out_dtype)
    else:
        out_spec = pl.BlockSpec((None, DIL_WIDTH, tl), lambda g, i: (g, 0, i))
        out_shape = jax.ShapeDtypeStruct((G, DIL_WIDTH, count * tl), out_dtype)
    return pl.pallas_call(
        functools.partial(_projcols_body, tl=tl, norm=norm, blocked=blocked),
        grid=(G, count),
        in_specs=[pl.BlockSpec((None, tl, D), lambda g, i: (g, first + i, 0)),
                  pl.BlockSpec((None, DIL_WIDTH, D), lambda g, i: (layer, seg, 0)),
                  pl.BlockSpec((DIL_WIDTH, LANES), lambda g, i: (0, 0)),
                  pl.BlockSpec((MXU_WIDTH, MXU_WIDTH), lambda g, i: (0, 0))],
        out_specs=out_spec,
        out_shape=out_shape,
        compiler_params=_params("parallel", "parallel"),
        name="proj_cols",
    )(h, wt, rowgain, bd)


def _mmres_body(y_ref, w_ref, x_ref, o_ref):
    o_ref[...] = x_ref[...] + jnp.dot(y_ref[...], w_ref[...], preferred_element_type=F32)


def matmul_residual(y, w, layer, x, tm, tn):
    M, K = y.shape
    N = w.shape[2]
    return pl.pallas_call(
        _mmres_body,
        grid=(M // tm, N // tn),
        in_specs=[pl.BlockSpec((tm, K), lambda i, j: (i, 0)),
                  pl.BlockSpec((None, K, tn), lambda i, j: (layer, 0, j)),
                  pl.BlockSpec((tm, tn), lambda i, j: (i, j))],
        out_specs=pl.BlockSpec((tm, tn), lambda i, j: (i, j)),
        out_shape=jax.ShapeDtypeStruct((M, N), F32),
        compiler_params=_params("parallel", "parallel"),
        name="matmul_residual",
    )(y, w, x)


def _gla_body(q_ref, k_ref, v_ref, r_ref, gl_ref, s0_ref, gn_ref, tri_ref, y_ref, st_ref, S_ref, *, TB, CH):
    c = pl.program_id(1)

    @pl.when(c == 0)
    def _():
        S_ref[...] = s0_ref[...]

    gl = gl_ref[...]
    g1 = gl.astype(BF16)
    r1 = gl - g1.astype(F32)
    g2 = r1.astype(BF16)
    g3 = (r1 - g2.astype(F32)).astype(BF16)
    tri = tri_ref[...]
    bfull = (jnp.dot(tri, g1, preferred_element_type=F32)
             + jnp.dot(tri, g2, preferred_element_type=F32)
             + jnp.dot(tri, g3, preferred_element_type=F32))

    H = GLA_HEADS
    R = H * CH
    own = (lax.broadcasted_iota(jnp.int32, (R, GLA_KD), 0) // CH
           == lax.broadcasted_iota(jnp.int32, (R, GLA_KD), 1) // GLA_DK)
    arow = lax.broadcasted_iota(jnp.int32, (R, R), 0)
    acol = lax.broadcasted_iota(jnp.int32, (R, R), 1)
    amask = (arow // CH == acol // CH) & (arow % CH >= acol % CH)
    heads = lambda a: jnp.concatenate([a] * H, axis=0)
    split = lambda ref, rs: jnp.concatenate([ref[rs, h * GLA_DV:(h + 1) * GLA_DV] for h in range(H)], axis=0)
    mid = CH // 2
    gn = gn_ref[...]
    S = S_ref[...]
    for sc in range(TB // CH):
        rs = slice(sc * CH, (sc + 1) * CH)
        b = bfull[rs]
        if sc > 0:
            b = b - bfull[sc * CH - 1:sc * CH]
        ref = b[mid:mid + 1]
        blast = b[CH - 1:CH]
        qf = q_ref[rs, :].astype(F32) * (GLA_DK ** -0.5)
        kf = k_ref[rs, :].astype(F32)
        qe = (qf * jnp.exp(b - ref)).astype(BF16)
        ke = jnp.where(own, heads(kf * jnp.exp(ref - b)), 0.0).astype(BF16)
        kd = jnp.where(own, heads(kf * jnp.exp(blast - b)), 0.0).astype(BF16)
        qb = jnp.where(own, heads(qf * jnp.exp(b)), 0.0).astype(BF16)
        vst = split(v_ref, rs)
        a = heads(lax.dot_general(qe, ke, _NT, preferred_element_type=F32))
        a = jnp.where(amask, a, 0.0).astype(BF16)
        o = (jnp.dot(a, vst, preferred_element_type=F32)
             + lax.dot_general(qb, S.astype(BF16), _NT, preferred_element_type=F32))
        S = jnp.exp(blast) * S + lax.dot_general(vst, kd, _TN, preferred_element_type=F32)
        y = (_rms(o, gn) * _silu(split(r_ref, rs).astype(F32))).astype(y_ref.dtype)
        for h in range(H):
            y_ref[rs, h * GLA_DV:(h + 1) * GLA_DV] = y[h * CH:(h + 1) * CH]
    S_ref[...] = S

    @pl.when(c == pl.num_programs(1) - 1)
    def _():
        st_ref[...] = S_ref[...]


def _pack_state(s):
    return s.transpose(0, 3, 1, 2).reshape(s.shape[0], GLA_DV, GLA_HEADS * GLA_DK)


def _unpack_state(s):
    return s.reshape(s.shape[0], GLA_DV, GLA_HEADS, GLA_DK).transpose(0, 2, 3, 1)


def gla_scan(p, glog, s0t, gn, TB, CH):
    B, T, _ = p.shape
    sblk = (None, GLA_DV, GLA_HEADS * GLA_DK)
    tri = jnp.asarray(np.tril(np.ones((TB, TB), np.float32)), BF16)
    return pl.pallas_call(
        functools.partial(_gla_body, TB=TB, CH=CH),
        grid=(B, T // TB),
        in_specs=[pl.BlockSpec((None, TB, GLA_KD), lambda b, c: (b, c, 0)),
                  pl.BlockSpec((None, TB, GLA_KD), lambda b, c: (b, c, 1)),
                  pl.BlockSpec((None, TB, GLA_VD), lambda b, c: (b, c, 1)),
                  pl.BlockSpec((None, TB, GLA_VD), lambda b, c: (b, c, 2)),
                  pl.BlockSpec((None, TB, GLA_KD), lambda b, c: (b, c, 0)),
                  pl.BlockSpec(sblk, lambda b, c: (b, 0, 0)),
                  pl.BlockSpec((1, GLA_DV), lambda b, c: (0, 0)),
                  pl.BlockSpec((TB, TB), lambda b, c: (0, 0))],
        out_specs=[pl.BlockSpec((None, TB, GLA_VD), lambda b, c: (b, c, 0)),
                   pl.BlockSpec(sblk, lambda b, c: (b, 0, 0))],
        out_shape=[jax.ShapeDtypeStruct((B, T, GLA_VD), BF16),
                   jax.ShapeDtypeStruct((B,) + sblk[1:], F32)],
        scratch_shapes=[pltpu.VMEM(sblk[1:], F32)],
        compiler_params=_params("parallel", "arbitrary"),
        name="gla_scan",
    )(p, p, p, p, glog, s0t, gn, tri)


ATTN_QBLOCKS = 2


def _attn_body(q_ref, kp_ref, k0_ref, k1_ref, vp_ref, v0_ref, v1_ref, bias_ref, o_ref, lse_ref):
    krefs = (kp_ref, k0_ref, k1_ref)
    vrefs = (vp_ref, v0_ref, v1_ref)
    lane = lax.broadcasted_iota(jnp.int32, (DIL_BLOCK, LANES), 1)
    lo = lane < DIL_HD
    zero = jnp.zeros((), BF16)
    for sb in range(ATTN_QBLOCKS):
        rows = slice(sb * DIL_BLOCK, (sb + 1) * DIL_BLOCK)
        kprev, kcur = krefs[sb], krefs[sb + 1]
        vprev, vcur = vrefs[sb], vrefs[sb + 1]
        var = jnp.where(pl.program_id(2) == 0, 0, 1) if sb == 0 else 1

        def scores(hp):
            cs = slice(hp * LANES, (hp + 1) * LANES)
            q2 = q_ref[rows, cs]
            qab = jnp.concatenate([jnp.where(lo, q2, zero), jnp.where(lo, zero, q2)], axis=0)
            k2t = jnp.concatenate([kprev[cs, :], kcur[cs, :]], axis=1)
            return jnp.dot(qab, k2t, preferred_element_type=F32)

        lse_acc = jnp.zeros((DIL_BLOCK, LANES), F32)
        ahead = scores(0)
        for hp in range(DIL_HEADS // 2):
            cs = slice(hp * LANES, (hp + 1) * LANES)
            s = ahead + bias_ref[var, hp]
            if hp + 1 < DIL_HEADS // 2:
                ahead = scores(hp + 1)
            v2 = jnp.concatenate([vprev[:, cs], vcur[:, cs]], axis=0)
            m = jnp.max(s, axis=-1, keepdims=True)
            p = jnp.exp(s - m)
            l = jnp.sum(p, axis=-1, keepdims=True)
            o = jnp.dot(p.astype(BF16), v2, preferred_element_type=F32) * (1.0 / l)
            o_ref[rows, cs] = jnp.where(lo, o[:DIL_BLOCK], o[DIL_BLOCK:]).astype(o_ref.dtype)
            lse = m + jnp.log(l)
            lse_acc = jnp.where(lane == 2 * hp, lse[:DIL_BLOCK], lse_acc)
            lse_acc = jnp.where(lane == 2 * hp + 1, lse[DIL_BLOCK:], lse_acc)
        lse_ref[rows, :] = lse_acc


def dil_attention(q, kt, v, bias):
    B, d, L, _ = q.shape
    nq = ATTN_QBLOCKS
    nb = L // DIL_BLOCK
    assert nb % nq == 0
    qblk = (None, None, nq * DIL_BLOCK, DIL_WIDTH)
    blk = (None, None, DIL_BLOCK, DIL_WIDTH)
    tblk = (None, None, None, DIL_WIDTH, DIL_BLOCK)
    prev = lambda n: jnp.maximum(nq * n - 1, 0)
    return pl.pallas_call(
        _attn_body,
        grid=(B, d, nb // nq),
        in_specs=[pl.BlockSpec(qblk, lambda b, r, n: (b, r, n, 0)),
                  pl.BlockSpec(tblk, lambda b, r, n: (b, r, prev(n), 0, 0)),
                  pl.BlockSpec(tblk, lambda b, r, n: (b, r, nq * n, 0, 0)),
                  pl.BlockSpec(tblk, lambda b, r, n: (b, r, nq * n + 1, 0, 0)),
                  pl.BlockSpec(blk, lambda b, r, n: (b, r, prev(n), 0)),
                  pl.BlockSpec(blk, lambda b, r, n: (b, r, nq * n, 0)),
                  pl.BlockSpec(blk, lambda b, r, n: (b, r, nq * n + 1, 0)),
                  pl.BlockSpec((2, DIL_HEADS // 2, 2 * DIL_BLOCK, 2 * DIL_BLOCK), lambda b, r, n: (0, 0, 0, 0))],
        out_specs=[pl.BlockSpec(qblk, lambda b, r, n: (b, r, n, 0)),
                   pl.BlockSpec((None, None, nq * DIL_BLOCK, LANES), lambda b, r, n: (b, r, n, 0))],
        out_shape=[jax.ShapeDtypeStruct((B, d, L, DIL_WIDTH), BF16),
                   jax.ShapeDtypeStruct((B, d, L, LANES), F32)],
        compiler_params=_params("parallel", "parallel", "arbitrary"),
        name="dil_attention",
    )(q, kt, kt, kt, v, v, v, bias)


UNPERM_SUB = MXU_WIDTH


def _dilout_body(o0_ref, o1_ref, o2_ref, l0_ref, l1_ref, l2_ref, x_ref, w_ref, e_ref, p1_ref, p2_ref, out_ref,
                 *, tm):
    def position_order(o_ref, pt_ref, d):
        if d == 1:
            return o_ref[0].astype(F32)
        pr = UNPERM_SUB // d
        out = []
        for s in range(tm // UNPERM_SUB):
            blk = jnp.concatenate([o_ref[r, pr * s:pr * (s + 1), :] for r in range(d)], axis=0)
            out.append(jnp.dot(pt_ref[...], blk, preferred_element_type=F32))
        return jnp.concatenate(out, axis=0)

    dils = [d for _, d in DIL_GROUPS]
    o0 = position_order(o0_ref, None, dils[0])
    o1 = position_order(o1_ref, p1_ref, dils[1])
    o2 = position_order(o2_ref, p2_ref, dils[2])
    l0, l1, l2 = l0_ref[...], l1_ref[...], l2_ref[...]
    mx = jnp.maximum(jnp.maximum(l0, l1), l2)
    e0, e1, e2 = jnp.exp(l0 - mx), jnp.exp(l1 - mx), jnp.exp(l2 - mx)
    den = e0 + e1 + e2
    ex = e_ref[...]

    def expand(w):
        w1 = w.astype(BF16)
        w2 = (w - w1.astype(F32)).astype(BF16)
        return jnp.dot(jnp.concatenate([w1, w2], axis=1), ex, preferred_element_type=F32)

    o = expand(e0 / den) * o0 + expand(e1 / den) * o1 + expand(e2 / den) * o2
    out_ref[...] = x_ref[...] + jnp.dot(o.astype(BF16), w_ref[...], preferred_element_type=F32)


def dil_out(os, ls, x, w, layer, expand, tm):
    B, T, D = x.shape
    dils = [d for _, d in DIL_GROUPS]
    ospecs = [pl.BlockSpec((None, d, tm // d, DIL_WIDTH), lambda b, i: (b, 0, i, 0)) for d in dils]
    lspec = pl.BlockSpec((None, tm, LANES), lambda b, i: (b, i, 0))
    pts = [jnp.transpose(_perm_matrix(d, UNPERM_SUB // d)) for d in dils[1:]]
    return pl.pallas_call(
        functools.partial(_dilout_body, tm=tm),
        grid=(B, T // tm),
        in_specs=ospecs + [lspec, lspec, lspec,
                           pl.BlockSpec((None, tm, D), lambda b, i: (b, i, 0)),
                           pl.BlockSpec((None, DIL_WIDTH, D_MODEL), lambda b, i: (layer, 0, 0)),
                           pl.BlockSpec((2 * LANES, DIL_WIDTH), lambda b, i: (0, 0))]
                 + [pl.BlockSpec(p.shape, lambda b, i: (0, 0)) for p in pts],
        out_specs=pl.BlockSpec((None, tm, D_MODEL), lambda b, i: (b, i, 0)),
        out_shape=jax.ShapeDtypeStruct((B, T, D_MODEL), F32),
        compiler_params=_params("parallel", "parallel"),
        name="dil_out",
    )(*os, *ls, x, w, expand, *pts)


S_CHUNK = 1024
S_NEWPAD = LANES


def _sattn_body(qkv_ref, k0_ref, v0_ref, k1_ref, v1_ref, k2_ref, v2_ref, b0_ref, b1_ref, b2_ref, bn_ref,
                o_ref, m_ref, l_ref, acc_ref, *, seq):
    c = pl.program_id(1)
    R = seq * DIL_HEADS
    rowh = lax.broadcasted_iota(jnp.int32, (R, DIL_WIDTH), 0) % DIL_HEADS
    colh = lax.broadcasted_iota(jnp.int32, (R, DIL_WIDTH), 1) // DIL_HD
    hmask = rowh == colh

    def seg(g, part):
        return qkv_ref[:, (3 * g + part) * DIL_WIDTH:(3 * g + part + 1) * DIL_WIDTH]

    def qbd(g):
        q = seg(g, 0)
        qrep = jnp.concatenate([jnp.broadcast_to(q[s:s + 1], (DIL_HEADS, DIL_WIDTH)) for s in range(seq)], axis=0)
        return jnp.where(hmask, qrep, 0.0).astype(BF16)

    def update(parts):
        m_old = m_ref[...]
        m_new = m_old
        for s, _ in parts:
            m_new = jnp.maximum(m_new, jnp.max(s, axis=-1, keepdims=True))
        alpha = jnp.exp(m_old - m_new)
        l = alpha * l_ref[...]
        acc = alpha * acc_ref[...]
        for s, pv_fn in parts:
            p = jnp.exp(s - m_new)
            l = l + jnp.sum(p, axis=-1, keepdims=True)
            acc = acc + pv_fn(p.astype(BF16))
        m_ref[...] = m_new
        l_ref[...] = l
        acc_ref[...] = acc

    def cache_segment(g, kt_ref, vt_ref, b_ref):
        s = jnp.dot(qbd(g), kt_ref[...].astype(BF16), preferred_element_type=F32) + b_ref[...]
        return s, lambda p: lax.dot_general(p, vt_ref[...].astype(BF16), _NT, preferred_element_type=F32)

    def new_segment(g):
        pad = jnp.zeros((S_NEWPAD - seq, DIL_WIDTH), F32)
        kn = jnp.concatenate([seg(g, 1), pad], axis=0).astype(BF16)
        vn = jnp.concatenate([seg(g, 2), pad], axis=0).astype(BF16)
        s = lax.dot_general(qbd(g), kn, _NT, preferred_element_type=F32) + bn_ref[g]
        return s, lambda p: jnp.dot(p, vn, preferred_element_type=F32)

    @pl.when(c == 0)
    def _():
        m_ref[...] = jnp.full(m_ref.shape, 2 * NEG, F32)
        l_ref[...] = jnp.zeros(l_ref.shape, F32)
        acc_ref[...] = jnp.zeros(acc_ref.shape, F32)
        update([new_segment(g) for g in range(N_GROUPS)]
               + [cache_segment(0, k0_ref, v0_ref, b0_ref), cache_segment(1, k1_ref, v1_ref, b1_ref),
                  cache_segment(2, k2_ref, v2_ref, b2_ref)])

    @pl.when(c > 0)
    def _():
        update([cache_segment(2, k2_ref, v2_ref, b2_ref)])

    @pl.when(c == pl.num_programs(1) - 1)
    def _():
        on = jnp.where(hmask, acc_ref[...] / l_ref[...], 0.0).astype(BF16)
        srow = lax.broadcasted_iota(jnp.int32, (8, R), 0)
        scol = lax.broadcasted_iota(jnp.int32, (8, R), 1) // DIL_HEADS
        sel = jnp.where(srow == scol, 1.0, 0.0).astype(BF16)
        o_ref[...] = jnp.dot(sel, on, preferred_element_type=F32)[:seq]


def dil_sample_attention(li, qkv, kts, vts, bbs, bn):
    DB, S, _ = qkv.shape
    R = DIL_HEADS * S
    W2 = kts[2].shape[-1]
    in_specs = [pl.BlockSpec((None, S, qkv.shape[-1]), lambda b, c: (b, 0, 0))]
    args = [qkv]
    for g in range(2):
        for a in (kts[g], vts[g]):
            in_specs.append(pl.BlockSpec((None, None, DIL_WIDTH, a.shape[-1]), lambda b, c: (li, b, 0, 0)))
            args.append(a)
    for a in (kts[2], vts[2]):
        in_specs.append(pl.BlockSpec((None, None, DIL_WIDTH, S_CHUNK), lambda b, c: (li, b, 0, c)))
        args.append(a)
    for g in range(2):
        in_specs.append(pl.BlockSpec(bbs[g].shape, lambda b, c: (0, 0)))
        args.append(bbs[g])
    in_specs.append(pl.BlockSpec((R, S_CHUNK), lambda b, c: (0, c)))
    args.append(bbs[2])
    in_specs.append(pl.BlockSpec(bn.shape, lambda b, c: (0, 0, 0)))
    args.append(bn)
    return pl.pallas_call(
        functools.partial(_sattn_body, seq=S),
        grid=(DB, W2 // S_CHUNK),
        in_specs=in_specs,
        out_specs=pl.BlockSpec((None, S, DIL_WIDTH), lambda b, c: (b, 0, 0)),
        out_shape=jax.ShapeDtypeStruct((DB, S, DIL_WIDTH), F32),
        scratch_shapes=[pltpu.VMEM((R, 1), F32), pltpu.VMEM((R, 1), F32), pltpu.VMEM((R, DIL_WIDTH), F32)],
        compiler_params=_params("parallel", "arbitrary"),
        name="dil_sample_attention",
    )(*args)


FFN_CHUNK = MXU_WIDTH
HALO = 16


FFN_ROWBLOCK = 128


def _causal_conv(ext, cw, cb):
    u1 = pltpu.roll(ext, 1, axis=0)[HALO:]
    u2 = pltpu.roll(ext, 2, axis=0)[HALO:]
    return cb + (cw[0:1] * u2 + cw[1:2] * u1 + cw[2:3] * ext[HALO:])


def _ffnup_body(x_ref, xh_ref, g_ref, wg_ref, wv_ref, cw_ref, cb_ref, act_ref, tail_ref, h_ref, *, tm):
    first = pl.program_id(1) == 0
    hh = _rms(xh_ref[...], g_ref[...])
    h_ref[0:HALO] = jnp.where(first, 0.0, hh).astype(BF16)
    h_ref[HALO:HALO + tm] = _rms(x_ref[...], g_ref[...]).astype(BF16)
    for c in range(D_FF // FFN_CHUNK):
        cols = (slice(c * FFN_CHUNK, (c + 1) * FFN_CHUNK),
                slice(D_FF + c * FFN_CHUNK, D_FF + (c + 1) * FFN_CHUNK))
        tails = [None, None]
        for a in range(0, tm, FFN_ROWBLOCK):
            conv = []
            for idx, w_ref in enumerate((wg_ref, wv_ref)):
                if a == 0:
                    ext = jnp.dot(h_ref[0:HALO + FFN_ROWBLOCK], w_ref[:, cols[0]], preferred_element_type=F32)
                else:
                    u = jnp.dot(h_ref[HALO + a:HALO + a + FFN_ROWBLOCK], w_ref[:, cols[0]],
                                preferred_element_type=F32)
                    ext = jnp.concatenate([tails[idx], u], axis=0)
                tails[idx] = ext[FFN_ROWBLOCK:]
                conv.append(_causal_conv(ext, cw_ref[:, cols[idx]], cb_ref[:, cols[idx]]))
            act_ref[a:a + FFN_ROWBLOCK, cols[0]] = (_silu(conv[0]) * conv[1]).astype(act_ref.dtype)
        for idx in range(2):
            tail_ref[:, cols[idx]] = tails[idx][HALO - 8:]


def ffn_up_prompt(x, g, w, layer, cw, cb, tm):
    B, T, D = x.shape
    hblk = tm // HALO
    return pl.pallas_call(
        functools.partial(_ffnup_body, tm=tm),
        grid=(B, T // tm),
        in_specs=[pl.BlockSpec((None, tm, D), lambda b, i: (b, i, 0)),
                  pl.BlockSpec((None, HALO, D), lambda b, i: (b, jnp.maximum(i * hblk - 1, 0), 0)),
                  pl.BlockSpec((1, D), lambda b, i: (0, 0)),
                  pl.BlockSpec((None, D, D_FF), lambda b, i: (layer, 0, 0)),
                  pl.BlockSpec((None, D, D_FF), lambda b, i: (layer, 0, 1)),
                  pl.BlockSpec((CONV_WIDTH, 2 * D_FF), lambda b, i: (0, 0)),
                  pl.BlockSpec((1, 2 * D_FF), lambda b, i: (0, 0))],
        out_specs=[pl.BlockSpec((None, tm, D_FF), lambda b, i: (b, i, 0)),
                   pl.BlockSpec((None, 8, 2 * D_FF), lambda b, i: (b, 0, 0))],
        out_shape=[jax.ShapeDtypeStruct((B, T, D_FF), BF16),
                   jax.ShapeDtypeStruct((B, 8, 2 * D_FF), F32)],
        scratch_shapes=[pltpu.VMEM((HALO + tm, D), BF16)],
        compiler_params=_params("parallel", "arbitrary"),
        name="ffn_up_prompt",
    )(x, x, g, w, w, cw, cb)


def _ffnup_s_body(x_ref, g_ref, wg_ref, wv_ref, cw_ref, cb_ref, c1_ref, c2_ref, act_ref, u_ref, u_scr, *, rows, seq):
    h = _rms(x_ref[...], g_ref[...]).astype(BF16)
    pos = lax.broadcasted_iota(jnp.int32, (rows, FFN_CHUNK), 0) % seq
    u_scr[:, 0:HALO] = jnp.zeros((2, HALO, FFN_CHUNK), F32)
    for c in range(D_FF // FFN_CHUNK):
        acts = []
        for idx, off in enumerate((0, D_FF)):
            cs = slice(off + c * FFN_CHUNK, off + (c + 1) * FFN_CHUNK)
            w_ref = wg_ref if idx == 0 else wv_ref
            u = jnp.dot(h, w_ref[:, c * FFN_CHUNK:(c + 1) * FFN_CHUNK], preferred_element_type=F32)
            u_scr[idx, HALO:HALO + rows] = u
            u1 = jnp.where(pos >= 1, u_scr[idx, HALO - 1:HALO - 1 + rows], c1_ref[:, cs])
            u2 = jnp.where(pos >= 2, u_scr[idx, HALO - 2:HALO - 2 + rows], c2_ref[:, cs])
            acts.append(cb_ref[:, cs] + (cw_ref[0:1, cs] * u2 + cw_ref[1:2, cs] * u1 + cw_ref[2:3, cs] * u))
            u_ref[:, cs] = u
        act_ref[:, c * FFN_CHUNK:(c + 1) * FFN_CHUNK] = (_silu(acts[0]) * acts[1]).astype(act_ref.dtype)


def ffn_up_sample(x, g, w, layer, cw, cb, carry1, carry2, seq):
    M, D = x.shape
    full = lambda shape: pl.BlockSpec(shape, lambda i: tuple(0 for _ in shape))
    return pl.pallas_call(
        functools.partial(_ffnup_s_body, rows=M, seq=seq),
        grid=(1,),
        in_specs=[full((M, D)), full((1, D)),
                  pl.BlockSpec((None, D, D_FF), lambda i: (layer, 0, 0)),
                  pl.BlockSpec((None, D, D_FF), lambda i: (layer, 0, 1)),
                  full((CONV_WIDTH, 2 * D_FF)), full((1, 2 * D_FF)),
                  full((M, 2 * D_FF)), full((M, 2 * D_FF))],
        out_specs=[full((M, D_FF)), full((M, 2 * D_FF))],
        out_shape=[jax.ShapeDtypeStruct((M, D_FF), BF16),
                   jax.ShapeDtypeStruct((M, 2 * D_FF), F32)],
        scratch_shapes=[pltpu.VMEM((2, HALO + M, FFN_CHUNK), F32)],
        compiler_params=_params("arbitrary"),
        name="ffn_up_sample",
    )(x, g, w, w, cw, cb, carry1, carry2)


def _rel_bucket(dist):
    max_exact = NUM_BUCKETS // 2
    df = jnp.maximum(dist, 1).astype(F32)
    large = max_exact + (jnp.log(df / max_exact) / math.log(MAX_DISTANCE / max_exact)
                         * (NUM_BUCKETS - max_exact)).astype(jnp.int32)
    large = jnp.minimum(large, NUM_BUCKETS - 1)
    return jnp.where(dist < max_exact, dist, large)


def _step_table(rel_bias, g, d):
    J = DIL_GROUPS[g][0] // d
    tab = rel_bias[:, g * DIL_HEADS:(g + 1) * DIL_HEADS].astype(F32)
    return tab[_rel_bucket(jnp.arange(J + 1) * d)]


def _lookup(table, idx):
    onehot = jax.nn.one_hot(jnp.asarray(idx, jnp.int32), table.shape[0], dtype=F32)
    return jnp.einsum('...j,jh->...h', onehot, table, precision=lax.Precision.HIGHEST)


def _prompt_bias(rel_bias, g, d):
    J = DIL_GROUPS[g][0] // d
    qi = np.arange(DIL_BLOCK)[:, None]
    ki = np.arange(2 * DIL_BLOCK)[None, :]
    rel = qi + DIL_BLOCK - ki
    bias = jnp.moveaxis(_lookup(_step_table(rel_bias, g, d), np.clip(rel, 0, J)), -1, 0)
    out = []
    for first in (True, False):
        valid = (rel >= 0) & (rel <= J) & ((ki >= DIL_BLOCK) | (not first))
        b = jnp.where(jnp.asarray(valid)[None], bias, NEG)
        out.append(b.reshape(DIL_HEADS // 2, 2 * DIL_BLOCK, 2 * DIL_BLOCK))
    return jnp.stack(out)


def _sample_bias(rel_bias, g, d, seq):
    W = DIL_GROUPS[g][0]
    J = W // d
    table = _step_table(rel_bias, g, d)
    s = np.arange(seq)[:, None]
    dist = W + s - np.arange(W)[None, :]
    valid = (dist >= 0) & (dist <= W) & (dist % d == 0)
    bval = jnp.moveaxis(_lookup(table, np.where(valid, dist // d, 0)), -1, 1)
    bbuf = jnp.where(jnp.asarray(valid)[:, None, :], bval, NEG).reshape(seq * DIL_HEADS, W)
    dn = s - np.arange(S_NEWPAD)[None, :]
    vn = (dn >= 0) & (dn % d == 0) & (dn // d <= J) & (np.arange(S_NEWPAD)[None, :] < seq)
    bnew = jnp.moveaxis(_lookup(table, np.where(vn, dn // d, 0)), -1, 1)
    bnew = jnp.where(jnp.asarray(vn)[:, None, :], bnew, NEG).reshape(seq * DIL_HEADS, S_NEWPAD)
    return bbuf, bnew


def _colgain(q_gain, k_gain, ngroups):
    seg = jnp.concatenate([jnp.tile(q_gain.astype(F32) * DIL_SCALE, DIL_HEADS),
                           jnp.tile(k_gain.astype(F32), DIL_HEADS),
                           jnp.ones((DIL_WIDTH,), F32)])
    return jnp.tile(seg, ngroups)[None, :]


def kernel(x_prompt, x_sample, state_gla, cache_k_g0, cache_v_g0, cache_k_g1, cache_v_g1, cache_k_g2, cache_v_g2,
           state_ffn_conv, rel_bias, norm_mix, norm_ffn, gla_w_in, gla_w_gate2, gla_b_gate, gla_norm, gla_w_out,
           dil_w_in, dil_q_norm, dil_k_norm, dil_w_out, ffn_w_up, ffn_conv_w, ffn_conv_b, ffn_w_down):
    B, T, D = x_prompt.shape
    DB, S, _ = x_sample.shape
    MP, MS = B * T, DB * S
    assert all(c.shape[2] == w for c, (w, _) in zip((cache_k_g0, cache_k_g1, cache_k_g2), DIL_GROUPS))

    xp = x_prompt.reshape(MP, D)
    xs = x_sample.reshape(MS, D)
    fmajor = lambda c: c.transpose(0, 1, 3, 4, 2).reshape(c.shape[0], c.shape[1], DIL_WIDTH, c.shape[2])
    k_caches = tuple(fmajor(c) for c in (cache_k_g0, cache_k_g1, cache_k_g2))
    v_caches = tuple(fmajor(c) for c in (cache_v_g0, cache_v_g1, cache_v_g2))

    blockdiag = jnp.asarray(np.kron(np.eye(MXU_WIDTH // DIL_HD), np.ones((DIL_HD, DIL_HD))), BF16)
    expand1 = np.kron(np.eye(LANES, DIL_HEADS, dtype=np.float32), np.ones((1, DIL_HD), np.float32))
    expand = jnp.asarray(np.concatenate([expand1, expand1], axis=0), BF16)
    p_bias = [_prompt_bias(rel_bias, g, d) for g, (_, d) in enumerate(DIL_GROUPS)]
    s_bias = [_sample_bias(rel_bias, g, d, S) for g, (_, d) in enumerate(DIL_GROUPS)]
    s_bias_buf = [b for b, _ in s_bias]
    s_bias_new = jnp.stack([b for _, b in s_bias])

    gla_p, gla_s = [], []
    kp = [[] for _ in DIL_GROUPS]
    vp = [[] for _ in DIL_GROUPS]
    kq = [[] for _ in DIL_GROUPS]
    vq = [[] for _ in DIL_GROUPS]
    conv_p, conv_s = [], []
    cache_h, cache_gain = [], []
    SPAD = 16

    gla_w_in_b = to_bf16(gla_w_in, 256)
    gla_w_out_b = to_bf16(gla_w_out, 256)
    dil_w_in_b = to_bf16(dil_w_in, 128)
    dil_w_out_b = to_bf16(dil_w_out, 256)
    ffn_w_up_b = to_bf16(ffn_w_up, 256)
    ffn_w_down_b = to_bf16(ffn_w_down, 256)
    dil_w_in_t = jnp.swapaxes(dil_w_in_b, 1, 2)
    n_main = 2 * GLA_KD + 2 * GLA_VD

    for i in range(DEPTH):
        li = i // N_MIXERS
        gmix = norm_mix[i][None, :].astype(F32)
        if i % N_MIXERS == 0:
            wgz = jnp.pad(gla_w_in[li][:, n_main:], ((0, 0), (0, LANES - GLA_GATE_RANK))).astype(BF16)
            wg2 = jnp.pad(gla_w_gate2[li], ((0, LANES - GLA_GATE_RANK), (0, 0))).astype(BF16)
            bg = gla_b_gate[li][None, :].astype(F32)
            gn = gla_norm[li][None, :].astype(F32)
            pp, glp = gla_in_proj(xp, gmix, gla_w_in_b, li, n_main, wgz, wg2, bg, 1024, 1024)
            pp, glp = pp.reshape(B, T, -1), glp.reshape(B, T, GLA_KD)
            s0 = jnp.zeros((B, GLA_DV, GLA_HEADS * GLA_DK), F32)
            yp, stp = gla_scan(pp, glp, s0, gn, 256, GLA_CHUNK)
            xp = matmul_residual(yp.reshape(MP, GLA_VD), gla_w_out_b, li, xp, 1024, D)
            gla_p.append(_unpack_state(stp))
            ps, gls = gla_in_proj(xs, gmix, gla_w_in_b, li, n_main, wgz, wg2, bg, MS, 1024)
            ps, gls = ps.reshape(DB, S, -1), gls.reshape(DB, S, GLA_KD)
            ps = jnp.pad(ps, ((0, 0), (0, SPAD - S), (0, 0)))
            gls = jnp.pad(gls, ((0, 0), (0, SPAD - S), (0, 0)))
            ys, sts = gla_scan(ps, gls, _pack_state(state_gla[li].astype(F32)), gn, SPAD, SPAD)
            xs = matmul_residual(ys[:, :S].reshape(MS, GLA_VD), gla_w_out_b, li, xs, MS, D)
            gla_s.append(_unpack_state(sts))
        else:
            cg = _colgain(dil_q_norm[li], dil_k_norm[li], N_GROUPS)
            kgain = jnp.broadcast_to(jnp.tile(dil_k_norm[li].astype(F32), DIL_HEADS)[:, None], (DIL_WIDTH, LANES))
            os_, ls_ = [], []
            dils = tuple(d for _, d in DIL_GROUPS)
            hs = dict(zip(dils, rms_prep(xp.reshape(B, T, D), gmix, dils, 1024)))
            for g, (W, d) in enumerate(DIL_GROUPS):
                L = T // d
                hg = hs[d].reshape(MP, D)
                q = proj_rows(hg, dil_w_in_b, li, 3 * g, cg, blockdiag, True, 2048).reshape(B, d, L, DIL_WIDTH)
                v = proj_rows(hg, dil_w_in_b, li, 3 * g + 2, cg, blockdiag, False, 2048).reshape(B, d, L, DIL_WIDTH)
                kt = proj_cols(hs[d].reshape(1, MP, D), dil_w_in_t, li, 3 * g + 1, kgain, blockdiag, True, BF16,
                               1024, blocked=True).reshape(B, d, L // DIL_BLOCK, DIL_WIDTH, DIL_BLOCK)
                o, lse = dil_attention(q, kt, v, p_bias[g])
                if d > 1:
                    lse = lse.transpose(0, 2, 1, 3)
                os_.append(o)
                ls_.append(lse.reshape(B, T, LANES))
            xp = dil_out(os_, ls_, xp.reshape(B, T, D), dil_w_out_b, li, expand, 512).reshape(MP, D)
            cache_h.append(hs[1].reshape(B, T, D))
            cache_gain.append(kgain)
            qkvs = norm_matmul_qk(xs.reshape(1, MS, D), gmix, dil_w_in_b, li, cg, blockdiag, F32, MS, 1024)
            osamp = dil_sample_attention(li, qkvs.reshape(DB, S, -1), k_caches, v_caches, s_bias_buf, s_bias_new)
            qkvs = qkvs.reshape(DB, S, N_GROUPS, 3, DIL_HEADS, DIL_HD)
            for g in range(N_GROUPS):
                kq[g].append(qkvs[:, :, g, 1])
                vq[g].append(qkvs[:, :, g, 2])
            osamp = osamp.reshape(MS, DIL_WIDTH).astype(BF16)
            xs = matmul_residual(osamp, dil_w_out_b, li, xs, MS, D)

        gffn = norm_ffn[i][None, :].astype(F32)
        cw = ffn_conv_w[i].astype(F32)
        cb = ffn_conv_b[i][None, :].astype(F32)
        act, tail = ffn_up_prompt(xp.reshape(B, T, D), gffn, ffn_w_up_b, i, cw, cb, 512)
        conv_p.append(tail[:, 8 - (CONV_WIDTH - 1):])
        xp = matmul_residual(act.reshape(MP, D_FF), ffn_w_down_b, i, xp, 1024, D)
        buf = state_ffn_conv[i].astype(F32)
        zeros = jnp.zeros((DB, S - 1, 2 * D_FF), F32)
        carry1 = jnp.concatenate([buf[:, 1:2], zeros], axis=1).reshape(MS, 2 * D_FF)
        carry2 = jnp.concatenate([buf, zeros[:, 1:]], axis=1).reshape(MS, 2 * D_FF)
        acts, us = ffn_up_sample(xs, gffn, ffn_w_up_b, i, cw, cb, carry1, carry2, S)
        conv_s.append(us.reshape(DB, S, 2 * D_FF)[:, S - (CONV_WIDTH - 1):])
        xs = matmul_residual(acts, ffn_w_down_b, i, xs, MS, D)

    outs = [xp.reshape(B, T, D), xs.reshape(DB, S, D), jnp.stack(gla_p), jnp.stack(gla_s)]
    pmajor = lambda c: c.reshape(c.shape[0], B, DIL_HEADS, DIL_HD, c.shape[-1]).transpose(0, 1, 4, 2, 3)
    gains = jnp.stack(cache_gain)
    for g, (W, _) in enumerate(DIL_GROUPS):
        keep = min(W, T)
        kc = cache_rows(cache_h, dil_w_in_t, 3 * g + 1, gains, blockdiag, True, keep, min(keep, 512))
        vc = cache_rows(cache_h, dil_w_in_t, 3 * g + 2, gains, blockdiag, False, keep, min(keep, 512))
        outs += [pmajor(kc), jnp.stack(kq[g]), pmajor(vc), jnp.stack(vq[g])]
    outs += [jnp.stack(conv_p), jnp.stack(conv_s)]
    return tuple(outs)
```

```python
import functools
import math

import numpy as np
import jax
import jax.numpy as jnp
from jax import lax
from jax.experimental import pallas as pl
from jax.experimental.pallas import tpu as pltpu

F32 = jnp.float32
BF16 = jnp.bfloat16

D_MODEL = 1024
DEPTH = 4
N_MIXERS = 2
GLA_HEADS = 4
GLA_KD = 512
GLA_VD = 1024
GLA_DK = 128
GLA_DV = 256
GLA_GATE_RANK = 16
GLA_GATE_NORM = 16.0
GLA_CHUNK = 32
DIL_GROUPS = ((128, 1), (512, 4), (2048, 16))
N_GROUPS = 3
DIL_HEADS = 16
DIL_HD = 64
DIL_WIDTH = 1024
DIL_BLOCK = 128
DIL_SCALE = DIL_HD ** -0.5
NUM_BUCKETS = 32
MAX_DISTANCE = 2048
D_FF = 2816
CONV_WIDTH = 3
EPS = 1e-6
NEG = -1e30

LANES = 128
MXU_WIDTH = 256
VMEM_LIMIT = 48 * 1024 * 1024

_NT = (((1,), (1,)), ((), ()))
_TN = (((0,), (0,)), ((), ()))


def _params(*sem):
    return pltpu.CompilerParams(dimension_semantics=sem, vmem_limit_bytes=VMEM_LIMIT)


def _rms(x, g):
    return x * lax.rsqrt(jnp.mean(x * x, axis=-1, keepdims=True) + EPS) * g


def _silu(x):
    return x * (1.0 / (1.0 + jnp.exp(-x)))


def _cast_body(w_ref, o_ref):
    o_ref[...] = w_ref[...].astype(o_ref.dtype)


def to_bf16(w, rows):
    L, R, C = w.shape
    return pl.pallas_call(
        _cast_body,
        grid=(L, R // rows),
        in_specs=[pl.BlockSpec((None, rows, C), lambda l, i: (l, i, 0))],
        out_specs=pl.BlockSpec((None, rows, C), lambda l, i: (l, i, 0)),
        out_shape=jax.ShapeDtypeStruct((L, R, C), BF16),
        compiler_params=_params("parallel", "parallel"),
        name="to_bf16",
    )(w)


def _glain_body(x_ref, g_ref, w_ref, wgz_ref, wg2_ref, b_ref, o_ref, gl_ref, h_ref):
    @pl.when(pl.program_id(1) == 0)
    def _():
        h = _rms(x_ref[...], g_ref[...]).astype(BF16)
        h_ref[...] = h
        gz = jnp.dot(h, wgz_ref[...], preferred_element_type=F32)
        z = jnp.dot(gz.astype(BF16), wg2_ref[...], preferred_element_type=F32) + b_ref[...]
        gl_ref[...] = (jnp.minimum(z, 0.0) - jnp.log(1.0 + jnp.exp(-jnp.abs(z)))) * (1.0 / GLA_GATE_NORM)

    o_ref[...] = jnp.dot(h_ref[...], w_ref[...], preferred_element_type=F32).astype(o_ref.dtype)


def gla_in_proj(x, g, w, layer, ncols, wgz, wg2, b, tm, tn):
    M, D = x.shape
    return pl.pallas_call(
        _glain_body,
        grid=(M // tm, ncols // tn),
        in_specs=[pl.BlockSpec((tm, D), lambda i, j: (i, 0)),
                  pl.BlockSpec((1, D), lambda i, j: (0, 0)),
                  pl.BlockSpec((None, D, tn), lambda i, j: (layer, 0, j)),
                  pl.BlockSpec((D, LANES), lambda i, j: (0, 0)),
                  pl.BlockSpec((LANES, GLA_KD), lambda i, j: (0, 0)),
                  pl.BlockSpec((1, GLA_KD), lambda i, j: (0, 0))],
        out_specs=[pl.BlockSpec((tm, tn), lambda i, j: (i, j)),
                   pl.BlockSpec((tm, GLA_KD), lambda i, j: (i, 0))],
        out_shape=[jax.ShapeDtypeStruct((M, ncols), BF16),
                   jax.ShapeDtypeStruct((M, GLA_KD), F32)],
        scratch_shapes=[pltpu.VMEM((tm, D), BF16)],
        compiler_params=_params("parallel", "arbitrary"),
        name="gla_in_proj",
    )(x, g, w, wgz, wg2, b)


PERM_ROWS = 16
QK_ROWBLOCK = 512


def _nmqk_body(x_ref, g_ref, w_ref, cg_ref, bd_ref, perm_ref, o_ref, h_ref, *, tm, tn, d):
    j = pl.program_id(2)
    rows = tm // d

    @pl.when(j == 0)
    def _():
        h = _rms(x_ref[...], g_ref[...]).astype(BF16)
        if d == 1:
            h_ref[...] = h
        else:
            sub = PERM_ROWS * d
            for s in range(tm // sub):
                hs = jnp.dot(perm_ref[...], h[s * sub:(s + 1) * sub], preferred_element_type=F32).astype(BF16)
                for r in range(d):
                    dst = r * rows + PERM_ROWS * s
                    h_ref[dst:dst + PERM_ROWS, :] = hs[r * PERM_ROWS:(r + 1) * PERM_ROWS]

    is_norm = ((j * tn) // DIL_WIDTH) % 3 != 2

    rb = min(QK_ROWBLOCK, tm)

    def emit(a, cs, y):
        yb = y.astype(o_ref.dtype)
        for r in range(d):
            lo, hi = max(a, r * rows), min(a + rb, (r + 1) * rows)
            if lo < hi:
                o_ref[r, lo - r * rows:hi - r * rows, cs] = yb[lo - a:hi - a]

    def chunks(norm):
        blocks = [(slice(c * MXU_WIDTH, (c + 1) * MXU_WIDTH), a)
                  for c in range(tn // MXU_WIDTH) for a in range(0, tm, rb)]
        proj = lambda cs, a: jnp.dot(h_ref[a:a + rb, :], w_ref[:, cs], preferred_element_type=F32)
        ahead = proj(*blocks[0])
        for n, (cs, a) in enumerate(blocks):
            pc = ahead
            if n + 1 < len(blocks):
                ahead = proj(*blocks[n + 1])
            if norm:
                ss = jnp.dot((pc * pc).astype(BF16), bd_ref[...], preferred_element_type=F32)
                pc = pc * lax.rsqrt(ss * (1.0 / DIL_HD) + EPS) * cg_ref[:, cs]
            emit(a, cs, pc)

    @pl.when(is_norm)
    def _():
        chunks(True)

    @pl.when(jnp.logical_not(is_norm))
    def _():
        chunks(False)


def norm_matmul_qk(x, g, w, layer, colgain, bd, out_dtype, tm, tn, d=1, group=None):
    B, T, D = x.shape
    N = w.shape[2] if group is None else 3 * DIL_WIDTH
    j0 = 0 if group is None else group * N // tn
    sub = PERM_ROWS * d
    return pl.pallas_call(
        functools.partial(_nmqk_body, tm=tm, tn=tn, d=d),
        grid=(B, T // tm, N // tn),
        in_specs=[pl.BlockSpec((None, tm, D), lambda b, i, j: (b, i, 0)),
                  pl.BlockSpec((1, D), lambda b, i, j: (0, 0)),
                  pl.BlockSpec((None, D, tn), lambda b, i, j: (layer, 0, j + j0)),
                  pl.BlockSpec((1, tn), lambda b, i, j: (0, j + j0)),
                  pl.BlockSpec((MXU_WIDTH, MXU_WIDTH), lambda b, i, j: (0, 0)),
                  pl.BlockSpec((sub, sub), lambda b, i, j: (0, 0))],
        out_specs=pl.BlockSpec((None, d, tm // d, tn), lambda b, i, j: (b, 0, i, j)),
        out_shape=jax.ShapeDtypeStruct((B, d, T // d, N), out_dtype),
        scratch_shapes=[pltpu.VMEM((tm, D), BF16)],
        compiler_params=_params("parallel", "parallel", "arbitrary"),
        name="norm_matmul_qk",
    )(x, g, w, colgain, bd, _perm_matrix(d))


def _perm_matrix(d, pr=PERM_ROWS):
    sub = pr * d
    pm = np.zeros((sub, sub), np.float32)
    for r in range(d):
        for i in range(pr):
            pm[r * pr + i, i * d + r] = 1.0
    return jnp.asarray(pm, BF16)


def _prep_body(x_ref, g_ref, *refs, tm, ds):
    perm_refs, o_refs = refs[:len(ds)], refs[len(ds):]
    h = _rms(x_ref[...], g_ref[...]).astype(BF16)
    for d, perm_ref, o_ref in zip(ds, perm_refs, o_refs):
        if d == 1:
            o_ref[0] = h
            continue
        sub = PERM_ROWS * d
        for s in range(tm // sub):
            hs = jnp.dot(perm_ref[...], h[s * sub:(s + 1) * sub], preferred_element_type=F32).astype(BF16)
            for r in range(d):
                o_ref[r, PERM_ROWS * s:PERM_ROWS * (s + 1), :] = hs[r * PERM_ROWS:(r + 1) * PERM_ROWS]


def rms_prep(x, g, ds, tm):
    B, T, D = x.shape
    return pl.pallas_call(
        functools.partial(_prep_body, tm=tm, ds=ds),
        grid=(B, T // tm),
        in_specs=[pl.BlockSpec((None, tm, D), lambda b, i: (b, i, 0)),
                  pl.BlockSpec((1, D), lambda b, i: (0, 0))]
                 + [pl.BlockSpec((PERM_ROWS * d, PERM_ROWS * d), lambda b, i: (0, 0)) for d in ds],
        out_specs=[pl.BlockSpec((None, d, tm // d, D), lambda b, i: (b, 0, i, 0)) for d in ds],
        out_shape=[jax.ShapeDtypeStruct((B, d, T // d, D), BF16) for d in ds],
        compiler_params=_params("parallel", "parallel"),
        name="rms_prep",
    )(x, g, *[_perm_matrix(d) for d in ds])


def _projrows_body(h_ref, w_ref, cg_ref, bd_ref, o_ref, *, tm, norm):
    rb = min(QK_ROWBLOCK, tm)
    blocks = [(slice(c * MXU_WIDTH, (c + 1) * MXU_WIDTH), a)
              for c in range(DIL_WIDTH // MXU_WIDTH) for a in range(0, tm, rb)]
    proj = lambda cs, a: jnp.dot(h_ref[a:a + rb, :], w_ref[:, cs], preferred_element_type=F32)
    ahead = proj(*blocks[0])
    for n, (cs, a) in enumerate(blocks):
        pc = ahead
        if n + 1 < len(blocks):
            ahead = proj(*blocks[n + 1])
        if norm:
            ss = jnp.dot((pc * pc).astype(BF16), bd_ref[...], preferred_element_type=F32)
            pc = pc * lax.rsqrt(ss * (1.0 / DIL_HD) + EPS) * cg_ref[:, cs]
        o_ref[a:a + rb, cs] = pc.astype(o_ref.dtype)


def proj_rows(h, w, layer, seg, colgain, bd, norm, tm):
    M, D = h.shape
    return pl.pallas_call(
        functools.partial(_projrows_body, tm=tm, norm=norm),
        grid=(M // tm,),
        in_specs=[pl.BlockSpec((tm, D), lambda i: (i, 0)),
                  pl.BlockSpec((None, D, DIL_WIDTH), lambda i: (layer, 0, seg)),
                  pl.BlockSpec((1, DIL_WIDTH), lambda i: (0, seg)),
                  pl.BlockSpec((MXU_WIDTH, MXU_WIDTH), lambda i: (0, 0))],
        out_specs=pl.BlockSpec((tm, DIL_WIDTH), lambda i: (i, 0)),
        out_shape=jax.ShapeDtypeStruct((M, DIL_WIDTH), BF16),
        compiler_params=_params("parallel"),
        name="proj_rows",
    )(h, w, colgain, bd)


def _projcols_body(h_ref, wt_ref, cg_ref, bd_ref, o_ref, *, tl, norm, blocked):
    _projcols_core(h_ref[...], wt_ref, cg_ref, bd_ref, o_ref, tl, norm, blocked)


def _cacherows_body(ha_ref, hb_ref, wt_ref, cg_ref, bd_ref, o_ref, *, tl, norm):
    h = jnp.where(pl.program_id(0) == 0, ha_ref[...], hb_ref[...])
    _projcols_core(h, wt_ref, cg_ref, bd_ref, o_ref, tl, norm, False)


def cache_rows(hs, wt, seg, rowgains, bd, norm, keep, tl):
    B, T, D = hs[0].shape
    first, count = (T - keep) // tl, keep // tl
    hspec = pl.BlockSpec((None, tl, D), lambda l, g, i: (g, first + i, 0))
    return pl.pallas_call(
        functools.partial(_cacherows_body, tl=tl, norm=norm),
        grid=(len(hs), B, count),
        in_specs=[hspec, hspec,
                  pl.BlockSpec((None, DIL_WIDTH, D), lambda l, g, i: (l, seg, 0)),
                  pl.BlockSpec((None, DIL_WIDTH, LANES), lambda l, g, i: (l, 0, 0)),
                  pl.BlockSpec((MXU_WIDTH, MXU_WIDTH), lambda l, g, i: (0, 0))],
        out_specs=pl.BlockSpec((None, None, DIL_WIDTH, tl), lambda l, g, i: (l, g, 0, i)),
        out_shape=jax.ShapeDtypeStruct((len(hs), B, DIL_WIDTH, keep), F32),
        compiler_params=_params("parallel", "parallel", "parallel"),
        name="cache_rows",
    )(hs[0], hs[1], wt, rowgains, bd)


def _projcols_core(h, wt_ref, cg_ref, bd_ref, o_ref, tl, norm, blocked):
    kt = lax.dot_general(wt_ref[...], h, _NT, preferred_element_type=F32)
    for fb in range(DIL_WIDTH // MXU_WIDTH):
        fs = slice(fb * MXU_WIDTH, (fb + 1) * MXU_WIDTH)
        blk = kt[fs]
        if norm:
            ss = jnp.dot(bd_ref[...], (blk * blk).astype(BF16), preferred_element_type=F32)
            gain = jnp.concatenate([cg_ref[fs, :]] * (tl // LANES), axis=1)
            blk = blk * lax.rsqrt(ss * (1.0 / DIL_HD) + EPS) * gain
        blk = blk.astype(o_ref.dtype)
        if blocked:
            for jb in range(tl // DIL_BLOCK):
                o_ref[jb, fs, :] = blk[:, jb * DIL_BLOCK:(jb + 1) * DIL_BLOCK]
        else:
            o_ref[fs, :] = blk


def proj_cols(h, wt, layer, seg, rowgain, bd, norm, out_dtype, tl, first=0, count=None, blocked=False):
    G, L, D = h.shape
    count = L // tl if count is None else count
    if blocked:
        nb = tl // DIL_BLOCK
        out_spec = pl.BlockSpec((None, nb, DIL_WIDTH, DIL_BLOCK), lambda g, i: (g, i, 0, 0))
        out_shape = jax.ShapeDtypeStruct((G, count * nb, DIL_WIDTH, DIL_BLOCK), out_dtype)
    else:
        out_spec = pl.BlockSpec((None, DIL_WIDTH, tl), lambda g, i: (g, 0, i))
        out_shape = jax.ShapeDtypeStruct((G, DIL_WIDTH, count * tl), out_dtype)
    return pl.pallas_call(
        functools.partial(_projcols_body, tl=tl, norm=norm, blocked=blocked),
        grid=(G, count),
        in_specs=[pl.BlockSpec((None, tl, D), lambda g, i: (g, first + i, 0)),
                  pl.BlockSpec((None, DIL_WIDTH, D), lambda g, i: (layer, seg, 0)),
                  pl.BlockSpec((DIL_WIDTH, LANES), lambda g, i: (0, 0)),
                  pl.BlockSpec((MXU_WIDTH, MXU_WIDTH), lambda g, i: (0, 0))],
        out_specs=out_spec,
        out_shape=out_shape,
        compiler_params=_params("parallel", "parallel"),
        name="proj_cols",
    )(h, wt, rowgain, bd)


def _mmres_body(y_ref, w_ref, x_ref, o_ref):
    o_ref[...] = x_ref[...] + jnp.dot(y_ref[...], w_ref[...], preferred_element_type=F32)


def matmul_residual(y, w, layer, x, tm, tn):
    M, K = y.shape
    N = w.shape[2]
    return pl.pallas_call(
        _mmres_body,
        grid=(M // tm, N // tn),
        in_specs=[pl.BlockSpec((tm, K), lambda i, j: (i, 0)),
                  pl.BlockSpec((None, K, tn), lambda i, j: (layer, 0, j)),
                  pl.BlockSpec((tm, tn), lambda i, j: (i, j))],
        out_specs=pl.BlockSpec((tm, tn), lambda i, j: (i, j)),
        out_shape=jax.ShapeDtypeStruct((M, N), F32),
        compiler_params=_params("parallel", "parallel"),
        name="matmul_residual",
    )(y, w, x)


def _gla_body(q_ref, k_ref, v_ref, r_ref, gl_ref, s0_ref, gn_ref, tri_ref, y_ref, st_ref, S_ref, *, TB, CH):
    c = pl.program_id(1)

    @pl.when(c == 0)
    def _():
        S_ref[...] = s0_ref[...]

    gl = gl_ref[...]
    g1 = gl.astype(BF16)
    r1 = gl - g1.astype(F32)
    g2 = r1.astype(BF16)
    g3 = (r1 - g2.astype(F32)).astype(BF16)
    tri = tri_ref[...]
    bfull = (jnp.dot(tri, g1, preferred_element_type=F32)
             + jnp.dot(tri, g2, preferred_element_type=F32)
             + jnp.dot(tri, g3, preferred_element_type=F32))

    H = GLA_HEADS
    R = H * CH
    own = (lax.broadcasted_iota(jnp.int32, (R, GLA_KD), 0) // CH
           == lax.broadcasted_iota(jnp.int32, (R, GLA_KD), 1) // GLA_DK)
    arow = lax.broadcasted_iota(jnp.int32, (R, R), 0)
    acol = lax.broadcasted_iota(jnp.int32, (R, R), 1)
    amask = (arow // CH == acol // CH) & (arow % CH >= acol % CH)
    heads = lambda a: jnp.concatenate([a] * H, axis=0)
    split = lambda ref, rs: jnp.concatenate([ref[rs, h * GLA_DV:(h + 1) * GLA_DV] for h in range(H)], axis=0)
    mid = CH // 2
    gn = gn_ref[...]
    S = S_ref[...]
    for sc in range(TB // CH):
        rs = slice(sc * CH, (sc + 1) * CH)
        b = bfull[rs]
        if sc > 0:
            b = b - bfull[sc * CH - 1:sc * CH]
        ref = b[mid:mid + 1]
        blast = b[CH - 1:CH]
        qf = q_ref[rs, :].astype(F32) * (GLA_DK ** -0.5)
        kf = k_ref[rs, :].astype(F32)
        qe = (qf * jnp.exp(b - ref)).astype(BF16)
        ke = jnp.where(own, heads(kf * jnp.exp(ref - b)), 0.0).astype(BF16)
        kd = jnp.where(own, heads(kf * jnp.exp(blast - b)), 0.0).astype(BF16)
        qb = jnp.where(own, heads(qf * jnp.exp(b)), 0.0).astype(BF16)
        vst = split(v_ref, rs)
        a = heads(lax.dot_general(qe, ke, _NT, preferred_element_type=F32))
        a = jnp.where(amask, a, 0.0).astype(BF16)
        o = (jnp.dot(a, vst, preferred_element_type=F32)
             + lax.dot_general(qb, S.astype(BF16), _NT, preferred_element_type=F32))
        S = jnp.exp(blast) * S + lax.dot_general(vst, kd, _TN, preferred_element_type=F32)
        y = (_rms(o, gn) * _silu(split(r_ref, rs).astype(F32))).astype(y_ref.dtype)
        for h in range(H):
            y_ref[rs, h * GLA_DV:(h + 1) * GLA_DV] = y[h * CH:(h + 1) * CH]
    S_ref[...] = S

    @pl.when(c == pl.num_programs(1) - 1)
    def _():
        st_ref[...] = S_ref[...]


def _pack_state(s):
    return s.transpose(0, 3, 1, 2).reshape(s.shape[0], GLA_DV, GLA_HEADS * GLA_DK)


def _unpack_state(s):
    return s.reshape(s.shape[0], GLA_DV, GLA_HEADS, GLA_DK).transpose(0, 2, 3, 1)


def gla_scan(p, glog, s0t, gn, TB, CH):
    B, T, _ = p.shape
    sblk = (None, GLA_DV, GLA_HEADS * GLA_DK)
    tri = jnp.asarray(np.tril(np.ones((TB, TB), np.float32)), BF16)
    return pl.pallas_call(
        functools.partial(_gla_body, TB=TB, CH=CH),
        grid=(B, T // TB),
        in_specs=[pl.BlockSpec((None, TB, GLA_KD), lambda b, c: (b, c, 0)),
                  pl.BlockSpec((None, TB, GLA_KD), lambda b, c: (b, c, 1)),
                  pl.BlockSpec((None, TB, GLA_VD), lambda b, c: (b, c, 1)),
                  pl.BlockSpec((None, TB, GLA_VD), lambda b, c: (b, c, 2)),
                  pl.BlockSpec((None, TB, GLA_KD), lambda b, c: (b, c, 0)),
                  pl.BlockSpec(sblk, lambda b, c: (b, 0, 0)),
                  pl.BlockSpec((1, GLA_DV), lambda b, c: (0, 0)),
                  pl.BlockSpec((TB, TB), lambda b, c: (0, 0))],
        out_specs=[pl.BlockSpec((None, TB, GLA_VD), lambda b, c: (b, c, 0)),
                   pl.BlockSpec(sblk, lambda b, c: (b, 0, 0))],
        out_shape=[jax.ShapeDtypeStruct((B, T, GLA_VD), BF16),
                   jax.ShapeDtypeStruct((B,) + sblk[1:], F32)],
        scratch_shapes=[pltpu.VMEM(sblk[1:], F32)],
        compiler_params=_params("parallel", "arbitrary"),
        name="gla_scan",
    )(p, p, p, p, glog, s0t, gn, tri)


ATTN_QBLOCKS = 4


def _attn_body(q_ref, *refs, nq):
    krefs = refs[:nq + 1]
    vrefs = refs[nq + 1:2 * nq + 2]
    bias_ref, o_ref, lse_ref = refs[2 * nq + 2:]
    lane = lax.broadcasted_iota(jnp.int32, (DIL_BLOCK, LANES), 1)
    lo = lane < DIL_HD
    zero = jnp.zeros((), BF16)
    for sb in range(nq):
        rows = slice(sb * DIL_BLOCK, (sb + 1) * DIL_BLOCK)
        kprev, kcur = krefs[sb], krefs[sb + 1]
        vprev, vcur = vrefs[sb], vrefs[sb + 1]
        var = jnp.where(pl.program_id(2) == 0, 0, 1) if sb == 0 else 1

        def scores(hp):
            cs = slice(hp * LANES, (hp + 1) * LANES)
            q2 = q_ref[rows, cs]
            qab = jnp.concatenate([jnp.where(lo, q2, zero), jnp.where(lo, zero, q2)], axis=0)
            k2t = jnp.concatenate([kprev[cs, :], kcur[cs, :]], axis=1)
            return jnp.dot(qab, k2t, preferred_element_type=F32)

        lse_acc = jnp.zeros((DIL_BLOCK, LANES), F32)
        ahead = scores(0)
        for hp in range(DIL_HEADS // 2):
            cs = slice(hp * LANES, (hp + 1) * LANES)
            s = ahead + bias_ref[var, hp]
            if hp + 1 < DIL_HEADS // 2:
                ahead = scores(hp + 1)
            v2 = jnp.concatenate([vprev[:, cs], vcur[:, cs]], axis=0)
            m = jnp.max(s, axis=-1, keepdims=True)
            p = jnp.exp(s - m)
            l = jnp.sum(p, axis=-1, keepdims=True)
            o = jnp.dot(p.astype(BF16), v2, preferred_element_type=F32) * (1.0 / l)
            o_ref[rows, cs] = jnp.where(lo, o[:DIL_BLOCK], o[DIL_BLOCK:]).astype(o_ref.dtype)
            lse = m + jnp.log(l)
            lse_acc = jnp.where(lane == 2 * hp, lse[:DIL_BLOCK], lse_acc)
            lse_acc = jnp.where(lane == 2 * hp + 1, lse[DIL_BLOCK:], lse_acc)
        lse_ref[rows, :] = lse_acc


def dil_attention(q, kt, v, bias):
    B, d, L, _ = q.shape
    nb = L // DIL_BLOCK
    nq = math.gcd(ATTN_QBLOCKS, nb)
    qblk = (None, None, nq * DIL_BLOCK, DIL_WIDTH)
    blk = (None, None, DIL_BLOCK, DIL_WIDTH)
    tblk = (None, None, None, DIL_WIDTH, DIL_BLOCK)
    tile = lambda t: (lambda n: jnp.maximum(nq * n - 1, 0)) if t == 0 else (lambda n: nq * n + t - 1)
    kspecs = [pl.BlockSpec(tblk, lambda b, r, n, f=tile(t): (b, r, f(n), 0, 0)) for t in range(nq + 1)]
    vspecs = [pl.BlockSpec(blk, lambda b, r, n, f=tile(t): (b, r, f(n), 0)) for t in range(nq + 1)]
    return pl.pallas_call(
        functools.partial(_attn_body, nq=nq),
        grid=(B, d, nb // nq),
        in_specs=[pl.BlockSpec(qblk, lambda b, r, n: (b, r, n, 0))] + kspecs + vspecs
                 + [pl.BlockSpec((2, DIL_HEADS // 2, 2 * DIL_BLOCK, 2 * DIL_BLOCK), lambda b, r, n: (0, 0, 0, 0))],
        out_specs=[pl.BlockSpec(qblk, lambda b, r, n: (b, r, n, 0)),
                   pl.BlockSpec((None, None, nq * DIL_BLOCK, LANES), lambda b, r, n: (b, r, n, 0))],
        out_shape=[jax.ShapeDtypeStruct((B, d, L, DIL_WIDTH), BF16),
                   jax.ShapeDtypeStruct((B, d, L, LANES), F32)],
        compiler_params=_params("parallel", "parallel", "arbitrary"),
        name="dil_attention",
    )(q, *([kt] * (nq + 1)), *([v] * (nq + 1)), bias)


UNPERM_SUB = MXU_WIDTH


def _dilout_body(o0_ref, o1_ref, o2_ref, l0_ref, l1_ref, l2_ref, x_ref, w_ref, e_ref, p1_ref, p2_ref, out_ref,
                 *, tm):
    def position_order(o_ref, pt_ref, d):
        if d == 1:
            return o_ref[0].astype(F32)
        pr = UNPERM_SUB // d
        out = []
        for s in range(tm // UNPERM_SUB):
            blk = jnp.concatenate([o_ref[r, pr * s:pr * (s + 1), :] for r in range(d)], axis=0)
            out.append(jnp.dot(pt_ref[...], blk, preferred_element_type=F32))
        return jnp.concatenate(out, axis=0)

    dils = [d for _, d in DIL_GROUPS]
    o0 = position_order(o0_ref, None, dils[0])
    o1 = position_order(o1_ref, p1_ref, dils[1])
    o2 = position_order(o2_ref, p2_ref, dils[2])
    l0, l1, l2 = l0_ref[...], l1_ref[...], l2_ref[...]
    mx = jnp.maximum(jnp.maximum(l0, l1), l2)
    e0, e1, e2 = jnp.exp(l0 - mx), jnp.exp(l1 - mx), jnp.exp(l2 - mx)
    den = e0 + e1 + e2
    ex = e_ref[...]

    def expand(w):
        w1 = w.astype(BF16)
        w2 = (w - w1.astype(F32)).astype(BF16)
        return jnp.dot(jnp.concatenate([w1, w2], axis=1), ex, preferred_element_type=F32)

    o = expand(e0 / den) * o0 + expand(e1 / den) * o1 + expand(e2 / den) * o2
    out_ref[...] = x_ref[...] + jnp.dot(o.astype(BF16), w_ref[...], preferred_element_type=F32)


def dil_out(os, ls, x, w, layer, expand, tm):
    B, T, D = x.shape
    dils = [d for _, d in DIL_GROUPS]
    ospecs = [pl.BlockSpec((None, d, tm // d, DIL_WIDTH), lambda b, i: (b, 0, i, 0)) for d in dils]
    lspec = pl.BlockSpec((None, tm, LANES), lambda b, i: (b, i, 0))
    pts = [jnp.transpose(_perm_matrix(d, UNPERM_SUB // d)) for d in dils[1:]]
    return pl.pallas_call(
        functools.partial(_dilout_body, tm=tm),
        grid=(B, T // tm),
        in_specs=ospecs + [lspec, lspec, lspec,
                           pl.BlockSpec((None, tm, D), lambda b, i: (b, i, 0)),
                           pl.BlockSpec((None, DIL_WIDTH, D_MODEL), lambda b, i: (layer, 0, 0)),
                           pl.BlockSpec((2 * LANES, DIL_WIDTH), lambda b, i: (0, 0))]
                 + [pl.BlockSpec(p.shape, lambda b, i: (0, 0)) for p in pts],
        out_specs=pl.BlockSpec((None, tm, D_MODEL), lambda b, i: (b, i, 0)),
        out_shape=jax.ShapeDtypeStruct((B, T, D_MODEL), F32),
        compiler_params=_params("parallel", "parallel"),
        name="dil_out",
    )(*os, *ls, x, w, expand, *pts)


S_CHUNK = 1024
S_NEWPAD = LANES


def _sattn_body(qkv_ref, k0_ref, v0_ref, k1_ref, v1_ref, k2_ref, v2_ref, b0_ref, b1_ref, b2_ref, bn_ref,
                o_ref, m_ref, l_ref, acc_ref, *, seq):
    c = pl.program_id(1)
    R = seq * DIL_HEADS
    rowh = lax.broadcasted_iota(jnp.int32, (R, DIL_WIDTH), 0) % DIL_HEADS
    colh = lax.broadcasted_iota(jnp.int32, (R, DIL_WIDTH), 1) // DIL_HD
    hmask = rowh == colh

    def seg(g, part):
        return qkv_ref[:, (3 * g + part) * DIL_WIDTH:(3 * g + part + 1) * DIL_WIDTH]

    def qbd(g):
        q = seg(g, 0)
        qrep = jnp.concatenate([jnp.broadcast_to(q[s:s + 1], (DIL_HEADS, DIL_WIDTH)) for s in range(seq)], axis=0)
        return jnp.where(hmask, qrep, 0.0).astype(BF16)

    def update(parts):
        m_old = m_ref[...]
        m_new = m_old
        for s, _ in parts:
            m_new = jnp.maximum(m_new, jnp.max(s, axis=-1, keepdims=True))
        alpha = jnp.exp(m_old - m_new)
        l = alpha * l_ref[...]
        acc = alpha * acc_ref[...]
        for s, pv_fn in parts:
            p = jnp.exp(s - m_new)
            l = l + jnp.sum(p, axis=-1, keepdims=True)
            acc = acc + pv_fn(p.astype(BF16))
        m_ref[...] = m_new
        l_ref[...] = l
        acc_ref[...] = acc

    def cache_segment(g, kt_ref, vt_ref, b_ref):
        s = jnp.dot(qbd(g), kt_ref[...].astype(BF16), preferred_element_type=F32) + b_ref[...]
        return s, lambda p: lax.dot_general(p, vt_ref[...].astype(BF16), _NT, preferred_element_type=F32)

    def new_segment(g):
        pad = jnp.zeros((S_NEWPAD - seq, DIL_WIDTH), F32)
        kn = jnp.concatenate([seg(g, 1), pad], axis=0).astype(BF16)
        vn = jnp.concatenate([seg(g, 2), pad], axis=0).astype(BF16)
        s = lax.dot_general(qbd(g), kn, _NT, preferred_element_type=F32) + bn_ref[g]
        return s, lambda p: jnp.dot(p, vn, preferred_element_type=F32)

    @pl.when(c == 0)
    def _():
        m_ref[...] = jnp.full(m_ref.shape, 2 * NEG, F32)
        l_ref[...] = jnp.zeros(l_ref.shape, F32)
        acc_ref[...] = jnp.zeros(acc_ref.shape, F32)
        update([new_segment(g) for g in range(N_GROUPS)]
               + [cache_segment(0, k0_ref, v0_ref, b0_ref), cache_segment(1, k1_ref, v1_ref, b1_ref),
                  cache_segment(2, k2_ref, v2_ref, b2_ref)])

    @pl.when(c > 0)
    def _():
        update([cache_segment(2, k2_ref, v2_ref, b2_ref)])

    @pl.when(c == pl.num_programs(1) - 1)
    def _():
        on = jnp.where(hmask, acc_ref[...] / l_ref[...], 0.0).astype(BF16)
        srow = lax.broadcasted_iota(jnp.int32, (8, R), 0)
        scol = lax.broadcasted_iota(jnp.int32, (8, R), 1) // DIL_HEADS
        sel = jnp.where(srow == scol, 1.0, 0.0).astype(BF16)
        o_ref[...] = jnp.dot(sel, on, preferred_element_type=F32)[:seq]


def dil_sample_attention(li, qkv, kts, vts, bbs, bn):
    DB, S, _ = qkv.shape
    R = DIL_HEADS * S
    W2 = kts[2].shape[-1]
    in_specs = [pl.BlockSpec((None, S, qkv.shape[-1]), lambda b, c: (b, 0, 0))]
    args = [qkv]
    for g in range(2):
        for a in (kts[g], vts[g]):
            in_specs.append(pl.BlockSpec((None, None, DIL_WIDTH, a.shape[-1]), lambda b, c: (li, b, 0, 0)))
            args.append(a)
    for a in (kts[2], vts[2]):
        in_specs.append(pl.BlockSpec((None, None, DIL_WIDTH, S_CHUNK), lambda b, c: (li, b, 0, c)))
        args.append(a)
    for g in range(2):
        in_specs.append(pl.BlockSpec(bbs[g].shape, lambda b, c: (0, 0)))
        args.append(bbs[g])
    in_specs.append(pl.BlockSpec((R, S_CHUNK), lambda b, c: (0, c)))
    args.append(bbs[2])
    in_specs.append(pl.BlockSpec(bn.shape, lambda b, c: (0, 0, 0)))
    args.append(bn)
    return pl.pallas_call(
        functools.partial(_sattn_body, seq=S),
        grid=(DB, W2 // S_CHUNK),
        in_specs=in_specs,
        out_specs=pl.BlockSpec((None, S, DIL_WIDTH), lambda b, c: (b, 0, 0)),
        out_shape=jax.ShapeDtypeStruct((DB, S, DIL_WIDTH), F32),
        scratch_shapes=[pltpu.VMEM((R, 1), F32), pltpu.VMEM((R, 1), F32), pltpu.VMEM((R, DIL_WIDTH), F32)],
        compiler_params=_params("parallel", "arbitrary"),
        name="dil_sample_attention",
    )(*args)


FFN_CHUNK = MXU_WIDTH
HALO = 16


FFN_ROWBLOCK = 128


def _causal_conv(ext, cw, cb):
    u1 = pltpu.roll(ext, 1, axis=0)[HALO:]
    u2 = pltpu.roll(ext, 2, axis=0)[HALO:]
    return cb + (cw[0:1] * u2 + cw[1:2] * u1 + cw[2:3] * ext[HALO:])


def _ffnup_body(x_ref, xh_ref, g_ref, wg_ref, wv_ref, cw_ref, cb_ref, act_ref, tail_ref, h_ref, *, tm):
    first = pl.program_id(1) == 0
    hh = _rms(xh_ref[...], g_ref[...])
    h_ref[0:HALO] = jnp.where(first, 0.0, hh).astype(BF16)
    h_ref[HALO:HALO + tm] = _rms(x_ref[...], g_ref[...]).astype(BF16)
    for c in range(D_FF // FFN_CHUNK):
        cols = (slice(c * FFN_CHUNK, (c + 1) * FFN_CHUNK),
                slice(D_FF + c * FFN_CHUNK, D_FF + (c + 1) * FFN_CHUNK))
        tails = [None, None]
        for a in range(0, tm, FFN_ROWBLOCK):
            conv = []
            for idx, w_ref in enumerate((wg_ref, wv_ref)):
                if a == 0:
                    ext = jnp.dot(h_ref[0:HALO + FFN_ROWBLOCK], w_ref[:, cols[0]], preferred_element_type=F32)
                else:
                    u = jnp.dot(h_ref[HALO + a:HALO + a + FFN_ROWBLOCK], w_ref[:, cols[0]],
                                preferred_element_type=F32)
                    ext = jnp.concatenate([tails[idx], u], axis=0)
                tails[idx] = ext[FFN_ROWBLOCK:]
                conv.append(_causal_conv(ext, cw_ref[:, cols[idx]], cb_ref[:, cols[idx]]))
            act_ref[a:a + FFN_ROWBLOCK, cols[0]] = (_silu(conv[0]) * conv[1]).astype(act_ref.dtype)
        for idx in range(2):
            tail_ref[:, cols[idx]] = tails[idx][HALO - 8:]


def ffn_up_prompt(x, g, w, layer, cw, cb, tm):
    B, T, D = x.shape
    hblk = tm // HALO
    return pl.pallas_call(
        functools.partial(_ffnup_body, tm=tm),
        grid=(B, T // tm),
        in_specs=[pl.BlockSpec((None, tm, D), lambda b, i: (b, i, 0)),
                  pl.BlockSpec((None, HALO, D), lambda b, i: (b, jnp.maximum(i * hblk - 1, 0), 0)),
                  pl.BlockSpec((1, D), lambda b, i: (0, 0)),
                  pl.BlockSpec((None, D, D_FF), lambda b, i: (layer, 0, 0)),
                  pl.BlockSpec((None, D, D_FF), lambda b, i: (layer, 0, 1)),
                  pl.BlockSpec((CONV_WIDTH, 2 * D_FF), lambda b, i: (0, 0)),
                  pl.BlockSpec((1, 2 * D_FF), lambda b, i: (0, 0))],
        out_specs=[pl.BlockSpec((None, tm, D_FF), lambda b, i: (b, i, 0)),
                   pl.BlockSpec((None, 8, 2 * D_FF), lambda b, i: (b, 0, 0))],
        out_shape=[jax.ShapeDtypeStruct((B, T, D_FF), BF16),
                   jax.ShapeDtypeStruct((B, 8, 2 * D_FF), F32)],
        scratch_shapes=[pltpu.VMEM((HALO + tm, D), BF16)],
        compiler_params=_params("parallel", "arbitrary"),
        name="ffn_up_prompt",
    )(x, x, g, w, w, cw, cb)


def _ffnup_s_body(x_ref, g_ref, wg_ref, wv_ref, cw_ref, cb_ref, c1_ref, c2_ref, act_ref, u_ref, u_scr, *, rows, seq):
    h = _rms(x_ref[...], g_ref[...]).astype(BF16)
    pos = lax.broadcasted_iota(jnp.int32, (rows, FFN_CHUNK), 0) % seq
    u_scr[:, 0:HALO] = jnp.zeros((2, HALO, FFN_CHUNK), F32)
    for c in range(D_FF // FFN_CHUNK):
        acts = []
        for idx, off in enumerate((0, D_FF)):
            cs = slice(off + c * FFN_CHUNK, off + (c + 1) * FFN_CHUNK)
            w_ref = wg_ref if idx == 0 else wv_ref
            u = jnp.dot(h, w_ref[:, c * FFN_CHUNK:(c + 1) * FFN_CHUNK], preferred_element_type=F32)
            u_scr[idx, HALO:HALO + rows] = u
            u1 = jnp.where(pos >= 1, u_scr[idx, HALO - 1:HALO - 1 + rows], c1_ref[:, cs])
            u2 = jnp.where(pos >= 2, u_scr[idx, HALO - 2:HALO - 2 + rows], c2_ref[:, cs])
            acts.append(cb_ref[:, cs] + (cw_ref[0:1, cs] * u2 + cw_ref[1:2, cs] * u1 + cw_ref[2:3, cs] * u))
            u_ref[:, cs] = u
        act_ref[:, c * FFN_CHUNK:(c + 1) * FFN_CHUNK] = (_silu(acts[0]) * acts[1]).astype(act_ref.dtype)


def ffn_up_sample(x, g, w, layer, cw, cb, carry1, carry2, seq):
    M, D = x.shape
    full = lambda shape: pl.BlockSpec(shape, lambda i: tuple(0 for _ in shape))
    return pl.pallas_call(
        functools.partial(_ffnup_s_body, rows=M, seq=seq),
        grid=(1,),
        in_specs=[full((M, D)), full((1, D)),
                  pl.BlockSpec((None, D, D_FF), lambda i: (layer, 0, 0)),
                  pl.BlockSpec((None, D, D_FF), lambda i: (layer, 0, 1)),
                  full((CONV_WIDTH, 2 * D_FF)), full((1, 2 * D_FF)),
                  full((M, 2 * D_FF)), full((M, 2 * D_FF))],
        out_specs=[full((M, D_FF)), full((M, 2 * D_FF))],
        out_shape=[jax.ShapeDtypeStruct((M, D_FF), BF16),
                   jax.ShapeDtypeStruct((M, 2 * D_FF), F32)],
        scratch_shapes=[pltpu.VMEM((2, HALO + M, FFN_CHUNK), F32)],
        compiler_params=_params("arbitrary"),
        name="ffn_up_sample",
    )(x, g, w, w, cw, cb, carry1, carry2)


def _rel_bucket(dist):
    max_exact = NUM_BUCKETS // 2
    df = jnp.maximum(dist, 1).astype(F32)
    large = max_exact + (jnp.log(df / max_exact) / math.log(MAX_DISTANCE / max_exact)
                         * (NUM_BUCKETS - max_exact)).astype(jnp.int32)
    large = jnp.minimum(large, NUM_BUCKETS - 1)
    return jnp.where(dist < max_exact, dist, large)


def _step_table(rel_bias, g, d):
    J = DIL_GROUPS[g][0] // d
    tab = rel_bias[:, g * DIL_HEADS:(g + 1) * DIL_HEADS].astype(F32)
    return tab[_rel_bucket(jnp.arange(J + 1) * d)]


def _lookup(table, idx):
    onehot = jax.nn.one_hot(jnp.asarray(idx, jnp.int32), table.shape[0], dtype=F32)
    return jnp.einsum('...j,jh->...h', onehot, table, precision=lax.Precision.HIGHEST)


def _prompt_bias(rel_bias, g, d):
    J = DIL_GROUPS[g][0] // d
    qi = np.arange(DIL_BLOCK)[:, None]
    ki = np.arange(2 * DIL_BLOCK)[None, :]
    rel = qi + DIL_BLOCK - ki
    bias = jnp.moveaxis(_lookup(_step_table(rel_bias, g, d), np.clip(rel, 0, J)), -1, 0)
    out = []
    for first in (True, False):
        valid = (rel >= 0) & (rel <= J) & ((ki >= DIL_BLOCK) | (not first))
        b = jnp.where(jnp.asarray(valid)[None], bias, NEG)
        out.append(b.reshape(DIL_HEADS // 2, 2 * DIL_BLOCK, 2 * DIL_BLOCK))
    return jnp.stack(out)


def _sample_bias(rel_bias, g, d, seq):
    W = DIL_GROUPS[g][0]
    J = W // d
    table = _step_table(rel_bias, g, d)
    s = np.arange(seq)[:, None]
    dist = W + s - np.arange(W)[None, :]
    valid = (dist >= 0) & (dist <= W) & (dist % d == 0)
    bval = jnp.moveaxis(_lookup(table, np.where(valid, dist // d, 0)), -1, 1)
    bbuf = jnp.where(jnp.asarray(valid)[:, None, :], bval, NEG).reshape(seq * DIL_HEADS, W)
    dn = s - np.arange(S_NEWPAD)[None, :]
    vn = (dn >= 0) & (dn % d == 0) & (dn // d <= J) & (np.arange(S_NEWPAD)[None, :] < seq)
    bnew = jnp.moveaxis(_lookup(table, np.where(vn, dn // d, 0)), -1, 1)
    bnew = jnp.where(jnp.asarray(vn)[:, None, :], bnew, NEG).reshape(seq * DIL_HEADS, S_NEWPAD)
    return bbuf, bnew


def _colgain(q_gain, k_gain, ngroups):
    seg = jnp.concatenate([jnp.tile(q_gain.astype(F32) * DIL_SCALE, DIL_HEADS),
                           jnp.tile(k_gain.astype(F32), DIL_HEADS),
                           jnp.ones((DIL_WIDTH,), F32)])
    return jnp.tile(seg, ngroups)[None, :]


def kernel(x_prompt, x_sample, state_gla, cache_k_g0, cache_v_g0, cache_k_g1, cache_v_g1, cache_k_g2, cache_v_g2,
           state_ffn_conv, rel_bias, norm_mix, norm_ffn, gla_w_in, gla_w_gate2, gla_b_gate, gla_norm, gla_w_out,
           dil_w_in, dil_q_norm, dil_k_norm, dil_w_out, ffn_w_up, ffn_conv_w, ffn_conv_b, ffn_w_down):
    B, T, D = x_prompt.shape
    DB, S, _ = x_sample.shape
    MP, MS = B * T, DB * S
    assert all(c.shape[2] == w for c, (w, _) in zip((cache_k_g0, cache_k_g1, cache_k_g2), DIL_GROUPS))

    xp = x_prompt.reshape(MP, D)
    xs = x_sample.reshape(MS, D)
    fmajor = lambda c: c.transpose(0, 1, 3, 4, 2).reshape(c.shape[0], c.shape[1], DIL_WIDTH, c.shape[2])
    k_caches = tuple(fmajor(c) for c in (cache_k_g0, cache_k_g1, cache_k_g2))
    v_caches = tuple(fmajor(c) for c in (cache_v_g0, cache_v_g1, cache_v_g2))

    blockdiag = jnp.asarray(np.kron(np.eye(MXU_WIDTH // DIL_HD), np.ones((DIL_HD, DIL_HD))), BF16)
    expand1 = np.kron(np.eye(LANES, DIL_HEADS, dtype=np.float32), np.ones((1, DIL_HD), np.float32))
    expand = jnp.asarray(np.concatenate([expand1, expand1], axis=0), BF16)
    p_bias = [_prompt_bias(rel_bias, g, d) for g, (_, d) in enumerate(DIL_GROUPS)]
    s_bias = [_sample_bias(rel_bias, g, d, S) for g, (_, d) in enumerate(DIL_GROUPS)]
    s_bias_buf = [b for b, _ in s_bias]
    s_bias_new = jnp.stack([b for _, b in s_bias])

    gla_p, gla_s = [], []
    kp = [[] for _ in DIL_GROUPS]
    vp = [[] for _ in DIL_GROUPS]
    kq = [[] for _ in DIL_GROUPS]
    vq = [[] for _ in DIL_GROUPS]
    conv_p, conv_s = [], []
    cache_h, cache_gain = [], []
    SPAD = 16

    gla_w_in_b = to_bf16(gla_w_in, 256)
    gla_w_out_b = to_bf16(gla_w_out, 256)
    dil_w_in_b = to_bf16(dil_w_in, 128)
    dil_w_out_b = to_bf16(dil_w_out, 256)
    ffn_w_up_b = to_bf16(ffn_w_up, 256)
    ffn_w_down_b = to_bf16(ffn_w_down, 256)
    dil_w_in_t = jnp.swapaxes(dil_w_in_b, 1, 2)
    n_main = 2 * GLA_KD + 2 * GLA_VD

    for i in range(DEPTH):
        li = i // N_MIXERS
        gmix = norm_mix[i][None, :].astype(F32)
        if i % N_MIXERS == 0:
            wgz = jnp.pad(gla_w_in[li][:, n_main:], ((0, 0), (0, LANES - GLA_GATE_RANK))).astype(BF16)
            wg2 = jnp.pad(gla_w_gate2[li], ((0, LANES - GLA_GATE_RANK), (0, 0))).astype(BF16)
            bg = gla_b_gate[li][None, :].astype(F32)
            gn = gla_norm[li][None, :].astype(F32)
            pp, glp = gla_in_proj(xp, gmix, gla_w_in_b, li, n_main, wgz, wg2, bg, 1024, 1024)
            pp, glp = pp.reshape(B, T, -1), glp.reshape(B, T, GLA_KD)
            s0 = jnp.zeros((B, GLA_DV, GLA_HEADS * GLA_DK), F32)
            yp, stp = gla_scan(pp, glp, s0, gn, 256, GLA_CHUNK)
            xp = matmul_residual(yp.reshape(MP, GLA_VD), gla_w_out_b, li, xp, 1024, D)
            gla_p.append(_unpack_state(stp))
            ps, gls = gla_in_proj(xs, gmix, gla_w_in_b, li, n_main, wgz, wg2, bg, MS, 1024)
            ps, gls = ps.reshape(DB, S, -1), gls.reshape(DB, S, GLA_KD)
            ps = jnp.pad(ps, ((0, 0), (0, SPAD - S), (0, 0)))
            gls = jnp.pad(gls, ((0, 0), (0, SPAD - S), (0, 0)))
            ys, sts = gla_scan(ps, gls, _pack_state(state_gla[li].astype(F32)), gn, SPAD, SPAD)
            xs = matmul_residual(ys[:, :S].reshape(MS, GLA_VD), gla_w_out_b, li, xs, MS, D)
            gla_s.append(_unpack_state(sts))
        else:
            cg = _colgain(dil_q_norm[li], dil_k_norm[li], N_GROUPS)
            kgain = jnp.broadcast_to(jnp.tile(dil_k_norm[li].astype(F32), DIL_HEADS)[:, None], (DIL_WIDTH, LANES))
            os_, ls_ = [], []
            dils = tuple(d for _, d in DIL_GROUPS)
            hs = dict(zip(dils, rms_prep(xp.reshape(B, T, D), gmix, dils, 1024)))
            for g, (W, d) in enumerate(DIL_GROUPS):
                L = T // d
                hg = hs[d].reshape(MP, D)
                q = proj_rows(hg, dil_w_in_b, li, 3 * g, cg, blockdiag, True, 2048).reshape(B, d, L, DIL_WIDTH)
                v = proj_rows(hg, dil_w_in_b, li, 3 * g + 2, cg, blockdiag, False, 2048).reshape(B, d, L, DIL_WIDTH)
                kt = proj_cols(hs[d].reshape(1, MP, D), dil_w_in_t, li, 3 * g + 1, kgain, blockdiag, True, BF16,
                               1024, blocked=True).reshape(B, d, L // DIL_BLOCK, DIL_WIDTH, DIL_BLOCK)
                o, lse = dil_attention(q, kt, v, p_bias[g])
                if d > 1:
                    lse = lse.transpose(0, 2, 1, 3)
                os_.append(o)
                ls_.append(lse.reshape(B, T, LANES))
            xp = dil_out(os_, ls_, xp.reshape(B, T, D), dil_w_out_b, li, expand, 512).reshape(MP, D)
            cache_h.append(hs[1].reshape(B, T, D))
            cache_gain.append(kgain)
            qkvs = norm_matmul_qk(xs.reshape(1, MS, D), gmix, dil_w_in_b, li, cg, blockdiag, F32, MS, 1024)
            osamp = dil_sample_attention(li, qkvs.reshape(DB, S, -1), k_caches, v_caches, s_bias_buf, s_bias_new)
            qkvs = qkvs.reshape(DB, S, N_GROUPS, 3, DIL_HEADS, DIL_HD)
            for g in range(N_GROUPS):
                kq[g].append(qkvs[:, :, g, 1])
                vq[g].append(qkvs[:, :, g, 2])
            osamp = osamp.reshape(MS, DIL_WIDTH).astype(BF16)
            xs = matmul_residual(osamp, dil_w_out_b, li, xs, MS, D)

        gffn = norm_ffn[i][None, :].astype(F32)
        cw = ffn_conv_w[i].astype(F32)
        cb = ffn_conv_b[i][None, :].astype(F32)
        act, tail = ffn_up_prompt(xp.reshape(B, T, D), gffn, ffn_w_up_b, i, cw, cb, 512)
        conv_p.append(tail[:, 8 - (CONV_WIDTH - 1):])
        xp = matmul_residual(act.reshape(MP, D_FF), ffn_w_down_b, i, xp, 1024, D)
        buf = state_ffn_conv[i].astype(F32)
        zeros = jnp.zeros((DB, S - 1, 2 * D_FF), F32)
        carry1 = jnp.concatenate([buf[:, 1:2], zeros], axis=1).reshape(MS, 2 * D_FF)
        carry2 = jnp.concatenate([buf, zeros[:, 1:]], axis=1).reshape(MS, 2 * D_FF)
        acts, us = ffn_up_sample(xs, gffn, ffn_w_up_b, i, cw, cb, carry1, carry2, S)
        conv_s.append(us.reshape(DB, S, 2 * D_FF)[:, S - (CONV_WIDTH - 1):])
        xs = matmul_residual(acts, ffn_w_down_b, i, xs, MS, D)

    outs = [xp.reshape(B, T, D), xs.reshape(DB, S, D), jnp.stack(gla_p), jnp.stack(gla_s)]
    pmajor = lambda c: c.reshape(c.shape[0], B, DIL_HEADS, DIL_HD, c.shape[-1]).transpose(0, 1, 4, 2, 3)
    gains = jnp.stack(cache_gain)
    for g, (W, _) in enumerate(DIL_GROUPS):
        keep = min(W, T)
        kc = cache_rows(cache_h, dil_w_in_t, 3 * g + 1, gains, blockdiag, True, keep, min(keep, 512))
        vc = cache_rows(cache_h, dil_w_in_t, 3 * g + 2, gains, blockdiag, False, keep, min(keep, 512))
        outs += [pmajor(kc), jnp.stack(kq[g]), pmajor(vc), jnp.stack(vq[g])]
    outs += [jnp.stack(conv_p), jnp.stack(conv_s)]
    return tuple(outs)
```
